```python
import math
import jax, jax.numpy as jnp
from jax import lax
import numpy as np

D_MODEL = 1024
BATCH = 4
SEQ = 4096
DEPTH = 2
DEC_BATCH = 32
DEC_SEQ = 8
PAST_LEN = 8192
PAGE_SIZE = 128

CONV_C = D_MODEL // 4
CONV_W = 31
NSA_H = 8
NSA_DH = D_MODEL // 16
NSA_G = 2
NSA_HPG = NSA_H // NSA_G
CMP_STRIDE = 16
CMP_LEN = 2 * CMP_STRIDE
CMP_HID = 2 * NSA_DH
SEL_BLK = 64
SEL_TOPN = 16
WINDOW = 512
FORCE_SCORE = 1e9
DIFF_H = 4
DIFF_DV = D_MODEL // 16
DIFF_DQK = DIFF_DV // 2
D_FF = -(-8 * D_MODEL // (3 * 256)) * 256
ROPE_THETA = 10000.0
QBLK = 128
LN_EPS = 1e-5
DN_ALPHA = (2 * DEPTH) ** 0.25
DN_BETA = (8 * DEPTH) ** -0.25

IN_SPLITS = (CONV_C, CONV_C, NSA_H * NSA_DH,
             NSA_G * NSA_DH, NSA_G * NSA_DH, NSA_G * NSA_DH, NSA_G * NSA_DH, NSA_G * NSA_DH, NSA_G * NSA_DH,
             3 * NSA_H, DIFF_H * 2 * DIFF_DQK, DIFF_H * 2 * DIFF_DQK, DIFF_H * DIFF_DV)
IN_VALUE_LIKE = (True, True, False, False, True, False, True, False, True, False, False, False, True)
N_IN = sum(IN_SPLITS)
MIX_W = CONV_C + NSA_H * NSA_DH + DIFF_H * DIFF_DV

kernel_name = "hymba_conv_nsa_diff_decoder_step"


def _split_points():
    return [int(v) for v in np.cumsum(IN_SPLITS)[:-1]]


def _in_col_scale():
    cols = [np.full((n,), DN_BETA if v else 1.0, np.float32) for n, v in zip(IN_SPLITS, IN_VALUE_LIKE)]
    return np.concatenate(cols) * np.float32(D_MODEL ** -0.5)


def layer_norm(x, g, b):
    xf = x.astype(jnp.float32)
    mu = jnp.mean(xf, -1, keepdims=True)
    var = jnp.mean(jnp.square(xf - mu), -1, keepdims=True)
    return ((xf - mu) * lax.rsqrt(var + LN_EPS) * g.astype(jnp.float32) + b.astype(jnp.float32)).astype(x.dtype)


def rms_norm(x, g):
    xf = x.astype(jnp.float32)
    return (xf * lax.rsqrt(jnp.mean(jnp.square(xf), -1, keepdims=True) + LN_EPS) * g.astype(jnp.float32)).astype(x.dtype)


def rope(x, pos):
    half = x.shape[-1] // 2
    inv = ROPE_THETA ** (-jnp.arange(half, dtype=jnp.float32) / half)
    ang = pos.astype(jnp.float32)[:, None] * inv[None, :]
    cos = jnp.cos(ang)[None, :, None, :]
    sin = jnp.sin(ang)[None, :, None, :]
    x1 = x[..., :half].astype(jnp.float32)
    x2 = x[..., half:].astype(jnp.float32)
    return jnp.concatenate([x1 * cos - x2 * sin, x2 * cos + x1 * sin], -1).astype(x.dtype)


def masked_softmax(s, mask):
    s = jnp.where(mask, s.astype(jnp.float32), -jnp.inf)
    m = jnp.max(s, -1, keepdims=True)
    m = jnp.where(jnp.isfinite(m), m, 0.0)
    p = jnp.exp(s - m)
    return p / jnp.maximum(jnp.sum(p, -1, keepdims=True), 1e-30)


def sweep_queries(fn, q_args, q_pos):
    T = q_pos.shape[0]
    blk = QBLK if T % QBLK == 0 else T
    n = T // blk
    xs = tuple(jnp.moveaxis(a.reshape(a.shape[0], n, blk, *a.shape[2:]), 1, 0) for a in q_args)
    out = lax.map(lambda z: fn(*z[0], z[1]), (xs, q_pos.reshape(n, blk)))
    out = jnp.moveaxis(out, 0, 1)
    return out.reshape(out.shape[0], T, *out.shape[3:])


def gather_pages(cache_l, page_table):
    g = cache_l[page_table]
    return g.reshape(page_table.shape[0], page_table.shape[1] * cache_l.shape[1], *cache_l.shape[2:])


def conv_mixer(a, g, buf, dw_w, dw_b, ln_g, ln_b):
    u = a * jax.nn.sigmoid(g)
    ext = jnp.concatenate([buf.astype(u.dtype), u], axis=1)
    y = lax.conv_general_dilated(ext, dw_w[:, None, :].astype(u.dtype), window_strides=(1,), padding='VALID',
                                 dimension_numbers=('NWC', 'WIO', 'NWC'), feature_group_count=CONV_C) + dw_b
    y = jax.nn.silu(layer_norm(y, ln_g, ln_b))
    return y, ext[:, -(CONV_W - 1):]


def nsa_compress(k, pe, w1, b1, w2):
    B, L, G, D = k.shape
    n16 = L // CMP_STRIDE
    k16 = k[:, :n16 * CMP_STRIDE].reshape(B, n16, CMP_STRIDE, G, D)
    blk = jnp.concatenate([k16[:, :-1], k16[:, 1:]], axis=2) + pe[None, None, :, None, :]
    blk = blk.transpose(0, 1, 3, 2, 4).reshape(B, n16 - 1, G, CMP_LEN * D)
    return jax.nn.gelu(blk @ w1 + b1) @ w2


def nsa_cmp_sel(qg, qg_r, kc, vc, ks, vs, q_pos, w):
    B, T, G, HPG, D = qg.shape
    L = kc.shape[1]
    scale = D ** -0.5
    kcc = nsa_compress(kc, w['cmp_pe_k'], w['cmp_w1_k'], w['cmp_b1_k'], w['cmp_w2_k'])
    vcc = nsa_compress(vc, w['cmp_pe_v'], w['cmp_w1_v'], w['cmp_b1_v'], w['cmp_w2_v'])
    nc = kcc.shape[1]
    c_end = CMP_STRIDE * jnp.arange(nc, dtype=jnp.int32) + CMP_LEN - 1
    s = jnp.einsum('btghd,bcgd->btghc', qg, kcc) * scale
    pc = masked_softmax(s, (c_end[None, :] <= q_pos[:, None])[None, :, None, None, :])
    o_cmp = jnp.einsum('btghc,bcgd->btghd', pc.astype(vcc.dtype), vcc)
    imp = jnp.sum(pc, axis=3)
    n_sel = -(-L // SEL_BLK)
    r = SEL_BLK // CMP_STRIDE
    p_pad = jnp.pad(imp, ((0, 0), (0, 0), (0, 0), (1, max(0, r * n_sel - nc))))
    idx_np = r * np.arange(n_sel)[:, None] + np.arange(r + 1)[None, :]
    imp_sel = jnp.sum(jnp.take(p_pad, idx_np, axis=-1), -1)
    j = jnp.arange(n_sel, dtype=jnp.int32)[None, :]
    cur = (q_pos // SEL_BLK)[:, None]
    forced = (j == 0) | (j == cur) | (j == cur - 1)
    score = jnp.where(forced[None, :, None, :], FORCE_SCORE, imp_sel)
    score = jnp.where((j <= cur)[None, :, None, :], score, -jnp.inf)
    _, sel_idx = lax.top_k(score, min(SEL_TOPN, n_sel))
    pad = n_sel * SEL_BLK - L
    ksb = jnp.pad(ks, ((0, 0), (0, pad), (0, 0), (0, 0))).reshape(B, n_sel, SEL_BLK, G, D).transpose(0, 3, 1, 2, 4)
    vsb = jnp.pad(vs, ((0, 0), (0, pad), (0, 0), (0, 0))).reshape(B, n_sel, SEL_BLK, G, D).transpose(0, 3, 1, 2, 4)
    b_ix = jnp.arange(B)[:, None, None, None]
    g_ix = jnp.arange(G)[None, None, :, None]

    def sel_block(qb, ib, tb):
        kg = ksb[b_ix, g_ix, ib]
        vg = vsb[b_ix, g_ix, ib]
        kpos = ib[..., None] * SEL_BLK + jnp.arange(SEL_BLK, dtype=jnp.int32)
        sb = jnp.einsum('bcghd,bcgkpd->bcghkp', qb, kg) * scale
        mask = (kpos <= tb[None, :, None, None, None])[:, :, :, None]
        pb = masked_softmax(sb.reshape(*sb.shape[:4], -1), mask.reshape(*mask.shape[:4], -1)).reshape(sb.shape)
        return jnp.einsum('bcghkp,bcgkpd->bcghd', pb.astype(vg.dtype), vg)

    o_sel = sweep_queries(sel_block, (qg_r, sel_idx), q_pos)
    return o_cmp, o_sel


def window_banded(q, k, v):
    B, T, G, HPG, D = q.shape
    nb = T // QBLK
    nw = WINDOW // QBLK
    kp = jnp.pad(k, ((0, 0), (WINDOW, 0), (0, 0), (0, 0))).reshape(B, nb + nw, QBLK, G, D)
    vp = jnp.pad(v, ((0, 0), (WINDOW, 0), (0, 0), (0, 0))).reshape(B, nb + nw, QBLK, G, D)
    kband = jnp.concatenate([kp[:, o:o + nb] for o in range(nw + 1)], axis=2)
    vband = jnp.concatenate([vp[:, o:o + nb] for o in range(nw + 1)], axis=2)
    qb = q.reshape(B, nb, QBLK, G, HPG, D)
    s = jnp.einsum('bnqghd,bnkgd->bnqghk', qb, kband) * (D ** -0.5)
    start = jnp.arange(nb, dtype=jnp.int32)[:, None] * QBLK
    qpos = start + jnp.arange(QBLK, dtype=jnp.int32)[None, :]
    kpos = start - WINDOW + jnp.arange(WINDOW + QBLK, dtype=jnp.int32)[None, :]
    dlt = qpos[:, :, None] - kpos[:, None, :]
    mask = (dlt >= 0) & (dlt < WINDOW) & (kpos[:, None, :] >= 0)
    p = masked_softmax(s, mask[None, :, :, None, None, :])
    o = jnp.einsum('bnqghk,bnkgd->bnqghd', p.astype(vband.dtype), vband)
    return o.reshape(B, T, G, HPG, D)


def window_dense(q, k, v, q_pos, k_pos):
    s = jnp.einsum('btghd,bsgd->btghs', q, k) * (q.shape[-1] ** -0.5)
    dlt = q_pos[:, None] - k_pos[None, :]
    p = masked_softmax(s, ((dlt >= 0) & (dlt < WINDOW))[None, :, None, None, :])
    return jnp.einsum('btghs,bsgd->btghd', p.astype(v.dtype), v)


def diff_mixer(q, k, v, q_pos, k_pos, lam, sub_g, lam_init):
    scale = DIFF_DQK ** -0.5

    def block(qb, tb):
        s = jnp.einsum('bqhid,bkhid->bhiqk', qb, k) * scale
        p = masked_softmax(s, k_pos[None, :] <= tb[:, None])
        a = p[:, :, 0] - lam * p[:, :, 1]
        return jnp.einsum('bhqk,bkhd->bqhd', a.astype(v.dtype), v)

    o = sweep_queries(block, (q,), q_pos)
    return rms_norm(o, sub_g) * (1.0 - lam_init)


def layer(x, pos, past, w, lam_init, past_len):
    B, T, _ = x.shape
    z = x @ w['w_in']
    ca, cg, nq, ck, cv, sk, sv, wk, wv, gt, dq, dk, dv = jnp.split(z, _split_points(), axis=-1)

    def cat(a, b):
        return jnp.concatenate([a.astype(b.dtype), b], axis=1)

    conv_buf = jnp.zeros((B, CONV_W - 1, CONV_C), x.dtype) if past is None else past['conv']
    y_conv, new_conv = conv_mixer(ca, cg, conv_buf, w['conv_dw_w'], w['conv_dw_b'], w['conv_ln_g'], w['conv_ln_b'])

    q = nq.reshape(B, T, NSA_H, NSA_DH)
    qg = q.reshape(B, T, NSA_G, NSA_HPG, NSA_DH)
    qg_r = rope(q, pos).reshape(B, T, NSA_G, NSA_HPG, NSA_DH)
    ck = ck.reshape(B, T, NSA_G, NSA_DH)
    cv = cv.reshape(B, T, NSA_G, NSA_DH)
    sk = rope(sk.reshape(B, T, NSA_G, NSA_DH), pos)
    sv = sv.reshape(B, T, NSA_G, NSA_DH)
    wk = rope(wk.reshape(B, T, NSA_G, NSA_DH), pos)
    wv = wv.reshape(B, T, NSA_G, NSA_DH)
    if past is None:
        kc_all, vc_all, ks_all, vs_all = ck, cv, sk, sv
    else:
        kc_all, vc_all = cat(past['cmp_k'], ck), cat(past['cmp_v'], cv)
        ks_all, vs_all = cat(past['sel_k'], sk), cat(past['sel_v'], sv)
    o_cmp, o_sel = nsa_cmp_sel(qg, qg_r, kc_all, vc_all, ks_all, vs_all, pos, w)
    if past is None:
        o_win = window_banded(qg_r, wk, wv)
        n_keep = min(WINDOW, T)
        new_wk, new_wv = wk[:, T - n_keep:], wv[:, T - n_keep:]
    else:
        wb = past['win_k'].shape[1]
        wk_all, wv_all = cat(past['win_k'], wk), cat(past['win_v'], wv)
        k_pos = jnp.concatenate([past_len - wb + jnp.arange(wb, dtype=jnp.int32), pos])
        o_win = window_dense(qg_r, wk_all, wv_all, pos, k_pos)
        new_wk, new_wv = wk_all[:, -wb:], wv_all[:, -wb:]
    gate = jax.nn.sigmoid(gt).reshape(B, T, NSA_G, NSA_HPG, 3)
    o_nsa = (gate[..., 0:1] * o_cmp + gate[..., 1:2] * o_sel + gate[..., 2:3] * o_win).reshape(B, T, NSA_H * NSA_DH)

    dq = rope(dq.reshape(B, T, 2 * DIFF_H, DIFF_DQK), pos).reshape(B, T, DIFF_H, 2, DIFF_DQK)
    dk = rope(dk.reshape(B, T, 2 * DIFF_H, DIFF_DQK), pos).reshape(B, T, DIFF_H, 2 * DIFF_DQK)
    dv = dv.reshape(B, T, DIFF_H, DIFF_DV)
    if past is None:
        dk_all, dv_all, dk_pos = dk, dv, pos
    else:
        dk_all, dv_all = cat(past['diff_k'], dk), cat(past['diff_v'], dv)
        dk_pos = jnp.arange(past_len + T, dtype=jnp.int32)
    f32 = jnp.float32
    lam = (jnp.exp(jnp.sum(w['lq1'].astype(f32) * w['lk1'].astype(f32)))
           - jnp.exp(jnp.sum(w['lq2'].astype(f32) * w['lk2'].astype(f32))) + lam_init)
    o_diff = diff_mixer(dq, dk_all.reshape(B, -1, DIFF_H, 2, DIFF_DQK), dv_all, pos, dk_pos, lam,
                        w['diff_subln_g'], lam_init).reshape(B, T, DIFF_H * DIFF_DV)

    mix = jnp.concatenate([y_conv, o_nsa, o_diff], axis=-1) @ w['w_out']
    x = layer_norm(DN_ALPHA * x + mix, w['ln1_g'], w['ln1_b'])
    h = jax.nn.silu(x @ w['ffn_w1']) * (x @ w['ffn_w3'])
    x = layer_norm(DN_ALPHA * x + h @ w['ffn_w2'], w['ln2_g'], w['ln2_b'])
    return x, (ck, cv, sk, sv, dk, dv, new_wk, new_wv, new_conv)


def setup_inputs(seed: int = 0) -> dict:
    key = jax.random.key(seed)
    keys = iter(jax.random.split(key, 64))

    def nrm(shape, scale):
        return jax.random.normal(next(keys), shape, jnp.float32) * scale

    n_pages = PAST_LEN // PAGE_SIZE
    n_used = DEC_BATCH * n_pages
    n_pool = (5 * n_used + 3) // 4
    wb = min(WINDOW, PAST_LEN)
    page_table = jax.random.permutation(next(keys), n_pool)[:n_used].astype(jnp.int32).reshape(DEC_BATCH, n_pages)
    kv_nsa = (DEPTH, n_pool, PAGE_SIZE, NSA_G, NSA_DH)
    return {
        "x_prompt": nrm((BATCH, SEQ, D_MODEL), 1.0),
        "x_sample": nrm((DEC_BATCH, DEC_SEQ, D_MODEL), 1.0),
        "cache_nsa_cmp_k": nrm(kv_nsa, 1.0),
        "cache_nsa_cmp_v": nrm(kv_nsa, DN_BETA),
        "cache_nsa_sel_k": nrm(kv_nsa, 1.0),
        "cache_nsa_sel_v": nrm(kv_nsa, DN_BETA),
        "cache_diff_k": nrm((DEPTH, n_pool, PAGE_SIZE, DIFF_H, 2 * DIFF_DQK), 1.0),
        "cache_diff_v": nrm((DEPTH, n_pool, PAGE_SIZE, DIFF_H, DIFF_DV), DN_BETA),
        "state_nsa_win_k": nrm((DEPTH, DEC_BATCH, wb, NSA_G, NSA_DH), 1.0),
        "state_nsa_win_v": nrm((DEPTH, DEC_BATCH, wb, NSA_G, NSA_DH), DN_BETA),
        "state_conv": nrm((DEPTH, DEC_BATCH, CONV_W - 1, CONV_C), 0.5),
        "page_table": page_table,
        "w_in": nrm((DEPTH, D_MODEL, N_IN), 1.0) * jnp.asarray(_in_col_scale()),
        "conv_dw_w": nrm((DEPTH, CONV_W, CONV_C), CONV_W ** -0.5),
        "conv_dw_b": nrm((DEPTH, CONV_C), 0.02),
        "conv_ln_g": 1.0 + nrm((DEPTH, CONV_C), 0.02),
        "conv_ln_b": nrm((DEPTH, CONV_C), 0.02),
        "cmp_pe_k": nrm((DEPTH, CMP_LEN, NSA_DH), 0.02),
        "cmp_w1_k": nrm((DEPTH, CMP_LEN * NSA_DH, CMP_HID), (CMP_LEN * NSA_DH) ** -0.5),
        "cmp_b1_k": nrm((DEPTH, CMP_HID), 0.02),
        "cmp_w2_k": nrm((DEPTH, CMP_HID, NSA_DH), CMP_HID ** -0.5),
        "cmp_pe_v": nrm((DEPTH, CMP_LEN, NSA_DH), 0.02),
        "cmp_w1_v": nrm((DEPTH, CMP_LEN * NSA_DH, CMP_HID), (CMP_LEN * NSA_DH) ** -0.5),
        "cmp_b1_v": nrm((DEPTH, CMP_HID), 0.02),
        "cmp_w2_v": nrm((DEPTH, CMP_HID, NSA_DH), CMP_HID ** -0.5 * DN_BETA),
        "diff_lq1": nrm((DEPTH, DIFF_DQK), 0.1),
        "diff_lk1": nrm((DEPTH, DIFF_DQK), 0.1),
        "diff_lq2": nrm((DEPTH, DIFF_DQK), 0.1),
        "diff_lk2": nrm((DEPTH, DIFF_DQK), 0.1),
        "diff_subln_g": 1.0 + nrm((DEPTH, DIFF_DV), 0.02),
        "w_out": nrm((DEPTH, MIX_W, D_MODEL), MIX_W ** -0.5 * DN_BETA),
        "ln1_g": 1.0 + nrm((DEPTH, D_MODEL), 0.02),
        "ln1_b": nrm((DEPTH, D_MODEL), 0.02),
        "ln2_g": 1.0 + nrm((DEPTH, D_MODEL), 0.02),
        "ln2_b": nrm((DEPTH, D_MODEL), 0.02),
        "ffn_w1": nrm((DEPTH, D_MODEL, D_FF), D_MODEL ** -0.5 * DN_BETA),
        "ffn_w3": nrm((DEPTH, D_MODEL, D_FF), D_MODEL ** -0.5 * DN_BETA),
        "ffn_w2": nrm((DEPTH, D_FF, D_MODEL), D_FF ** -0.5 * DN_BETA),
    }


def reference(x_prompt, x_sample, cache_nsa_cmp_k, cache_nsa_cmp_v, cache_nsa_sel_k, cache_nsa_sel_v,
              cache_diff_k, cache_diff_v, state_nsa_win_k, state_nsa_win_v, state_conv, page_table,
              w_in, conv_dw_w, conv_dw_b, conv_ln_g, conv_ln_b,
              cmp_pe_k, cmp_w1_k, cmp_b1_k, cmp_w2_k, cmp_pe_v, cmp_w1_v, cmp_b1_v, cmp_w2_v,
              diff_lq1, diff_lk1, diff_lq2, diff_lk2, diff_subln_g, w_out,
              ln1_g, ln1_b, ln2_g, ln2_b, ffn_w1, ffn_w3, ffn_w2):
    past_len = page_table.shape[1] * PAGE_SIZE
    pos_p = jnp.arange(x_prompt.shape[1], dtype=jnp.int32)
    pos_s = past_len + jnp.arange(x_sample.shape[1], dtype=jnp.int32)
    xp, xs = x_prompt, x_sample
    outs_p, outs_s = [], []
    for l in range(DEPTH):
        w = dict(w_in=w_in[l], conv_dw_w=conv_dw_w[l], conv_dw_b=conv_dw_b[l], conv_ln_g=conv_ln_g[l],
                 conv_ln_b=conv_ln_b[l], cmp_pe_k=cmp_pe_k[l], cmp_w1_k=cmp_w1_k[l], cmp_b1_k=cmp_b1_k[l],
                 cmp_w2_k=cmp_w2_k[l], cmp_pe_v=cmp_pe_v[l], cmp_w1_v=cmp_w1_v[l], cmp_b1_v=cmp_b1_v[l],
                 cmp_w2_v=cmp_w2_v[l], lq1=diff_lq1[l], lk1=diff_lk1[l], lq2=diff_lq2[l], lk2=diff_lk2[l],
                 diff_subln_g=diff_subln_g[l], w_out=w_out[l], ln1_g=ln1_g[l], ln1_b=ln1_b[l],
                 ln2_g=ln2_g[l], ln2_b=ln2_b[l], ffn_w1=ffn_w1[l], ffn_w3=ffn_w3[l], ffn_w2=ffn_w2[l])
        lam_init = 0.8 - 0.6 * math.exp(-0.3 * l)
        past = dict(cmp_k=gather_pages(cache_nsa_cmp_k[l], page_table),
                    cmp_v=gather_pages(cache_nsa_cmp_v[l], page_table),
                    sel_k=gather_pages(cache_nsa_sel_k[l], page_table),
                    sel_v=gather_pages(cache_nsa_sel_v[l], page_table),
                    diff_k=gather_pages(cache_diff_k[l], page_table),
                    diff_v=gather_pages(cache_diff_v[l], page_table),
                    win_k=state_nsa_win_k[l], win_v=state_nsa_win_v[l], conv=state_conv[l])
        xp, new_p = layer(xp, pos_p, None, w, lam_init, past_len)
        xs, new_s = layer(xs, pos_s, past, w, lam_init, past_len)
        outs_p.append(new_p)
        outs_s.append(new_s)
    p_cmp_k, p_cmp_v, p_sel_k, p_sel_v, p_diff_k, p_diff_v, p_win_k, p_win_v, p_conv = [
        jnp.stack([o[i] for o in outs_p]) for i in range(9)]
    s_cmp_k, s_cmp_v, s_sel_k, s_sel_v, s_diff_k, s_diff_v, s_win_k, s_win_v, s_conv = [
        jnp.stack([o[i] for o in outs_s]) for i in range(9)]
    return (xp, xs,
            p_cmp_k, p_cmp_v, p_sel_k, p_sel_v, p_diff_k, p_diff_v, p_win_k, p_win_v, p_conv,
            s_cmp_k, s_cmp_v, s_sel_k, s_sel_v, s_diff_k, s_diff_v, s_win_k, s_win_v, s_conv)
```

```python
import functools
import math

import jax
import jax.numpy as jnp
import numpy as np
from jax import lax
from jax.experimental import pallas as pl
from jax.experimental.pallas import tpu as pltpu

F32 = jnp.float32
BF16 = jnp.bfloat16

CONV_W = 31
NSA_H = 8
NSA_G = 2
NSA_HPG = NSA_H // NSA_G
NSA_DH = 64
CMP_STRIDE = 16
CMP_LEN = 32
CMP_HID = 128
SEL_BLK = 64
SEL_TOPN = 16
WINDOW = 512
FORCE_SCORE = 1e9
DIFF_H = 4
DIFF_DV = 64
DIFF_DQK = 32
ROPE_THETA = 10000.0
LN_EPS = 1e-5
PAGE = 128

LANES = 128
TQ = 128
NEG = -1e30
VMEM_LIMIT = 52 * 1024 * 1024

C_CA, C_CG, C_Q, C_KV, C_GT, C_DQ, C_DK, C_DV, C_END = 0, 256, 512, 1536, 2304, 2432, 3456, 3712, 3968


def _cparams(sem):
    return pltpu.CompilerParams(dimension_semantics=sem, vmem_limit_bytes=VMEM_LIMIT)


def _sigmoid(x):
    return 1.0 / (1.0 + jnp.exp(-x))


def _ln_rows(x, g, b):
    mu = jnp.mean(x, axis=-1, keepdims=True)
    xc = x - mu
    var = jnp.mean(xc * xc, axis=-1, keepdims=True)
    return xc * lax.rsqrt(var + LN_EPS) * g + b


def _dot(a, b):
    return jnp.dot(a, b, preferred_element_type=F32)


def _dot_nt(a, b):
    return lax.dot_general(a, b, (((1,), (1,)), ((), ())), preferred_element_type=F32)


def _split3_dot(x, m):
    hi = x.astype(BF16)
    r1 = x - hi.astype(F32)
    mid = r1.astype(BF16)
    lo = (r1 - mid.astype(F32)).astype(BF16)
    return _dot(hi, m) + _dot(mid, m) + _dot(lo, m)


def _rope(x, cos, sin_signed, half):
    lane = lax.broadcasted_iota(jnp.int32, x.shape, 1)
    first = (lane & (2 * half - 1)) < half
    rot = jnp.where(first, pltpu.roll(x, LANES - half, 1), pltpu.roll(x, half, 1))
    return x * cos + rot * sin_signed


def _inproj_body(x_ref, w_ref, c64_ref, s64_ref, c32_ref, s32_ref,
                 u_ref, qu_ref, qr_ref, ck_ref, cv_ref, sk_ref, sv_ref, wk_ref, wv_ref,
                 gt_ref, dq_ref, dk_ref, dv_ref, *t_refs, dq_scale):
    xb = x_ref[...].astype(BF16)

    def mm(lo, hi):
        return _dot(xb, w_ref[:, lo:hi])

    c64, s64, c32, s32 = c64_ref[...], s64_ref[...], c32_ref[...], s32_ref[...]
    z = mm(C_CA, C_Q)
    u_ref[...] = z[:, :256] * _sigmoid(z[:, 256:])
    for h in range(8):
        zq = mm(C_Q + LANES * h, C_Q + LANES * (h + 1))
        qu_ref[:, LANES * h:LANES * (h + 1)] = zq.astype(BF16)
        qr_ref[:, LANES * h:LANES * (h + 1)] = _rope(zq, c64, s64, 32).astype(BF16)
    z = mm(C_KV, C_GT)
    ck_ref[...] = z[:, 0:128]
    cv_ref[...] = z[:, 128:256]
    sk = _rope(z[:, 256:384], c64, s64, 32)
    sk_ref[...] = sk
    sv = z[:, 384:512]
    sv_ref[...] = sv
    wk = _rope(z[:, 512:640], c64, s64, 32)
    wk_ref[...] = wk
    wv = z[:, 640:768]
    wv_ref[...] = wv
    gt_ref[...] = _sigmoid(mm(C_GT, C_DQ))
    for h in range(8):
        zq = mm(C_DQ + LANES * h, C_DQ + LANES * (h + 1))
        dq_ref[:, LANES * h:LANES * (h + 1)] = (_rope(zq, c32, s32, 16) * dq_scale).astype(BF16)
    z = mm(C_DK, C_DV)
    dk0 = _rope(z[:, :128], c32, s32, 16)
    dk1 = _rope(z[:, 128:], c32, s32, 16)
    dk_ref[:, :128] = dk0
    dk_ref[:, 128:] = dk1
    dv = mm(C_DV, C_END)
    dv_ref[...] = dv
    if t_refs:
        skb_ref, wkb_ref, dkb_ref, svt_ref, wvt_ref, dvt_ref = t_refs
        skb_ref[...] = sk.astype(BF16)
        wkb_ref[...] = wk.astype(BF16)
        dkb_ref[:, :128] = dk0.astype(BF16)
        dkb_ref[:, 128:] = dk1.astype(BF16)
        svt_ref[...] = sv.T.astype(BF16)
        wvt_ref[...] = wv.T.astype(BF16)
        dvt_ref[:128, :] = dv[:, :128].T.astype(BF16)
        dvt_ref[128:, :] = dv[:, 128:].T.astype(BF16)


def _inproj(x, w, tabs, nb, t, tm, with_t):
    n, d = x.shape
    nt = t // tm if with_t else 1
    grid = (n // tm,)
    row = lambda c: pl.BlockSpec((tm, c), lambda i: (i, 0))
    tab = pl.BlockSpec((tm, LANES), (lambda i: (i % nt, 0)) if with_t else (lambda i: (i, 0)))
    in_specs = [row(d), pl.BlockSpec((d, C_END), lambda i: (0, 0)), tab, tab, tab, tab]
    shapes = [(256, F32), (1024, BF16), (1024, BF16)] + [(128, F32)] * 6 + [(128, F32), (1024, BF16), (256, F32), (256, F32)]
    out_shape = [jax.ShapeDtypeStruct((n, c), dt) for c, dt in shapes]
    out_specs = [row(c) for c, _ in shapes]
    if with_t:
        out_shape += [jax.ShapeDtypeStruct((n, 128), BF16), jax.ShapeDtypeStruct((n, 128), BF16),
                      jax.ShapeDtypeStruct((n, 256), BF16)]
        out_specs += [row(128), row(128), row(256)]
        for c in (128, 128, 256):
            out_shape.append(jax.ShapeDtypeStruct((nb, c, t), BF16))
            out_specs.append(pl.BlockSpec((None, c, tm), lambda i: (i // nt, 0, i % nt)))
    return pl.pallas_call(
        functools.partial(_inproj_body, dq_scale=DIFF_DQK ** -0.5),
        grid=grid, in_specs=in_specs, out_specs=out_specs, out_shape=out_shape,
        compiler_params=_cparams(("parallel",)),
    )(x, w, *tabs)


CONV_PAD = 32


def _conv_body(ext_ref, w_ref, b_ref, g_ref, beta_ref, y_ref, acc_ref, *, tt, rs):
    t0 = pl.multiple_of(pl.program_id(1) * tt, 8)
    off = CONV_PAD - (CONV_W - 1)
    for c in range(2):
        cs = slice(LANES * c, LANES * (c + 1))
        for r in range(tt // rs):
            win = ext_ref[pl.ds(t0 + rs * r, rs + CONV_PAD), cs]
            acc = jnp.zeros((rs, LANES), F32)
            for k in range(CONV_W):
                acc = acc + win[off + k:off + k + rs, :] * w_ref[k:k + 1, cs]
            acc_ref[rs * r:rs * (r + 1), cs] = acc
    y = _ln_rows(acc_ref[...] + b_ref[...], g_ref[...], beta_ref[...])
    y_ref[...] = y * _sigmoid(y)


def _conv(ext, w, b, g, beta, t, tt):
    nb, le, c = ext.shape
    rs = min(tt, 64)
    vec = pl.BlockSpec((1, c), lambda bi, ti: (0, 0))
    return pl.pallas_call(
        functools.partial(_conv_body, tt=tt, rs=rs),
        grid=(nb, t // tt),
        in_specs=[pl.BlockSpec((None, le, c), lambda bi, ti: (bi, 0, 0)),
                  pl.BlockSpec((CONV_W, c), lambda bi, ti: (0, 0)), vec, vec, vec],
        out_specs=pl.BlockSpec((None, tt, c), lambda bi, ti: (bi, ti, 0)),
        out_shape=jax.ShapeDtypeStruct((nb, t, c), F32),
        scratch_shapes=[pltpu.VMEM((tt, c), F32)],
        compiler_params=_cparams(("parallel", "parallel")),
    )(ext, w, b, g, beta)


def _compress(src_ref, n16, pea, peb, wa_ref, wb_ref, b1, w2_ref):
    x = jnp.concatenate([src_ref[pl.ds(p, n16, stride=CMP_STRIDE), :] for p in range(CMP_STRIDE)], axis=1)
    a = _dot((x + pea).astype(BF16), wa_ref[...])
    bm = _dot((x + peb).astype(BF16), wb_ref[...])
    h = a + pltpu.roll(bm, n16 - 1, 0) + b1
    gl = 0.5 * h * (1.0 + jnp.tanh(0.7978845608028654 * (h + 0.044715 * (h * h * h))))
    return _dot(gl.astype(BF16), w2_ref[...])


def _top_rows(score, ridx, k):
    cnt = jnp.zeros(score.shape, F32)
    for jp in range(score.shape[0]):
        row = score[jp:jp + 1, :]
        beats = (row > score) | ((row == score) & (ridx > jp))
        cnt = cnt + beats.astype(F32)
    return cnt < k


def _pcmp_body(qu_ref, ck_ref, cv_ref, pak_ref, pbk_ref, wak_ref, wbk_ref, b1k_ref, w2k_ref,
               pav_ref, pbv_ref, wav_ref, wbv_ref, b1v_ref, w2v_ref, mt_ref,
               o_ref, sel_ref, kcc_ref, vcct_ref, *, n16, nsel):
    ti = pl.program_id(1)

    @pl.when(ti == 0)
    def _():
        kcc = _compress(ck_ref, n16, pak_ref[...], pbk_ref[...], wak_ref, wbk_ref, b1k_ref[...], w2k_ref)
        kcc_ref[...] = kcc.astype(BF16)
        vcc = _compress(cv_ref, n16, pav_ref[...], pbv_ref[...], wav_ref, wbv_ref, b1v_ref[...], w2v_ref)
        vcct_ref[...] = vcc.T.astype(BF16)

    q = qu_ref[...]
    kcc = kcc_ref[...]
    vcct = vcct_ref[...]
    nq = 4 * TQ
    cidx = lax.broadcasted_iota(jnp.int32, (n16, nq), 0)
    qpos = ti * TQ + (lax.broadcasted_iota(jnp.int32, (n16, nq), 1) & (TQ - 1))
    vis = (CMP_STRIDE * cidx + CMP_LEN - 1 <= qpos) & (cidx < n16 - 1)
    jidx = lax.broadcasted_iota(jnp.int32, (nsel, TQ), 0)
    qp1 = ti * TQ + lax.broadcasted_iota(jnp.int32, (nsel, TQ), 1)
    cur = qp1 >> 6
    forced = (jidx == 0) | (jidx == cur) | (jidx == cur - 1)
    for g in range(NSA_G):
        qs = jnp.concatenate([q[:, LANES * (4 * g + a):LANES * (4 * g + a + 1)] for a in range(4)], axis=0)
        st = _dot_nt(kcc, qs)
        st = jnp.where(vis, st, NEG)
        m = jnp.max(st, axis=0, keepdims=True)
        p = jnp.where(vis, jnp.exp(st - m), 0.0)
        l = jnp.sum(p, axis=0, keepdims=True)
        p = p / jnp.maximum(l, 1e-30)
        ot = _dot(vcct, p.astype(BF16))
        for a2 in range(2):
            blk = jnp.concatenate([ot[64 * g:64 * g + 64, TQ * (2 * a2 + e):TQ * (2 * a2 + e + 1)] for e in range(2)], axis=0)
            o_ref[:, 256 * g + LANES * a2:256 * g + LANES * (a2 + 1)] = blk.T
        imp = p[:, 0:TQ] + p[:, TQ:2 * TQ] + p[:, 2 * TQ:3 * TQ] + p[:, 3 * TQ:4 * TQ]
        hi = imp.astype(BF16)
        r1 = imp - hi.astype(F32)
        mid = r1.astype(BF16)
        lo = (r1 - mid.astype(F32)).astype(BF16)
        mt = mt_ref[...]
        isel = _dot(mt, hi) + _dot(mt, mid) + _dot(mt, lo)
        score = jnp.where(forced, FORCE_SCORE, isel)
        score = jnp.where(jidx <= cur, score, -jnp.inf)
        sel = _top_rows(score, jidx, min(SEL_TOPN, nsel)) & (jidx <= cur)
        sel_ref[g] = sel.astype(F32)


def _pcmp(qu, ck, cv, cw, nb, t):
    n16 = t // CMP_STRIDE
    nsel = t // SEL_BLK
    nt = t // TQ
    mt = np.zeros((nsel, n16), np.float32)
    for j in range(nsel):
        for c in range(4 * j - 1, 4 * j + 4):
            if 0 <= c < n16 - 1:
                mt[j, c] = 1.0
    mt = jnp.asarray(mt, BF16)
    const = lambda a: pl.BlockSpec(a.shape, lambda bi, ti: (0,) * a.ndim)
    wlist = [cw[k] for k in ("pak", "pbk", "wak", "wbk", "b1k", "w2k", "pav", "pbv", "wav", "wbv", "b1v", "w2v")] + [mt]
    return pl.pallas_call(
        functools.partial(_pcmp_body, n16=n16, nsel=nsel),
        grid=(nb, nt),
        in_specs=[pl.BlockSpec((TQ, 1024), lambda bi, ti: (bi * nt + ti, 0)),
                  pl.BlockSpec((None, t, LANES), lambda bi, ti: (bi, 0, 0)),
                  pl.BlockSpec((None, t, LANES), lambda bi, ti: (bi, 0, 0))] + [const(a) for a in wlist],
        out_specs=[pl.BlockSpec((TQ, 512), lambda bi, ti: (bi * nt + ti, 0)),
                   pl.BlockSpec((None, None, NSA_G, nsel, TQ), lambda bi, ti: (bi, ti, 0, 0, 0))],
        out_shape=[jax.ShapeDtypeStruct((nb * t, 512), F32),
                   jax.ShapeDtypeStruct((nb, nt, NSA_G, nsel, TQ), F32)],
        scratch_shapes=[pltpu.VMEM((n16, LANES), BF16), pltpu.VMEM((LANES, n16), BF16)],
        compiler_params=_cparams(("parallel", "arbitrary")),
    )(qu, ck.reshape(nb, t, LANES), cv.reshape(nb, t, LANES), *wlist)


def _flash_body(*refs, mode, lam_init):
    if mode == "sel":
        q_ref, k_ref, vt_ref, sel_ref, o_ref, acc_ref, m_ref, l_ref = refs
    elif mode == "win":
        q_ref, k_ref, vt_ref, o_ref, acc_ref, m_ref, l_ref = refs
    else:
        q_ref, k_ref, vt_ref, lq1_ref, lk1_ref, lq2_ref, lk2_ref, sg_ref, o_ref, acc_ref, m_ref, l_ref = refs
    ti = pl.program_id(1)
    nq = 4 * TQ
    q = q_ref[...]
    krow = lax.broadcasted_iota(jnp.int32, (TQ, nq), 0)
    qcol = lax.broadcasted_iota(jnp.int32, (TQ, nq), 1) & (TQ - 1)
    if mode == "diff":
        lam = (jnp.exp(jnp.sum(lq1_ref[...] * lk1_ref[...], keepdims=True))
               - jnp.exp(jnp.sum(lq2_ref[...] * lk2_ref[...], keepdims=True)) + lam_init)

    for s in range(2):
        koff = LANES * s if mode == "diff" else 0
        qs = jnp.concatenate([q[:, LANES * (4 * s + a):LANES * (4 * s + a + 1)] for a in range(4)], axis=0)
        m_ref[...] = jnp.full((1, nq), NEG, F32)
        l_ref[...] = jnp.zeros((1, nq), F32)
        acc_ref[...] = jnp.zeros((LANES, nq), F32)

        def step(j, mask_kind):
            k0 = pl.multiple_of(j * TQ, TQ)
            kt = k_ref[pl.ds(k0, TQ), koff:koff + LANES]
            st = _dot_nt(kt, qs)
            dlt = (ti - j) * TQ + qcol - krow
            if mask_kind == "causal":
                st = jnp.where(dlt >= 0, st, NEG)
            elif mask_kind == "band":
                st = jnp.where(dlt < WINDOW, st, NEG)
            elif mask_kind == "sel":
                r0 = sel_ref[s, pl.ds(2 * j, 1), :]
                r1 = sel_ref[s, pl.ds(2 * j + 1, 1), :]
                sm = jnp.concatenate([jnp.broadcast_to(r0, (SEL_BLK, TQ)), jnp.broadcast_to(r1, (SEL_BLK, TQ))], axis=0)
                sm = jnp.concatenate([sm, sm, sm, sm], axis=1)
                st = jnp.where((sm > 0.5) & (dlt >= 0), st, NEG)
            m_old = m_ref[...]
            m_new = jnp.maximum(m_old, jnp.max(st, axis=0, keepdims=True))
            alpha = jnp.exp(m_old - m_new)
            p = jnp.exp(st - m_new)
            l_ref[...] = alpha * l_ref[...] + jnp.sum(p, axis=0, keepdims=True)
            vt = vt_ref[koff:koff + LANES, pl.ds(k0, TQ)]
            acc_ref[...] = alpha * acc_ref[...] + _dot(vt, p.astype(BF16))
            m_ref[...] = m_new

        if mode == "sel":
            lax.fori_loop(0, ti + 1, lambda j, c: (step(j, "sel"), c)[1], 0)
        elif mode == "diff":
            lax.fori_loop(0, ti, lambda j, c: (step(j, None), c)[1], 0)
            step(ti, "causal")
        else:
            nw = WINDOW // TQ

            @pl.when(ti >= nw)
            def _():
                step(ti - nw, "band")

            lax.fori_loop(jnp.maximum(ti - nw + 1, 0), ti, lambda j, c: (step(j, None), c)[1], 0)
            step(ti, "causal")

        z = acc_ref[...] / l_ref[...]
        if mode == "diff":
            halves = []
            for hh in range(2):
                zr = z[64 * hh:64 * hh + 64, :]
                d = zr[:, TQ * (2 * hh):TQ * (2 * hh + 1)] - lam * zr[:, TQ * (2 * hh + 1):TQ * (2 * hh + 2)]
                ms = jnp.mean(d * d, axis=0, keepdims=True)
                halves.append(d * lax.rsqrt(ms + LN_EPS) * sg_ref[...] * (1.0 - lam_init))
            o_ref[:, LANES * s:LANES * (s + 1)] = jnp.concatenate(halves, axis=0).T
        else:
            for a2 in range(2):
                blk = jnp.concatenate([z[64 * s:64 * s + 64, TQ * (2 * a2 + e):TQ * (2 * a2 + e + 1)] for e in range(2)], axis=0)
                o_ref[:, 256 * s + LANES * a2:256 * s + LANES * (a2 + 1)] = blk.T


def _flash(mode, q, k, vt, nb, t, extra=(), lam_init=0.0):
    nt = t // TQ
    kw = k.shape[-1]
    ow = 256 if mode == "diff" else 512
    in_specs = [pl.BlockSpec((TQ, 1024), lambda bi, ti: (bi * nt + ti, 0)),
                pl.BlockSpec((None, t, kw), lambda bi, ti: (bi, 0, 0)),
                pl.BlockSpec((None, kw, t), lambda bi, ti: (bi, 0, 0))]
    if mode == "sel":
        nsel = t // SEL_BLK
        in_specs.append(pl.BlockSpec((None, None, NSA_G, nsel, TQ), lambda bi, ti: (bi, ti, 0, 0, 0)))
    elif mode == "diff":
        in_specs += [pl.BlockSpec(a.shape, lambda bi, ti: (0, 0)) for a in extra]
    return pl.pallas_call(
        functools.partial(_flash_body, mode=mode, lam_init=lam_init),
        grid=(nb, nt), in_specs=in_specs,
        out_specs=pl.BlockSpec((TQ, ow), lambda bi, ti: (bi * nt + ti, 0)),
        out_shape=jax.ShapeDtypeStruct((nb * t, ow), F32),
        scratch_shapes=[pltpu.VMEM((LANES, 4 * TQ), F32), pltpu.VMEM((1, 4 * TQ), F32), pltpu.VMEM((1, 4 * TQ), F32)],
        compiler_params=_cparams(("parallel", "parallel")),
    )(q, k.reshape(nb, t, kw), vt, *extra)


def _outproj_body(x_ref, yc_ref, oc_ref, os_ref, ow_ref, gt_ref, od_ref, e_ref, w_ref, g_ref, b_ref, o_ref, *, alpha):
    gt = gt_ref[...]
    hi = gt.astype(BF16)
    lo = (gt - hi.astype(F32)).astype(BF16)
    e = e_ref[...]
    gx = _dot(hi, e) + _dot(lo, e)
    onsa = gx[:, 0:512] * oc_ref[...] + gx[:, 512:1024] * os_ref[...] + gx[:, 1024:1536] * ow_ref[...]
    mix = (_dot(yc_ref[...].astype(BF16), w_ref[0:256, :]) + _dot(onsa.astype(BF16), w_ref[256:768, :])
           + _dot(od_ref[...].astype(BF16), w_ref[768:1024, :]))
    o_ref[...] = _ln_rows(alpha * x_ref[...] + mix, g_ref[...], b_ref[...])


def _outproj(x, yc, oc, osel, ow, gt, od, e, w, g, b, alpha, tm):
    n, d = x.shape
    row = lambda c: pl.BlockSpec((tm, c), lambda i: (i, 0))
    const = lambda a: pl.BlockSpec(a.shape, lambda i: (0, 0))
    return pl.pallas_call(
        functools.partial(_outproj_body, alpha=alpha),
        grid=(n // tm,),
        in_specs=[row(d), row(256), row(512), row(512), row(512), row(128), row(256), const(e), const(w), const(g), const(b)],
        out_specs=row(d), out_shape=jax.ShapeDtypeStruct((n, d), F32),
        compiler_params=_cparams(("parallel",)),
    )(x, yc, oc, osel, ow, gt, od, e, w, g, b)


def _ffn_body(x_ref, w1_ref, w3_ref, w2_ref, g_ref, b_ref, o_ref, acc_ref, *, alpha):
    f = pl.program_id(1)
    xb = x_ref[...].astype(BF16)
    h1 = _dot(xb, w1_ref[...])
    h = h1 * _sigmoid(h1) * _dot(xb, w3_ref[...])
    part = _dot(h.astype(BF16), w2_ref[...])

    @pl.when(f == 0)
    def _():
        acc_ref[...] = part

    @pl.when(f != 0)
    def _():
        acc_ref[...] += part

    @pl.when(f == pl.num_programs(1) - 1)
    def _():
        o_ref[...] = _ln_rows(alpha * x_ref[...] + acc_ref[...], g_ref[...], b_ref[...])


def _ffn(x, w1, w3, w2, g, b, alpha, tm, nf):
    n, d = x.shape
    dff = w1.shape[1]
    tf = dff // nf
    return pl.pallas_call(
        functools.partial(_ffn_body, alpha=alpha),
        grid=(n // tm, nf),
        in_specs=[pl.BlockSpec((tm, d), lambda i, f: (i, 0)),
                  pl.BlockSpec((d, tf), lambda i, f: (0, f)),
                  pl.BlockSpec((d, tf), lambda i, f: (0, f)),
                  pl.BlockSpec((tf, d), lambda i, f: (f, 0)),
                  pl.BlockSpec((1, d), lambda i, f: (0, 0)),
                  pl.BlockSpec((1, d), lambda i, f: (0, 0))],
        out_specs=pl.BlockSpec((tm, d), lambda i, f: (i, 0)),
        out_shape=jax.ShapeDtypeStruct((n, d), F32),
        scratch_shapes=[pltpu.VMEM((tm, d), F32)],
        compiler_params=_cparams(("parallel", "arbitrary")),
    )(x, w1, w3, w2, g, b)


def _gather_pages(pt_ref, b, n_pages, srcs, layer):
    copies = []
    for hbm, buf, sem in srcs:
        for j in range(n_pages):
            copies.append(pltpu.make_async_copy(hbm.at[layer, pt_ref[b, j]], buf.at[pl.ds(PAGE * j, PAGE)], sem))
    for c in copies:
        c.start()
    for c in copies:
        c.wait()


def _stack_heads(q):
    return jnp.concatenate([q[:, LANES * h:LANES * (h + 1)] for h in range(8)], axis=0)


def _scmp_body(pt_ref, qu_ref, ck_hbm, cv_hbm, pak_ref, pbk_ref, wak_ref, wbk_ref, b1k_ref, w2k_ref,
               pav_ref, pbv_ref, wav_ref, wbv_ref, b1v_ref, w2v_ref, m_ref,
               o_ref, sel_ref, kbuf, vbuf, sem, *, layer, n_pages, past, nq):
    b = pl.program_id(0)
    _gather_pages(pt_ref, b, n_pages, [(ck_hbm, kbuf, sem.at[0]), (cv_hbm, vbuf, sem.at[1])], layer)
    n16 = past // CMP_STRIDE
    kcc = _compress(kbuf, n16, pak_ref[...], pbk_ref[...], wak_ref, wbk_ref, b1k_ref[...], w2k_ref).astype(BF16)
    vcc = _compress(vbuf, n16, pav_ref[...], pbv_ref[...], wav_ref, wbv_ref, b1v_ref[...], w2v_ref).astype(BF16)
    qa = _stack_heads(qu_ref[...])
    rows = 8 * nq
    s = _dot_nt(qa, kcc)
    cidx = lax.broadcasted_iota(jnp.int32, (rows, n16), 1)
    qpos = past + (lax.broadcasted_iota(jnp.int32, (rows, n16), 0) & (nq - 1))
    vis = (CMP_STRIDE * cidx + CMP_LEN - 1 <= qpos) & (cidx < n16 - 1)
    s = jnp.where(vis, s, NEG)
    m = jnp.max(s, axis=-1, keepdims=True)
    p = jnp.where(vis, jnp.exp(s - m), 0.0)
    l = jnp.sum(p, axis=-1, keepdims=True)
    p = p / jnp.maximum(l, 1e-30)
    o_ref[...] = _dot(p.astype(BF16), vcc)
    imp = jnp.concatenate(
        [p[nq * 4 * g:nq * (4 * g + 1)] + p[nq * (4 * g + 1):nq * (4 * g + 2)]
         + p[nq * (4 * g + 2):nq * (4 * g + 3)] + p[nq * (4 * g + 3):nq * (4 * g + 4)] for g in range(NSA_G)], axis=0)
    isel = _split3_dot(imp, m_ref[...])
    nselp = isel.shape[1]
    jidx = lax.broadcasted_iota(jnp.int32, (NSA_G * nq, nselp), 1)
    qp = past + (lax.broadcasted_iota(jnp.int32, (NSA_G * nq, nselp), 0) & (nq - 1))
    cur = qp >> 6
    forced = (jidx == 0) | (jidx == cur) | (jidx == cur - 1)
    score = jnp.where(forced, FORCE_SCORE, isel)
    score = jnp.where(jidx <= cur, score, -jnp.inf)
    n_sel = -(-(past + nq) // SEL_BLK)
    sel = jnp.zeros(score.shape, F32)
    for _ in range(min(SEL_TOPN, n_sel)):
        mx = jnp.max(score, axis=-1, keepdims=True)
        first = jnp.min(jnp.where(score == mx, jidx, nselp), axis=-1, keepdims=True)
        hit = jidx == first
        sel = jnp.where(hit, 1.0, sel)
        score = jnp.where(hit, -jnp.inf, score)
    sel_ref[...] = jnp.where(jidx <= cur, sel, 0.0)


def _scmp(pt, qu, cache_k, cache_v, cw, layer, nb, nq, past):
    n_pages = past // PAGE
    n16 = past // CMP_STRIDE
    n_sel = -(-(past + nq) // SEL_BLK)
    nselp = -(-n_sel // LANES) * LANES
    m = np.zeros((n16, nselp), np.float32)
    for j in range(n_sel):
        for c in range(4 * j - 1, 4 * j + 4):
            if 0 <= c < n16 - 1:
                m[c, j] = 1.0
    m = jnp.asarray(m, BF16)
    wlist = [cw[k] for k in ("pak", "pbk", "wak", "wbk", "b1k", "w2k", "pav", "pbv", "wav", "wbv", "b1v", "w2v")] + [m]
    const = lambda a: pl.BlockSpec(a.shape, lambda bi, pt_: (0,) * a.ndim)
    any_spec = pl.BlockSpec(memory_space=pl.ANY)
    grid_spec = pltpu.PrefetchScalarGridSpec(
        num_scalar_prefetch=1, grid=(nb,),
        in_specs=[pl.BlockSpec((nq, 1024), lambda bi, pt_: (bi, 0)), any_spec, any_spec] + [const(a) for a in wlist],
        out_specs=[pl.BlockSpec((None, 8 * nq, LANES), lambda bi, pt_: (bi, 0, 0)),
                   pl.BlockSpec((None, NSA_G * nq, nselp), lambda bi, pt_: (bi, 0, 0))],
        scratch_shapes=[pltpu.VMEM((past, LANES), F32), pltpu.VMEM((past, LANES), F32), pltpu.SemaphoreType.DMA((2,))])
    return pl.pallas_call(
        functools.partial(_scmp_body, layer=layer, n_pages=n_pages, past=past, nq=nq),
        grid_spec=grid_spec,
        out_shape=[jax.ShapeDtypeStruct((nb, 8 * nq, LANES), F32), jax.ShapeDtypeStruct((nb, NSA_G * nq, nselp), F32)],
        compiler_params=_cparams(("arbitrary",)),
    )(pt, qu, cache_k, cache_v, *wlist)


def _softmax_rows(s):
    m = jnp.max(s, axis=-1, keepdims=True)
    p = jnp.exp(s - m)
    return p, jnp.sum(p, axis=-1, keepdims=True)


def _sselwin_body(pt_ref, qr_ref, sel_ref, skn_ref, svn_ref, wkn_ref, wvn_ref, wks_ref, wvs_ref, e_ref, sk_hbm, sv_hbm,
                  osel_ref, owin_ref, kbuf, vbuf, sem, *, layer, n_pages, past, nq):
    b = pl.program_id(0)
    ktot = past + PAGE
    rows = 8 * nq
    kbuf[pl.ds(past, nq), :] = skn_ref[...]
    vbuf[pl.ds(past, nq), :] = svn_ref[...]
    kbuf[pl.ds(past + nq, PAGE - nq), :] = jnp.zeros((PAGE - nq, LANES), F32)
    vbuf[pl.ds(past + nq, PAGE - nq), :] = jnp.zeros((PAGE - nq, LANES), F32)
    _gather_pages(pt_ref, b, n_pages, [(sk_hbm, kbuf, sem.at[0]), (sv_hbm, vbuf, sem.at[1])], layer)
    qa = _stack_heads(qr_ref[...])
    s = _dot_nt(qa, kbuf[...].astype(BF16))
    se = _dot(sel_ref[...].astype(BF16), e_ref[...])
    se = jnp.concatenate([se[nq * g:nq * (g + 1)] for g in range(NSA_G) for _ in range(NSA_HPG)], axis=0)
    kpos = lax.broadcasted_iota(jnp.int32, (rows, ktot), 1)
    qpos = past + (lax.broadcasted_iota(jnp.int32, (rows, ktot), 0) & (nq - 1))
    s = jnp.where((se > 0.5) & (kpos <= qpos), s, NEG)
    p, l = _softmax_rows(s)
    osel_ref[...] = _dot(p.astype(BF16), vbuf[...].astype(BF16)) / l
    wb = wks_ref.shape[0]
    pad = jnp.zeros((PAGE - nq, LANES), F32)
    kw = jnp.concatenate([wks_ref[...], wkn_ref[...], pad], axis=0).astype(BF16)
    vw = jnp.concatenate([wvs_ref[...], wvn_ref[...], pad], axis=0).astype(BF16)
    s = _dot_nt(qa, kw)
    i = lax.broadcasted_iota(jnp.int32, (rows, wb + PAGE), 1)
    kp = jnp.where(i < wb, past - wb + i, past + i - wb)
    qp = past + (lax.broadcasted_iota(jnp.int32, (rows, wb + PAGE), 0) & (nq - 1))
    dlt = qp - kp
    s = jnp.where((dlt >= 0) & (dlt < WINDOW) & (i < wb + nq), s, NEG)
    p, l = _softmax_rows(s)
    owin_ref[...] = _dot(p.astype(BF16), vw) / l


def _sselwin(pt, qr, sel, skn, svn, wkn, wvn, wks, wvs, cache_k, cache_v, layer, nb, nq, past):
    n_pages = past // PAGE
    ktot = past + PAGE
    nselp = sel.shape[-1]
    e = (np.arange(ktot)[None, :] // SEL_BLK == np.arange(nselp)[:, None]).astype(np.float32)
    e = jnp.asarray(e, BF16)
    wb = wks.shape[1]
    any_spec = pl.BlockSpec(memory_space=pl.ANY)
    new = pl.BlockSpec((nq, LANES), lambda bi, pt_: (bi, 0))
    grid_spec = pltpu.PrefetchScalarGridSpec(
        num_scalar_prefetch=1, grid=(nb,),
        in_specs=[pl.BlockSpec((nq, 1024), lambda bi, pt_: (bi, 0)),
                  pl.BlockSpec((None, NSA_G * nq, nselp), lambda bi, pt_: (bi, 0, 0)),
                  new, new, new, new,
                  pl.BlockSpec((None, wb, LANES), lambda bi, pt_: (bi, 0, 0)),
                  pl.BlockSpec((None, wb, LANES), lambda bi, pt_: (bi, 0, 0)),
                  pl.BlockSpec(e.shape, lambda bi, pt_: (0, 0)), any_spec, any_spec],
        out_specs=[pl.BlockSpec((None, 8 * nq, LANES), lambda bi, pt_: (bi, 0, 0)),
                   pl.BlockSpec((None, 8 * nq, LANES), lambda bi, pt_: (bi, 0, 0))],
        scratch_shapes=[pltpu.VMEM((ktot, LANES), F32), pltpu.VMEM((ktot, LANES), F32), pltpu.SemaphoreType.DMA((2,))])
    return pl.pallas_call(
        functools.partial(_sselwin_body, layer=layer, n_pages=n_pages, past=past, nq=nq),
        grid_spec=grid_spec,
        out_shape=[jax.ShapeDtypeStruct((nb, 8 * nq, LANES), F32)] * 2,
        compiler_params=_cparams(("arbitrary",)),
    )(pt, qr, sel, skn, svn, wkn, wvn, wks, wvs, e, cache_k, cache_v)


def _sdiff_body(pt_ref, dq_ref, dkn_ref, dvn_ref, lq1_ref, lk1_ref, lq2_ref, lk2_ref, sg_ref, dk_hbm, dv_hbm,
                o_ref, kbuf, vbuf, sem, *, layer, n_pages, past, nq, lam_init):
    b = pl.program_id(0)
    ktot = past + PAGE
    rows = 8 * nq
    kbuf[pl.ds(past, nq), :] = dkn_ref[...]
    vbuf[pl.ds(past, nq), :] = dvn_ref[...]
    kbuf[pl.ds(past + nq, PAGE - nq), :] = jnp.zeros((PAGE - nq, 256), F32)
    vbuf[pl.ds(past + nq, PAGE - nq), :] = jnp.zeros((PAGE - nq, 256), F32)
    _gather_pages(pt_ref, b, n_pages, [(dk_hbm, kbuf, sem.at[0]), (dv_hbm, vbuf, sem.at[1])], layer)
    q = dq_ref[...]
    zero = jnp.zeros((nq, LANES), BF16)
    blocks = []
    for a in range(8):
        chunk = q[:, LANES * a:LANES * (a + 1)]
        blocks.append(jnp.concatenate([chunk, zero] if a < 4 else [zero, chunk], axis=1))
    qa = jnp.concatenate(blocks, axis=0)
    s = _dot_nt(qa, kbuf[...].astype(BF16))
    kpos = lax.broadcasted_iota(jnp.int32, (rows, ktot), 1)
    qpos = past + (lax.broadcasted_iota(jnp.int32, (rows, ktot), 0) & (nq - 1))
    s = jnp.where(kpos <= qpos, s, NEG)
    p, l = _softmax_rows(s)
    o = _dot(p.astype(BF16), vbuf[...].astype(BF16)) / l
    lam = (jnp.exp(jnp.sum(lq1_ref[...] * lk1_ref[...], keepdims=True))
           - jnp.exp(jnp.sum(lq2_ref[...] * lk2_ref[...], keepdims=True)) + lam_init)
    lane = lax.broadcasted_iota(jnp.int32, (nq, 256), 1)
    out = jnp.zeros((nq, 256), F32)
    for h in range(DIFF_H):
        d = o[2 * nq * h:2 * nq * h + nq] - lam * o[2 * nq * h + nq:2 * nq * (h + 1)]
        inh = (lane >> 6) == h
        ms = jnp.sum(jnp.where(inh, d * d, 0.0), axis=-1, keepdims=True) * (1.0 / DIFF_DV)
        out = out + jnp.where(inh, d * lax.rsqrt(ms + LN_EPS), 0.0)
    o_ref[...] = out * sg_ref[...] * (1.0 - lam_init)


def _sdiff(pt, dq, dkn, dvn, lams, sg_row, cache_k, cache_v, layer, nb, nq, past, lam_init):
    n_pages = past // PAGE
    ktot = past + PAGE
    any_spec = pl.BlockSpec(memory_space=pl.ANY)
    new = pl.BlockSpec((nq, 256), lambda bi, pt_: (bi, 0))
    small = [pl.BlockSpec(a.shape, lambda bi, pt_: (0, 0)) for a in (*lams, sg_row)]
    grid_spec = pltpu.PrefetchScalarGridSpec(
        num_scalar_prefetch=1, grid=(nb,),
        in_specs=[pl.BlockSpec((nq, 1024), lambda bi, pt_: (bi, 0)), new, new] + small + [any_spec, any_spec],
        out_specs=pl.BlockSpec((nq, 256), lambda bi, pt_: (bi, 0)),
        scratch_shapes=[pltpu.VMEM((ktot, 256), F32), pltpu.VMEM((ktot, 256), F32), pltpu.SemaphoreType.DMA((2,))])
    return pl.pallas_call(
        functools.partial(_sdiff_body, layer=layer, n_pages=n_pages, past=past, nq=nq, lam_init=lam_init),
        grid_spec=grid_spec,
        out_shape=jax.ShapeDtypeStruct((nb * nq, 256), F32),
        compiler_params=_cparams(("arbitrary",)),
    )(pt, dq, dkn, dvn, *lams, sg_row, cache_k, cache_v)


def _prep_w_in(w):
    d = w.shape[0]
    pts = np.cumsum([256, 256, 512, 128, 128, 128, 128, 128, 128, 24, 256, 256, 256])[:-1].tolist()
    ca, cg, nq, ck, cv, sk, sv, wk, wv, gt, dq, dk, dv = jnp.split(w, pts, axis=1)
    q5 = (nq * (NSA_DH ** -0.5)).reshape(d, NSA_G, NSA_HPG, 1, NSA_DH)
    qpad = (q5 * jnp.eye(NSA_G, dtype=F32)[None, :, None, :, None]).reshape(d, 8 * LANES)
    d6 = dq.reshape(d, 2, 2, 2, 1, 1, DIFF_DQK)
    sel = jnp.eye(2, dtype=F32)[:, None, :, None] * jnp.eye(2, dtype=F32)[None, :, None, :]
    dqpad = (d6 * sel[None, None, :, :, :, :, None]).reshape(d, 8 * LANES)
    gtp = jnp.pad(gt, ((0, 0), (0, LANES - gt.shape[1])))
    return jnp.concatenate([ca, cg, qpad, ck, cv, sk, sv, wk, wv, gtp, dqpad, dk, dv], axis=1).astype(BF16)


def _rope_tables(pos):
    out = []
    lane = np.arange(LANES)
    for half in (32, 16):
        inv = ROPE_THETA ** (-jnp.arange(half, dtype=F32) / half)
        ang = pos.astype(F32)[:, None] * inv[None, :]
        idx = lane % half
        sign = jnp.asarray(np.where(lane % (2 * half) < half, -1.0, 1.0), F32)
        out += [jnp.cos(ang)[:, idx], jnp.sin(ang)[:, idx] * sign[None, :]]
    return out


def _prep_cmp(pe, w1, b1, w2):
    def halves(x):
        res = []
        for part in (x[:16 * NSA_DH], x[16 * NSA_DH:]):
            p4 = part.reshape(CMP_STRIDE, 1, NSA_DH, 1, -1)
            eye = jnp.eye(NSA_G, dtype=F32)[None, :, None, :, None]
            res.append((p4 * eye).reshape(CMP_STRIDE * NSA_G * NSA_DH, NSA_G * part.shape[-1]))
        return res
    wa, wb = halves(w1)
    pea = jnp.tile(pe[:16, None, :], (1, NSA_G, 1)).reshape(1, -1)
    peb = jnp.tile(pe[16:, None, :], (1, NSA_G, 1)).reshape(1, -1)
    w2bd = (w2[None, :, None, :] * jnp.eye(NSA_G, dtype=F32)[:, None, :, None]).reshape(NSA_G * CMP_HID, NSA_G * NSA_DH)
    return pea, peb, wa.astype(BF16), wb.astype(BF16), jnp.tile(b1, NSA_G)[None, :], w2bd.astype(BF16)


def _gate_expand():
    e = np.zeros((LANES, 3 * 512), np.float32)
    for h in range(NSA_H):
        for br in range(3):
            e[3 * h + br, 512 * br + 64 * h:512 * br + 64 * (h + 1)] = 1.0
    return jnp.asarray(e, BF16)


def _rows_to_tokens(o, nb, nq):
    o6 = o.reshape(nb, NSA_G, NSA_HPG, nq, NSA_G, NSA_DH)
    pick = jnp.stack([o6[:, g, :, :, g, :] for g in range(NSA_G)], axis=1)
    return pick.transpose(0, 3, 1, 2, 4).reshape(nb * nq, NSA_H * NSA_DH)


def _prompt_layer(x, lw, tabs, nb, t, lam_init, alpha):
    n = nb * t
    (u, qu, qr, ck, cv, sk, sv, wk, wv, gt, dq, dk, dv, skb, wkb, dkb, svt, wvt, dvt) = _inproj(
        x, lw["w_in"], tabs, nb, t, 256, True)
    ext = jnp.pad(u.reshape(nb, t, 256), ((0, 0), (CONV_PAD, 0), (0, 0)))
    yc = _conv(ext, lw["dw_w"], lw["dw_b"], lw["cln_g"], lw["cln_b"], t, 256).reshape(n, 256)
    ocmp, sel = _pcmp(qu, ck, cv, lw["cmp"], nb, t)
    osel = _flash("sel", qr, skb, svt, nb, t, extra=(sel,))
    owin = _flash("win", qr, wkb, wvt, nb, t)
    odiff = _flash("diff", dq, dkb, dvt, nb, t, extra=lw["lams"] + (lw["sg_col"],), lam_init=lam_init)
    x1 = _outproj(x, yc, ocmp, osel, owin, gt, odiff, lw["gate_e"], lw["w_out"], lw["ln1_g"], lw["ln1_b"], alpha, 256)
    x2 = _ffn(x1, lw["w1"], lw["w3"], lw["w2"], lw["ln2_g"], lw["ln2_b"], alpha, 512, 2)
    nk = min(WINDOW, t)
    news = (ck.reshape(nb, t, NSA_G, NSA_DH), cv.reshape(nb, t, NSA_G, NSA_DH),
            sk.reshape(nb, t, NSA_G, NSA_DH), sv.reshape(nb, t, NSA_G, NSA_DH),
            dk.reshape(nb, t, DIFF_H, 2 * DIFF_DQK), dv.reshape(nb, t, DIFF_H, DIFF_DV),
            wk.reshape(nb, t, NSA_G, NSA_DH)[:, t - nk:], wv.reshape(nb, t, NSA_G, NSA_DH)[:, t - nk:],
            u.reshape(nb, t, 256)[:, t - (CONV_W - 1):])
    return x2, news


def _sample_layer(x, lw, tabs, caches, states, pt, layer, nb, nq, past, lam_init, alpha):
    n = nb * nq
    (u, qu, qr, ck, cv, sk, sv, wk, wv, gt, dq, dk, dv) = _inproj(x, lw["w_in"], tabs, nb, nq, n, False)
    c_cmp_k, c_cmp_v, c_sel_k, c_sel_v, c_diff_k, c_diff_v = caches
    st_wk, st_wv, st_conv = states
    ext = jnp.concatenate([jnp.zeros((nb, CONV_PAD - (CONV_W - 1), 256), F32), st_conv, u.reshape(nb, nq, 256)], axis=1)
    yc = _conv(ext, lw["dw_w"], lw["dw_b"], lw["cln_g"], lw["cln_b"], nq, nq).reshape(n, 256)
    ocmp, sel = _scmp(pt, qu, c_cmp_k, c_cmp_v, lw["cmp"], layer, nb, nq, past)
    wb = st_wk.shape[1]
    osel, owin = _sselwin(pt, qr, sel, sk, sv, wk, wv, st_wk.reshape(nb, wb, LANES), st_wv.reshape(nb, wb, LANES),
                          c_sel_k, c_sel_v, layer, nb, nq, past)
    odiff = _sdiff(pt, dq, dk, dv, lw["lams"], lw["sg_row"], c_diff_k, c_diff_v, layer, nb, nq, past, lam_init)
    x1 = _outproj(x, yc, _rows_to_tokens(ocmp, nb, nq), _rows_to_tokens(osel, nb, nq), _rows_to_tokens(owin, nb, nq),
                  gt, odiff, lw["gate_e"], lw["w_out"], lw["ln1_g"], lw["ln1_b"], alpha, n)
    x2 = _ffn(x1, lw["w1"], lw["w3"], lw["w2"], lw["ln2_g"], lw["ln2_b"], alpha, n, 2)
    new_wk = jnp.concatenate([st_wk, wk.reshape(nb, nq, NSA_G, NSA_DH)], axis=1)[:, -wb:]
    new_wv = jnp.concatenate([st_wv, wv.reshape(nb, nq, NSA_G, NSA_DH)], axis=1)[:, -wb:]
    new_conv = jnp.concatenate([st_conv, u.reshape(nb, nq, 256)], axis=1)[:, -(CONV_W - 1):]
    news = (ck.reshape(nb, nq, NSA_G, NSA_DH), cv.reshape(nb, nq, NSA_G, NSA_DH),
            sk.reshape(nb, nq, NSA_G, NSA_DH), sv.reshape(nb, nq, NSA_G, NSA_DH),
            dk.reshape(nb, nq, DIFF_H, 2 * DIFF_DQK), dv.reshape(nb, nq, DIFF_H, DIFF_DV),
            new_wk, new_wv, new_conv)
    return x2, news


def kernel(x_prompt, x_sample, cache_nsa_cmp_k, cache_nsa_cmp_v, cache_nsa_sel_k, cache_nsa_sel_v, cache_diff_k, cache_diff_v, state_nsa_win_k, state_nsa_win_v, state_conv, page_table, w_in, conv_dw_w, conv_dw_b, conv_ln_g, conv_ln_b, cmp_pe_k, cmp_w1_k, cmp_b1_k, cmp_w2_k, cmp_pe_v, cmp_w1_v, cmp_b1_v, cmp_w2_v, diff_lq1, diff_lk1, diff_lq2, diff_lk2, diff_subln_g, w_out, ln1_g, ln1_b, ln2_g, ln2_b, ffn_w1, ffn_w3, ffn_w2):
    nb, t, d = x_prompt.shape
    sb, nq, _ = x_sample.shape
    depth = w_in.shape[0]
    n_pool = cache_nsa_cmp_k.shape[1]
    past = page_table.shape[1] * PAGE
    alpha = (2 * depth) ** 0.25
    tabs_p = _rope_tables(jnp.arange(t, dtype=jnp.int32))
    tabs_s = _rope_tables(jnp.tile(past + jnp.arange(nq, dtype=jnp.int32), sb))
    caches = (cache_nsa_cmp_k.reshape(depth, n_pool, PAGE, LANES), cache_nsa_cmp_v.reshape(depth, n_pool, PAGE, LANES),
              cache_nsa_sel_k.reshape(depth, n_pool, PAGE, LANES), cache_nsa_sel_v.reshape(depth, n_pool, PAGE, LANES),
              cache_diff_k.reshape(depth, n_pool, PAGE, 256), cache_diff_v.reshape(depth, n_pool, PAGE, 256))
    gate_e = _gate_expand()
    xp = x_prompt.reshape(nb * t, d)
    xs = x_sample.reshape(sb * nq, d)
    outs_p, outs_s = [], []
    for l in range(depth):
        ck = _prep_cmp(cmp_pe_k[l], cmp_w1_k[l], cmp_b1_k[l], cmp_w2_k[l])
        cv = _prep_cmp(cmp_pe_v[l], cmp_w1_v[l], cmp_b1_v[l], cmp_w2_v[l])
        names = ("pa", "pb", "wa", "wb", "b1", "w2")
        cmpw = {n_ + "k": a for n_, a in zip(names, ck)}
        cmpw.update({n_ + "v": a for n_, a in zip(names, cv)})
        lw = dict(
            w_in=_prep_w_in(w_in[l]), dw_w=conv_dw_w[l], dw_b=conv_dw_b[l][None], cln_g=conv_ln_g[l][None],
            cln_b=conv_ln_b[l][None], cmp=cmpw,
            lams=(diff_lq1[l][None], diff_lk1[l][None], diff_lq2[l][None], diff_lk2[l][None]),
            sg_col=diff_subln_g[l][:, None], sg_row=jnp.tile(diff_subln_g[l], DIFF_H)[None],
            gate_e=gate_e, w_out=w_out[l].astype(BF16), ln1_g=ln1_g[l][None], ln1_b=ln1_b[l][None],
            ln2_g=ln2_g[l][None], ln2_b=ln2_b[l][None],
            w1=ffn_w1[l].astype(BF16), w3=ffn_w3[l].astype(BF16), w2=ffn_w2[l].astype(BF16))
        lam_init = 0.8 - 0.6 * math.exp(-0.3 * l)
        xp, new_p = _prompt_layer(xp, lw, tabs_p, nb, t, lam_init, alpha)
        xs, new_s = _sample_layer(xs, lw, tabs_s, caches, (state_nsa_win_k[l], state_nsa_win_v[l], state_conv[l]),
                                  page_table, l, sb, nq, past, lam_init, alpha)
        outs_p.append(new_p)
        outs_s.append(new_s)
    stk_p = [jnp.stack([o[i] for o in outs_p]) for i in range(9)]
    stk_s = [jnp.stack([o[i] for o in outs_s]) for i in range(9)]
    return (xp.reshape(nb, t, d), xs.reshape(sb, nq, d), *stk_p, *stk_s)
```

```python
import functools
import math

import jax
import jax.numpy as jnp
import numpy as np
from jax import lax
from jax.experimental import pallas as pl
from jax.experimental.pallas import tpu as pltpu

F32 = jnp.float32
BF16 = jnp.bfloat16

CONV_W = 31
NSA_H = 8
NSA_G = 2
NSA_HPG = NSA_H // NSA_G
NSA_DH = 64
CMP_STRIDE = 16
CMP_LEN = 32
CMP_HID = 128
SEL_BLK = 64
SEL_TOPN = 16
WINDOW = 512
FORCE_SCORE = 1e9
DIFF_H = 4
DIFF_DV = 64
DIFF_DQK = 32
ROPE_THETA = 10000.0
LN_EPS = 1e-5
PAGE = 128

LANES = 128
TQ = 128
TK = 256
LOG2E = 1.4426950408889634
NEG = -1e30
VMEM_LIMIT = 52 * 1024 * 1024

C_CA, C_CG, C_Q, C_KV, C_GT, C_DQ, C_DK, C_DV, C_END = 0, 256, 512, 1536, 2304, 2432, 3456, 3712, 3968


def _cparams(sem):
    return pltpu.CompilerParams(dimension_semantics=sem, vmem_limit_bytes=VMEM_LIMIT)


def _sigmoid(x):
    return 1.0 / (1.0 + jnp.exp(-x))


def _ln_rows(x, g, b):
    mu = jnp.mean(x, axis=-1, keepdims=True)
    xc = x - mu
    var = jnp.mean(xc * xc, axis=-1, keepdims=True)
    return xc * lax.rsqrt(var + LN_EPS) * g + b


def _dot(a, b):
    return jnp.dot(a, b, preferred_element_type=F32)


def _dot_nt(a, b):
    return lax.dot_general(a, b, (((1,), (1,)), ((), ())), preferred_element_type=F32)


def _split3_dot(x, m):
    hi = x.astype(BF16)
    r1 = x - hi.astype(F32)
    mid = r1.astype(BF16)
    lo = (r1 - mid.astype(F32)).astype(BF16)
    return _dot(hi, m) + _dot(mid, m) + _dot(lo, m)


def _rope(x, cos, sin_signed, half):
    lane = lax.broadcasted_iota(jnp.int32, x.shape, 1)
    first = (lane & (2 * half - 1)) < half
    rot = jnp.where(first, pltpu.roll(x, LANES - half, 1), pltpu.roll(x, half, 1))
    return x * cos + rot * sin_signed


def _inproj_body(x_ref, w_ref, c64_ref, s64_ref, c32_ref, s32_ref,
                 u_ref, qu_ref, qr_ref, ck_ref, cv_ref, sk_ref, sv_ref, wk_ref, wv_ref,
                 gt_ref, dq_ref, dk_ref, dv_ref, *t_refs, dq_scale):
    xb = x_ref[...].astype(BF16)

    def mm(lo, hi):
        return _dot(xb, w_ref[:, lo:hi])

    c64, s64, c32, s32 = c64_ref[...], s64_ref[...], c32_ref[...], s32_ref[...]
    z = mm(C_CA, C_Q)
    u_ref[...] = z[:, :256] * _sigmoid(z[:, 256:])
    for h in range(8):
        zq = mm(C_Q + LANES * h, C_Q + LANES * (h + 1))
        qu_ref[:, LANES * h:LANES * (h + 1)] = zq.astype(BF16)
        qr_ref[:, LANES * h:LANES * (h + 1)] = _rope(zq, c64, s64, 32).astype(BF16)
    z = mm(C_KV, C_GT)
    ck_ref[...] = z[:, 0:128]
    cv_ref[...] = z[:, 128:256]
    sk = _rope(z[:, 256:384], c64, s64, 32)
    sk_ref[...] = sk
    sv = z[:, 384:512]
    sv_ref[...] = sv
    wk = _rope(z[:, 512:640], c64, s64, 32)
    wk_ref[...] = wk
    wv = z[:, 640:768]
    wv_ref[...] = wv
    gt_ref[...] = _sigmoid(mm(C_GT, C_DQ))
    for h in range(8):
        zq = mm(C_DQ + LANES * h, C_DQ + LANES * (h + 1))
        dq_ref[:, LANES * h:LANES * (h + 1)] = (_rope(zq, c32, s32, 16) * dq_scale).astype(BF16)
    z = mm(C_DK, C_DV)
    dk0 = _rope(z[:, :128], c32, s32, 16)
    dk1 = _rope(z[:, 128:], c32, s32, 16)
    dk_ref[:, :128] = dk0
    dk_ref[:, 128:] = dk1
    dv = mm(C_DV, C_END)
    dv_ref[...] = dv
    if t_refs:
        skb_ref, wkb_ref, dkb_ref, svt_ref, wvt_ref, dvt_ref = t_refs
        skb_ref[...] = sk.astype(BF16)
        wkb_ref[...] = wk.astype(BF16)
        dkb_ref[:, :128] = dk0.astype(BF16)
        dkb_ref[:, 128:] = dk1.astype(BF16)
        svt_ref[...] = sv.T.astype(BF16)
        wvt_ref[...] = wv.T.astype(BF16)
        dvt_ref[:128, :] = dv[:, :128].T.astype(BF16)
        dvt_ref[128:, :] = dv[:, 128:].T.astype(BF16)


def _inproj(x, w, tabs, nb, t, tm, with_t):
    n, d = x.shape
    nt = t // tm if with_t else 1
    grid = (n // tm,)
    row = lambda c: pl.BlockSpec((tm, c), lambda i: (i, 0))
    tab = pl.BlockSpec((tm, LANES), (lambda i: (i % nt, 0)) if with_t else (lambda i: (i, 0)))
    in_specs = [row(d), pl.BlockSpec((d, C_END), lambda i: (0, 0)), tab, tab, tab, tab]
    shapes = [(256, F32), (1024, BF16), (1024, BF16)] + [(128, F32)] * 6 + [(128, F32), (1024, BF16), (256, F32), (256, F32)]
    out_shape = [jax.ShapeDtypeStruct((n, c), dt) for c, dt in shapes]
    out_specs = [row(c) for c, _ in shapes]
    if with_t:
        out_shape += [jax.ShapeDtypeStruct((n, 128), BF16), jax.ShapeDtypeStruct((n, 128), BF16),
                      jax.ShapeDtypeStruct((n, 256), BF16)]
        out_specs += [row(128), row(128), row(256)]
        for c in (128, 128, 256):
            out_shape.append(jax.ShapeDtypeStruct((nb, c, t), BF16))
            out_specs.append(pl.BlockSpec((None, c, tm), lambda i: (i // nt, 0, i % nt)))
    return pl.pallas_call(
        functools.partial(_inproj_body, dq_scale=DIFF_DQK ** -0.5 * LOG2E), name="inproj",
        grid=grid, in_specs=in_specs, out_specs=out_specs, out_shape=out_shape,
        compiler_params=_cparams(("parallel",)),
    )(x, w, *tabs)


CONV_PAD = 32


def _conv_body(ext_ref, w_ref, b_ref, g_ref, beta_ref, y_ref, acc_ref, *, tt, rs):
    t0 = pl.multiple_of(pl.program_id(1) * tt, 8)
    off = CONV_PAD - (CONV_W - 1)
    for c in range(2):
        cs = slice(LANES * c, LANES * (c + 1))
        for r in range(tt // rs):
            win = ext_ref[pl.ds(t0 + rs * r, rs + CONV_PAD), cs]
            acc = jnp.zeros((rs, LANES), F32)
            for k in range(CONV_W):
                acc = acc + win[off + k:off + k + rs, :] * w_ref[k:k + 1, cs]
            acc_ref[rs * r:rs * (r + 1), cs] = acc
    y = _ln_rows(acc_ref[...] + b_ref[...], g_ref[...], beta_ref[...])
    y_ref[...] = y * _sigmoid(y)


def _conv(ext, w, b, g, beta, t, tt):
    nb, le, c = ext.shape
    rs = min(tt, 64)
    vec = pl.BlockSpec((1, c), lambda bi, ti: (0, 0))
    return pl.pallas_call(
        functools.partial(_conv_body, tt=tt, rs=rs), name="conv",
        grid=(nb, t // tt),
        in_specs=[pl.BlockSpec((None, le, c), lambda bi, ti: (bi, 0, 0)),
                  pl.BlockSpec((CONV_W, c), lambda bi, ti: (0, 0)), vec, vec, vec],
        out_specs=pl.BlockSpec((None, tt, c), lambda bi, ti: (bi, ti, 0)),
        out_shape=jax.ShapeDtypeStruct((nb, t, c), F32),
        scratch_shapes=[pltpu.VMEM((tt, c), F32)],
        compiler_params=_cparams(("parallel", "parallel")),
    )(ext, w, b, g, beta)


def _compress(src_ref, n16, pea, peb, wa_ref, wb_ref, b1, w2_ref):
    x = jnp.concatenate([src_ref[pl.ds(p, n16, stride=CMP_STRIDE), :] for p in range(CMP_STRIDE)], axis=1)
    a = _dot((x + pea).astype(BF16), wa_ref[...])
    bm = _dot((x + peb).astype(BF16), wb_ref[...])
    h = a + pltpu.roll(bm, n16 - 1, 0) + b1
    gl = 0.5 * h * (1.0 + jnp.tanh(0.7978845608028654 * (h + 0.044715 * (h * h * h))))
    return _dot(gl.astype(BF16), w2_ref[...])


def _top_rows(score, ridx, k):
    cnt = jnp.zeros(score.shape, F32)
    for jp in range(score.shape[0]):
        row = score[jp:jp + 1, :]
        beats = (row > score) | ((row == score) & (ridx > jp))
        cnt = cnt + beats.astype(F32)
    return cnt < k


def _pcmp_body(qu_ref, ck_ref, cv_ref, pak_ref, pbk_ref, wak_ref, wbk_ref, b1k_ref, w2k_ref,
               pav_ref, pbv_ref, wav_ref, wbv_ref, b1v_ref, w2v_ref, mt_ref,
               o_ref, sel_ref, kcc_ref, vcct_ref, *, n16, nsel):
    ti = pl.program_id(1)

    @pl.when(ti == 0)
    def _():
        kcc = _compress(ck_ref, n16, pak_ref[...], pbk_ref[...], wak_ref, wbk_ref, b1k_ref[...], w2k_ref)
        kcc_ref[...] = kcc.astype(BF16)
        vcc = _compress(cv_ref, n16, pav_ref[...], pbv_ref[...], wav_ref, wbv_ref, b1v_ref[...], w2v_ref)
        vcct_ref[...] = vcc.T.astype(BF16)

    q = qu_ref[...]
    kcc = kcc_ref[...]
    vcct = vcct_ref[...]
    nq = 4 * TQ
    cidx = lax.broadcasted_iota(jnp.int32, (n16, nq), 0)
    qpos = ti * TQ + (lax.broadcasted_iota(jnp.int32, (n16, nq), 1) & (TQ - 1))
    vis = (CMP_STRIDE * cidx + CMP_LEN - 1 <= qpos) & (cidx < n16 - 1)
    jidx = lax.broadcasted_iota(jnp.int32, (nsel, TQ), 0)
    qp1 = ti * TQ + lax.broadcasted_iota(jnp.int32, (nsel, TQ), 1)
    cur = qp1 >> 6
    forced = (jidx == 0) | (jidx == cur) | (jidx == cur - 1)
    for g in range(NSA_G):
        qs = jnp.concatenate([q[:, LANES * (4 * g + a):LANES * (4 * g + a + 1)] for a in range(4)], axis=0)
        st = _dot_nt(kcc, qs)
        st = jnp.where(vis, st, NEG)
        m = jnp.max(st, axis=0, keepdims=True)
        p = jnp.where(vis, jnp.exp2(st - m), 0.0)
        l = jnp.sum(p, axis=0, keepdims=True)
        p = p / jnp.maximum(l, 1e-30)
        ot = _dot(vcct, p.astype(BF16))
        for a2 in range(2):
            blk = jnp.concatenate([ot[64 * g:64 * g + 64, TQ * (2 * a2 + e):TQ * (2 * a2 + e + 1)] for e in range(2)], axis=0)
            o_ref[:, 256 * g + LANES * a2:256 * g + LANES * (a2 + 1)] = blk.T
        imp = p[:, 0:TQ] + p[:, TQ:2 * TQ] + p[:, 2 * TQ:3 * TQ] + p[:, 3 * TQ:4 * TQ]
        hi = imp.astype(BF16)
        r1 = imp - hi.astype(F32)
        mid = r1.astype(BF16)
        lo = (r1 - mid.astype(F32)).astype(BF16)
        mt = mt_ref[...]
        isel = _dot(mt, hi) + _dot(mt, mid) + _dot(mt, lo)
        score = jnp.where(forced, FORCE_SCORE, isel)
        score = jnp.where(jidx <= cur, score, -jnp.inf)
        sel = _top_rows(score, jidx, min(SEL_TOPN, nsel)) & (jidx <= cur)
        sel_ref[g] = jnp.where(sel, 0.0, NEG)


def _pcmp(qu, ck, cv, cw, nb, t):
    n16 = t // CMP_STRIDE
    nsel = t // SEL_BLK
    nt = t // TQ
    mt = np.zeros((nsel, n16), np.float32)
    for j in range(nsel):
        for c in range(4 * j - 1, 4 * j + 4):
            if 0 <= c < n16 - 1:
                mt[j, c] = 1.0
    mt = jnp.asarray(mt, BF16)
    const = lambda a: pl.BlockSpec(a.shape, lambda bi, ti: (0,) * a.ndim)
    wlist = [cw[k] for k in ("pak", "pbk", "wak", "wbk", "b1k", "w2k", "pav", "pbv", "wav", "wbv", "b1v", "w2v")] + [mt]
    return pl.pallas_call(
        functools.partial(_pcmp_body, n16=n16, nsel=nsel), name="pcmp",
        grid=(nb, nt),
        in_specs=[pl.BlockSpec((TQ, 1024), lambda bi, ti: (bi * nt + ti, 0)),
                  pl.BlockSpec((None, t, LANES), lambda bi, ti: (bi, 0, 0)),
                  pl.BlockSpec((None, t, LANES), lambda bi, ti: (bi, 0, 0))] + [const(a) for a in wlist],
        out_specs=[pl.BlockSpec((TQ, 512), lambda bi, ti: (bi * nt + ti, 0)),
                   pl.BlockSpec((None, None, NSA_G, nsel, TQ), lambda bi, ti: (bi, ti, 0, 0, 0))],
        out_shape=[jax.ShapeDtypeStruct((nb * t, 512), F32),
                   jax.ShapeDtypeStruct((nb, nt, NSA_G, nsel, TQ), F32)],
        scratch_shapes=[pltpu.VMEM((n16, LANES), BF16), pltpu.VMEM((LANES, n16), BF16)],
        compiler_params=_cparams(("parallel", "arbitrary")),
    )(qu, ck.reshape(nb, t, LANES), cv.reshape(nb, t, LANES), *wlist)


def _stack_queries(q, s):
    return jnp.concatenate([q[:, LANES * (4 * s + a):LANES * (4 * s + a + 1)] for a in range(4)], axis=0)


def _flash_out(z, s, o_ref):
    for a2 in range(2):
        blk = jnp.concatenate([z[64 * s:64 * s + 64, TQ * (2 * a2 + e):TQ * (2 * a2 + e + 1)] for e in range(2)], axis=0)
        o_ref[:, 256 * s + LANES * a2:256 * s + LANES * (a2 + 1)] = blk.T


def _flash_body(*refs, mode, lam_init):
    if mode == "sel":
        q_ref, k_ref, vt_ref, selb_ref, o_ref, acc_ref, m_ref, l_ref = refs
    else:
        q_ref, k_ref, vt_ref, lq1_ref, lk1_ref, lq2_ref, lk2_ref, sg_ref, o_ref, acc_ref, m_ref, l_ref = refs
    ti = pl.program_id(1)
    nq = 4 * TQ
    q = q_ref[...]
    qs = [_stack_queries(q, s) for s in range(2)]
    krow = lax.broadcasted_iota(jnp.int32, (TK, nq), 0)
    qcol = lax.broadcasted_iota(jnp.int32, (TK, nq), 1) & (TQ - 1)
    m_ref[...] = jnp.full(m_ref.shape, NEG, F32)
    l_ref[...] = jnp.zeros(l_ref.shape, F32)
    acc_ref[...] = jnp.zeros(acc_ref.shape, F32)

    def step(j, causal):
        k0 = pl.multiple_of(j * TK, TK)
        for s in range(2):
            koff = LANES * s if mode == "diff" else 0
            st = _dot_nt(k_ref[pl.ds(k0, TK), koff:koff + LANES], qs[s])
            if mode == "sel":
                rows = [jnp.broadcast_to(selb_ref[s, pl.ds((TK // SEL_BLK) * j + r, 1), :], (SEL_BLK, TQ))
                        for r in range(TK // SEL_BLK)]
                bias = jnp.concatenate(rows, axis=0)
                st = st + jnp.concatenate([bias, bias, bias, bias], axis=1)
            if causal:
                st = jnp.where(ti * TQ + qcol >= j * TK + krow, st, NEG)
            m_old = m_ref[s]
            m_new = jnp.maximum(m_old, jnp.max(st, axis=0, keepdims=True))
            alpha = jnp.exp2(m_old - m_new)
            p = jnp.exp2(st - m_new)
            l_ref[s] = alpha * l_ref[s] + jnp.sum(p, axis=0, keepdims=True)
            acc_ref[s] = alpha * acc_ref[s] + _dot(vt_ref[koff:koff + LANES, pl.ds(k0, TK)], p.astype(BF16))
            m_ref[s] = m_new

    n_full = (ti * TQ) // TK
    lax.fori_loop(0, n_full, lambda j, c: (step(j, False), c)[1], 0)
    step(n_full, True)

    if mode == "diff":
        lam = (jnp.exp(jnp.sum(lq1_ref[...] * lk1_ref[...], keepdims=True))
               - jnp.exp(jnp.sum(lq2_ref[...] * lk2_ref[...], keepdims=True)) + lam_init)
    for s in range(2):
        z = acc_ref[s] / l_ref[s]
        if mode == "diff":
            halves = []
            for hh in range(2):
                zr = z[64 * hh:64 * hh + 64, :]
                d = zr[:, TQ * (2 * hh):TQ * (2 * hh + 1)] - lam * zr[:, TQ * (2 * hh + 1):TQ * (2 * hh + 2)]
                ms = jnp.mean(d * d, axis=0, keepdims=True)
                halves.append(d * lax.rsqrt(ms + LN_EPS) * sg_ref[...] * (1.0 - lam_init))
            o_ref[:, LANES * s:LANES * (s + 1)] = jnp.concatenate(halves, axis=0).T
        else:
            _flash_out(z, s, o_ref)


def _flash(mode, q, k, vt, nb, t, extra=(), lam_init=0.0):
    nt = t // TQ
    kw = k.shape[-1]
    ow = 256 if mode == "diff" else 512
    in_specs = [pl.BlockSpec((TQ, 1024), lambda bi, ti: (bi * nt + ti, 0)),
                pl.BlockSpec((None, t, kw), lambda bi, ti: (bi, 0, 0)),
                pl.BlockSpec((None, kw, t), lambda bi, ti: (bi, 0, 0))]
    if mode == "sel":
        nsel = t // SEL_BLK
        in_specs.append(pl.BlockSpec((None, None, NSA_G, nsel, TQ), lambda bi, ti: (bi, ti, 0, 0, 0)))
    else:
        in_specs += [pl.BlockSpec(a.shape, lambda bi, ti: (0, 0)) for a in extra]
    return pl.pallas_call(
        functools.partial(_flash_body, mode=mode, lam_init=lam_init), name="flash_" + mode,
        grid=(nb, nt), in_specs=in_specs,
        out_specs=pl.BlockSpec((TQ, ow), lambda bi, ti: (bi * nt + ti, 0)),
        out_shape=jax.ShapeDtypeStruct((nb * t, ow), F32),
        scratch_shapes=[pltpu.VMEM((2, LANES, 4 * TQ), F32), pltpu.VMEM((2, 1, 4 * TQ), F32), pltpu.VMEM((2, 1, 4 * TQ), F32)],
        compiler_params=_cparams(("parallel", "parallel")),
    )(q, k.reshape(nb, t, kw), vt, *extra)


def _win_body(q_ref, k_ref, vt_ref, o_ref):
    ti = pl.program_id(1)
    nq = 4 * TQ
    span = WINDOW + TQ
    k0 = pl.multiple_of(jnp.maximum(ti * TQ - WINDOW, 0), TQ)
    q = q_ref[...]
    kt = k_ref[pl.ds(k0, span), :]
    vt = vt_ref[:, pl.ds(k0, span)]
    krow = lax.broadcasted_iota(jnp.int32, (span, nq), 0)
    qcol = lax.broadcasted_iota(jnp.int32, (span, nq), 1) & (TQ - 1)
    dlt = ti * TQ + qcol - (k0 + krow)
    vis = (dlt >= 0) & (dlt < WINDOW)
    for s in range(2):
        st = jnp.where(vis, _dot_nt(kt, _stack_queries(q, s)), NEG)
        m = jnp.max(st, axis=0, keepdims=True)
        p = jnp.exp2(st - m)
        l = jnp.sum(p, axis=0, keepdims=True)
        _flash_out(_dot(vt, p.astype(BF16)) / l, s, o_ref)


def _win(q, k, vt, nb, t):
    nt = t // TQ
    assert t >= WINDOW + TQ
    return pl.pallas_call(
        _win_body, name="flash_win", grid=(nb, nt),
        in_specs=[pl.BlockSpec((TQ, 1024), lambda bi, ti: (bi * nt + ti, 0)),
                  pl.BlockSpec((None, t, LANES), lambda bi, ti: (bi, 0, 0)),
                  pl.BlockSpec((None, LANES, t), lambda bi, ti: (bi, 0, 0))],
        out_specs=pl.BlockSpec((TQ, 512), lambda bi, ti: (bi * nt + ti, 0)),
        out_shape=jax.ShapeDtypeStruct((nb * t, 512), F32),
        compiler_params=_cparams(("parallel", "parallel")),
    )(q, k.reshape(nb, t, LANES), vt)


def _outproj_body(x_ref, yc_ref, oc_ref, os_ref, ow_ref, gt_ref, od_ref, e_ref, w_ref, g_ref, b_ref, o_ref, *, alpha):
    gt = gt_ref[...]
    hi = gt.astype(BF16)
    lo = (gt - hi.astype(F32)).astype(BF16)
    e = e_ref[...]
    gx = _dot(hi, e) + _dot(lo, e)
    onsa = gx[:, 0:512] * oc_ref[...] + gx[:, 512:1024] * os_ref[...] + gx[:, 1024:1536] * ow_ref[...]
    mix = (_dot(yc_ref[...].astype(BF16), w_ref[0:256, :]) + _dot(onsa.astype(BF16), w_ref[256:768, :])
           + _dot(od_ref[...].astype(BF16), w_ref[768:1024, :]))
    o_ref[...] = _ln_rows(alpha * x_ref[...] + mix, g_ref[...], b_ref[...])


def _outproj(x, yc, oc, osel, ow, gt, od, e, w, g, b, alpha, tm):
    n, d = x.shape
    row = lambda c: pl.BlockSpec((tm, c), lambda i: (i, 0))
    const = lambda a: pl.BlockSpec(a.shape, lambda i: (0, 0))
    return pl.pallas_call(
        functools.partial(_outproj_body, alpha=alpha), name="outproj",
        grid=(n // tm,),
        in_specs=[row(d), row(256), row(512), row(512), row(512), row(128), row(256), const(e), const(w), const(g), const(b)],
        out_specs=row(d), out_shape=jax.ShapeDtypeStruct((n, d), F32),
        compiler_params=_cparams(("parallel",)),
    )(x, yc, oc, osel, ow, gt, od, e, w, g, b)


def _ffn_body(x_ref, w1_ref, w3_ref, w2_ref, g_ref, b_ref, o_ref, acc_ref, *, alpha):
    f = pl.program_id(1)
    xb = x_ref[...].astype(BF16)
    h1 = _dot(xb, w1_ref[...])
    h = h1 * _sigmoid(h1) * _dot(xb, w3_ref[...])
    part = _dot(h.astype(BF16), w2_ref[...])

    @pl.when(f == 0)
    def _():
        acc_ref[...] = part

    @pl.when(f != 0)
    def _():
        acc_ref[...] += part

    @pl.when(f == pl.num_programs(1) - 1)
    def _():
        o_ref[...] = _ln_rows(alpha * x_ref[...] + acc_ref[...], g_ref[...], b_ref[...])


def _ffn(x, w1, w3, w2, g, b, alpha, tm, nf):
    n, d = x.shape
    dff = w1.shape[1]
    tf = dff // nf
    return pl.pallas_call(
        functools.partial(_ffn_body, alpha=alpha), name="ffn",
        grid=(n // tm, nf),
        in_specs=[pl.BlockSpec((tm, d), lambda i, f: (i, 0)),
                  pl.BlockSpec((d, tf), lambda i, f: (0, f)),
                  pl.BlockSpec((d, tf), lambda i, f: (0, f)),
                  pl.BlockSpec((tf, d), lambda i, f: (f, 0)),
                  pl.BlockSpec((1, d), lambda i, f: (0, 0)),
                  pl.BlockSpec((1, d), lambda i, f: (0, 0))],
        out_specs=pl.BlockSpec((tm, d), lambda i, f: (i, 0)),
        out_shape=jax.ShapeDtypeStruct((n, d), F32),
        scratch_shapes=[pltpu.VMEM((tm, d), F32)],
        compiler_params=_cparams(("parallel", "arbitrary")),
    )(x, w1, w3, w2, g, b)


def _gather_pages(pt_ref, b, n_pages, srcs, layer):
    copies = []
    for hbm, buf, sem in srcs:
        for j in range(n_pages):
            copies.append(pltpu.make_async_copy(hbm.at[layer, pt_ref[b, j]], buf.at[:, pl.ds(PAGE * j, PAGE)], sem))
    for c in copies:
        c.start()
    for c in copies:
        c.wait()


def _stack_heads(q):
    return jnp.concatenate([q[:, LANES * h:LANES * (h + 1)] for h in range(8)], axis=0)


def _scmp_body(pt_ref, qu_ref, ck_hbm, cv_hbm, pak_ref, pbk_ref, wak_ref, wbk_ref, b1k_ref, w2k_ref,
               pav_ref, pbv_ref, wav_ref, wbv_ref, b1v_ref, w2v_ref, m_ref,
               o_ref, sel_ref, kbt, vbt, kbuf, vbuf, sem, *, layer, n_pages, past, nq):
    b = pl.program_id(0)
    _gather_pages(pt_ref, b, n_pages, [(ck_hbm, kbt, sem.at[0]), (cv_hbm, vbt, sem.at[1])], layer)
    for j in range(n_pages):
        kbuf[PAGE * j:PAGE * (j + 1), :] = kbt[:, PAGE * j:PAGE * (j + 1)].T
        vbuf[PAGE * j:PAGE * (j + 1), :] = vbt[:, PAGE * j:PAGE * (j + 1)].T
    n16 = past // CMP_STRIDE
    kcc = _compress(kbuf, n16, pak_ref[...], pbk_ref[...], wak_ref, wbk_ref, b1k_ref[...], w2k_ref).astype(BF16)
    vcc = _compress(vbuf, n16, pav_ref[...], pbv_ref[...], wav_ref, wbv_ref, b1v_ref[...], w2v_ref).astype(BF16)
    qa = _stack_heads(qu_ref[...])
    rows = 8 * nq
    s = _dot_nt(qa, kcc)
    cidx = lax.broadcasted_iota(jnp.int32, (rows, n16), 1)
    qpos = past + (lax.broadcasted_iota(jnp.int32, (rows, n16), 0) & (nq - 1))
    vis = (CMP_STRIDE * cidx + CMP_LEN - 1 <= qpos) & (cidx < n16 - 1)
    s = jnp.where(vis, s, NEG)
    m = jnp.max(s, axis=-1, keepdims=True)
    p = jnp.where(vis, jnp.exp2(s - m), 0.0)
    l = jnp.sum(p, axis=-1, keepdims=True)
    p = p / jnp.maximum(l, 1e-30)
    o_ref[...] = _dot(p.astype(BF16), vcc)
    imp = jnp.concatenate(
        [p[nq * 4 * g:nq * (4 * g + 1)] + p[nq * (4 * g + 1):nq * (4 * g + 2)]
         + p[nq * (4 * g + 2):nq * (4 * g + 3)] + p[nq * (4 * g + 3):nq * (4 * g + 4)] for g in range(NSA_G)], axis=0)
    isel = _split3_dot(imp, m_ref[...])
    nselp = isel.shape[1]
    jidx = lax.broadcasted_iota(jnp.int32, (NSA_G * nq, nselp), 1)
    qp = past + (lax.broadcasted_iota(jnp.int32, (NSA_G * nq, nselp), 0) & (nq - 1))
    cur = qp >> 6
    forced = (jidx == 0) | (jidx == cur) | (jidx == cur - 1)
    score = jnp.where(forced, FORCE_SCORE, isel)
    score = jnp.where(jidx <= cur, score, -jnp.inf)
    n_sel = -(-(past + nq) // SEL_BLK)
    sel = jnp.zeros(score.shape, F32)
    for _ in range(min(SEL_TOPN, n_sel)):
        mx = jnp.max(score, axis=-1, keepdims=True)
        first = jnp.min(jnp.where(score == mx, jidx, nselp), axis=-1, keepdims=True)
        hit = jidx == first
        sel = jnp.where(hit, 1.0, sel)
        score = jnp.where(hit, -jnp.inf, score)
    sel_ref[...] = jnp.where(jidx <= cur, sel, 0.0)


def _scmp(pt, qu, cache_k, cache_v, cw, layer, nb, nq, past):
    n_pages = past // PAGE
    n16 = past // CMP_STRIDE
    n_sel = -(-(past + nq) // SEL_BLK)
    nselp = -(-n_sel // LANES) * LANES
    m = np.zeros((n16, nselp), np.float32)
    for j in range(n_sel):
        for c in range(4 * j - 1, 4 * j + 4):
            if 0 <= c < n16 - 1:
                m[c, j] = 1.0
    m = jnp.asarray(m, BF16)
    wlist = [cw[k] for k in ("pak", "pbk", "wak", "wbk", "b1k", "w2k", "pav", "pbv", "wav", "wbv", "b1v", "w2v")] + [m]
    const = lambda a: pl.BlockSpec(a.shape, lambda bi, pt_: (0,) * a.ndim)
    any_spec = pl.BlockSpec(memory_space=pl.ANY)
    grid_spec = pltpu.PrefetchScalarGridSpec(
        num_scalar_prefetch=1, grid=(nb,),
        in_specs=[pl.BlockSpec((nq, 1024), lambda bi, pt_: (bi, 0)), any_spec, any_spec] + [const(a) for a in wlist],
        out_specs=[pl.BlockSpec((None, 8 * nq, LANES), lambda bi, pt_: (bi, 0, 0)),
                   pl.BlockSpec((None, NSA_G * nq, nselp), lambda bi, pt_: (bi, 0, 0))],
        scratch_shapes=[pltpu.VMEM((LANES, past), F32), pltpu.VMEM((LANES, past), F32),
                        pltpu.VMEM((past, LANES), F32), pltpu.VMEM((past, LANES), F32), pltpu.SemaphoreType.DMA((2,))])
    return pl.pallas_call(
        functools.partial(_scmp_body, layer=layer, n_pages=n_pages, past=past, nq=nq), name="scmp",
        grid_spec=grid_spec,
        out_shape=[jax.ShapeDtypeStruct((nb, 8 * nq, LANES), F32), jax.ShapeDtypeStruct((nb, NSA_G * nq, nselp), F32)],
        compiler_params=_cparams(("arbitrary",)),
    )(pt, qu, cache_k, cache_v, *wlist)


def _softmax_rows(s):
    m = jnp.max(s, axis=-1, keepdims=True)
    p = jnp.exp2(s - m)
    return p, jnp.sum(p, axis=-1, keepdims=True)


def _sselwin_body(pt_ref, qr_ref, sel_ref, skn_ref, svn_ref, wkn_ref, wvn_ref, wks_ref, wvs_ref, e_ref, sk_hbm, sv_hbm,
                  osel_ref, owin_ref, kbuf, vbuf, sem, *, layer, n_pages, past, nq):
    b = pl.program_id(0)
    ktot = past + PAGE
    rows = 8 * nq
    pad = jnp.zeros((PAGE - nq, LANES), F32)
    kbuf[:, pl.ds(past, PAGE)] = jnp.concatenate([skn_ref[...], pad], axis=0).T
    vbuf[:, pl.ds(past, PAGE)] = jnp.concatenate([svn_ref[...], pad], axis=0).T
    _gather_pages(pt_ref, b, n_pages, [(sk_hbm, kbuf, sem.at[0]), (sv_hbm, vbuf, sem.at[1])], layer)
    qa = _stack_heads(qr_ref[...])
    s = _dot(qa, kbuf[...].astype(BF16))
    se = _dot(sel_ref[...].astype(BF16), e_ref[...])
    se = jnp.concatenate([se[nq * g:nq * (g + 1)] for g in range(NSA_G) for _ in range(NSA_HPG)], axis=0)
    kpos = lax.broadcasted_iota(jnp.int32, (rows, ktot), 1)
    qpos = past + (lax.broadcasted_iota(jnp.int32, (rows, ktot), 0) & (nq - 1))
    s = jnp.where((se > 0.5) & (kpos <= qpos), s, NEG)
    p, l = _softmax_rows(s)
    osel_ref[...] = _dot_nt(p.astype(BF16), vbuf[...].astype(BF16)) / l
    wb = wks_ref.shape[0]
    kw = jnp.concatenate([wks_ref[...], wkn_ref[...], pad], axis=0).astype(BF16)
    vw = jnp.concatenate([wvs_ref[...], wvn_ref[...], pad], axis=0).astype(BF16)
    s = _dot_nt(qa, kw)
    i = lax.broadcasted_iota(jnp.int32, (rows, wb + PAGE), 1)
    kp = jnp.where(i < wb, past - wb + i, past + i - wb)
    qp = past + (lax.broadcasted_iota(jnp.int32, (rows, wb + PAGE), 0) & (nq - 1))
    dlt = qp - kp
    s = jnp.where((dlt >= 0) & (dlt < WINDOW) & (i < wb + nq), s, NEG)
    p, l = _softmax_rows(s)
    owin_ref[...] = _dot(p.astype(BF16), vw) / l


def _sselwin(pt, qr, sel, skn, svn, wkn, wvn, wks, wvs, cache_k, cache_v, layer, nb, nq, past):
    n_pages = past // PAGE
    ktot = past + PAGE
    nselp = sel.shape[-1]
    e = (np.arange(ktot)[None, :] // SEL_BLK == np.arange(nselp)[:, None]).astype(np.float32)
    e = jnp.asarray(e, BF16)
    wb = wks.shape[1]
    any_spec = pl.BlockSpec(memory_space=pl.ANY)
    new = pl.BlockSpec((nq, LANES), lambda bi, pt_: (bi, 0))
    grid_spec = pltpu.PrefetchScalarGridSpec(
        num_scalar_prefetch=1, grid=(nb,),
        in_specs=[pl.BlockSpec((nq, 1024), lambda bi, pt_: (bi, 0)),
                  pl.BlockSpec((None, NSA_G * nq, nselp), lambda bi, pt_: (bi, 0, 0)),
                  new, new, new, new,
                  pl.BlockSpec((None, wb, LANES), lambda bi, pt_: (bi, 0, 0)),
                  pl.BlockSpec((None, wb, LANES), lambda bi, pt_: (bi, 0, 0)),
                  pl.BlockSpec(e.shape, lambda bi, pt_: (0, 0)), any_spec, any_spec],
        out_specs=[pl.BlockSpec((None, 8 * nq, LANES), lambda bi, pt_: (bi, 0, 0)),
                   pl.BlockSpec((None, 8 * nq, LANES), lambda bi, pt_: (bi, 0, 0))],
        scratch_shapes=[pltpu.VMEM((LANES, ktot), F32), pltpu.VMEM((LANES, ktot), F32), pltpu.SemaphoreType.DMA((2,))])
    return pl.pallas_call(
        functools.partial(_sselwin_body, layer=layer, n_pages=n_pages, past=past, nq=nq), name="sselwin",
        grid_spec=grid_spec,
        out_shape=[jax.ShapeDtypeStruct((nb, 8 * nq, LANES), F32)] * 2,
        compiler_params=_cparams(("arbitrary",)),
    )(pt, qr, sel, skn, svn, wkn, wvn, wks, wvs, e, cache_k, cache_v)


def _sdiff_body(pt_ref, dq_ref, dkn_ref, dvn_ref, lq1_ref, lk1_ref, lq2_ref, lk2_ref, sg_ref, dk_hbm, dv_hbm,
                o_ref, kbuf, vbuf, sem, *, layer, n_pages, past, nq, lam_init):
    b = pl.program_id(0)
    ktot = past + PAGE
    rows = 8 * nq
    pad = jnp.zeros((PAGE - nq, 256), F32)
    kbuf[:, pl.ds(past, PAGE)] = jnp.concatenate([dkn_ref[...], pad], axis=0).T
    vbuf[:, pl.ds(past, PAGE)] = jnp.concatenate([dvn_ref[...], pad], axis=0).T
    _gather_pages(pt_ref, b, n_pages, [(dk_hbm, kbuf, sem.at[0]), (dv_hbm, vbuf, sem.at[1])], layer)
    q = dq_ref[...]
    zero = jnp.zeros((nq, LANES), BF16)
    blocks = []
    for a in range(8):
        chunk = q[:, LANES * a:LANES * (a + 1)]
        blocks.append(jnp.concatenate([chunk, zero] if a < 4 else [zero, chunk], axis=1))
    qa = jnp.concatenate(blocks, axis=0)
    s = _dot(qa, kbuf[...].astype(BF16))
    kpos = lax.broadcasted_iota(jnp.int32, (rows, ktot), 1)
    qpos = past + (lax.broadcasted_iota(jnp.int32, (rows, ktot), 0) & (nq - 1))
    s = jnp.where(kpos <= qpos, s, NEG)
    p, l = _softmax_rows(s)
    o = _dot_nt(p.astype(BF16), vbuf[...].astype(BF16)) / l
    lam = (jnp.exp(jnp.sum(lq1_ref[...] * lk1_ref[...], keepdims=True))
           - jnp.exp(jnp.sum(lq2_ref[...] * lk2_ref[...], keepdims=True)) + lam_init)
    lane = lax.broadcasted_iota(jnp.int32, (nq, 256), 1)
    out = jnp.zeros((nq, 256), F32)
    for h in range(DIFF_H):
        d = o[2 * nq * h:2 * nq * h + nq] - lam * o[2 * nq * h + nq:2 * nq * (h + 1)]
        inh = (lane >> 6) == h
        ms = jnp.sum(jnp.where(inh, d * d, 0.0), axis=-1, keepdims=True) * (1.0 / DIFF_DV)
        out = out + jnp.where(inh, d * lax.rsqrt(ms + LN_EPS), 0.0)
    o_ref[...] = out * sg_ref[...] * (1.0 - lam_init)


def _sdiff(pt, dq, dkn, dvn, lams, sg_row, cache_k, cache_v, layer, nb, nq, past, lam_init):
    n_pages = past // PAGE
    ktot = past + PAGE
    any_spec = pl.BlockSpec(memory_space=pl.ANY)
    new = pl.BlockSpec((nq, 256), lambda bi, pt_: (bi, 0))
    small = [pl.BlockSpec(a.shape, lambda bi, pt_: (0, 0)) for a in (*lams, sg_row)]
    grid_spec = pltpu.PrefetchScalarGridSpec(
        num_scalar_prefetch=1, grid=(nb,),
        in_specs=[pl.BlockSpec((nq, 1024), lambda bi, pt_: (bi, 0)), new, new] + small + [any_spec, any_spec],
        out_specs=pl.BlockSpec((nq, 256), lambda bi, pt_: (bi, 0)),
        scratch_shapes=[pltpu.VMEM((256, ktot), F32), pltpu.VMEM((256, ktot), F32), pltpu.SemaphoreType.DMA((2,))])
    return pl.pallas_call(
        functools.partial(_sdiff_body, layer=layer, n_pages=n_pages, past=past, nq=nq, lam_init=lam_init), name="sdiff",
        grid_spec=grid_spec,
        out_shape=jax.ShapeDtypeStruct((nb * nq, 256), F32),
        compiler_params=_cparams(("arbitrary",)),
    )(pt, dq, dkn, dvn, *lams, sg_row, cache_k, cache_v)


def _prep_w_in(w):
    d = w.shape[0]
    pts = np.cumsum([256, 256, 512, 128, 128, 128, 128, 128, 128, 24, 256, 256, 256])[:-1].tolist()
    ca, cg, nq, ck, cv, sk, sv, wk, wv, gt, dq, dk, dv = jnp.split(w, pts, axis=1)
    q5 = (nq * (NSA_DH ** -0.5 * LOG2E)).reshape(d, NSA_G, NSA_HPG, 1, NSA_DH)
    qpad = (q5 * jnp.eye(NSA_G, dtype=F32)[None, :, None, :, None]).reshape(d, 8 * LANES)
    d6 = dq.reshape(d, 2, 2, 2, 1, 1, DIFF_DQK)
    sel = jnp.eye(2, dtype=F32)[:, None, :, None] * jnp.eye(2, dtype=F32)[None, :, None, :]
    dqpad = (d6 * sel[None, None, :, :, :, :, None]).reshape(d, 8 * LANES)
    gtp = jnp.pad(gt, ((0, 0), (0, LANES - gt.shape[1])))
    return jnp.concatenate([ca, cg, qpad, ck, cv, sk, sv, wk, wv, gtp, dqpad, dk, dv], axis=1).astype(BF16)


def _rope_tables(pos):
    out = []
    lane = np.arange(LANES)
    for half in (32, 16):
        inv = ROPE_THETA ** (-jnp.arange(half, dtype=F32) / half)
        ang = pos.astype(F32)[:, None] * inv[None, :]
        idx = lane % half
        sign = jnp.asarray(np.where(lane % (2 * half) < half, -1.0, 1.0), F32)
        out += [jnp.cos(ang)[:, idx], jnp.sin(ang)[:, idx] * sign[None, :]]
    return out


def _prep_cmp(pe, w1, b1, w2):
    def halves(x):
        res = []
        for part in (x[:16 * NSA_DH], x[16 * NSA_DH:]):
            p4 = part.reshape(CMP_STRIDE, 1, NSA_DH, 1, -1)
            eye = jnp.eye(NSA_G, dtype=F32)[None, :, None, :, None]
            res.append((p4 * eye).reshape(CMP_STRIDE * NSA_G * NSA_DH, NSA_G * part.shape[-1]))
        return res
    wa, wb = halves(w1)
    pea = jnp.tile(pe[:16, None, :], (1, NSA_G, 1)).reshape(1, -1)
    peb = jnp.tile(pe[16:, None, :], (1, NSA_G, 1)).reshape(1, -1)
    w2bd = (w2[None, :, None, :] * jnp.eye(NSA_G, dtype=F32)[:, None, :, None]).reshape(NSA_G * CMP_HID, NSA_G * NSA_DH)
    return pea, peb, wa.astype(BF16), wb.astype(BF16), jnp.tile(b1, NSA_G)[None, :], w2bd.astype(BF16)


def _gate_expand():
    e = np.zeros((LANES, 3 * 512), np.float32)
    for h in range(NSA_H):
        for br in range(3):
            e[3 * h + br, 512 * br + 64 * h:512 * br + 64 * (h + 1)] = 1.0
    return jnp.asarray(e, BF16)


def _rows_to_tokens(o, nb, nq):
    o6 = o.reshape(nb, NSA_G, NSA_HPG, nq, NSA_G, NSA_DH)
    pick = jnp.stack([o6[:, g, :, :, g, :] for g in range(NSA_G)], axis=1)
    return pick.transpose(0, 3, 1, 2, 4).reshape(nb * nq, NSA_H * NSA_DH)


def _prompt_layer(x, lw, tabs, nb, t, lam_init, alpha):
    n = nb * t
    (u, qu, qr, ck, cv, sk, sv, wk, wv, gt, dq, dk, dv, skb, wkb, dkb, svt, wvt, dvt) = _inproj(
        x, lw["w_in"], tabs, nb, t, 256, True)
    ext = jnp.pad(u.reshape(nb, t, 256), ((0, 0), (CONV_PAD, 0), (0, 0)))
    yc = _conv(ext, lw["dw_w"], lw["dw_b"], lw["cln_g"], lw["cln_b"], t, 256).reshape(n, 256)
    ocmp, sel = _pcmp(qu, ck, cv, lw["cmp"], nb, t)
    osel = _flash("sel", qr, skb, svt, nb, t, extra=(sel,))
    owin = _win(qr, wkb, wvt, nb, t)
    odiff = _flash("diff", dq, dkb, dvt, nb, t, extra=lw["lams"] + (lw["sg_col"],), lam_init=lam_init)
    x1 = _outproj(x, yc, ocmp, osel, owin, gt, odiff, lw["gate_e"], lw["w_out"], lw["ln1_g"], lw["ln1_b"], alpha, 256)
    x2 = _ffn(x1, lw["w1"], lw["w3"], lw["w2"], lw["ln2_g"], lw["ln2_b"], alpha, 512, 2)
    nk = min(WINDOW, t)
    news = (ck.reshape(nb, t, NSA_G, NSA_DH), cv.reshape(nb, t, NSA_G, NSA_DH),
            sk.reshape(nb, t, NSA_G, NSA_DH), sv.reshape(nb, t, NSA_G, NSA_DH),
            dk.reshape(nb, t, DIFF_H, 2 * DIFF_DQK), dv.reshape(nb, t, DIFF_H, DIFF_DV),
            wk.reshape(nb, t, NSA_G, NSA_DH)[:, t - nk:], wv.reshape(nb, t, NSA_G, NSA_DH)[:, t - nk:],
            u.reshape(nb, t, 256)[:, t - (CONV_W - 1):])
    return x2, news


def _sample_layer(x, lw, tabs, caches, states, pt, layer, nb, nq, past, lam_init, alpha):
    n = nb * nq
    (u, qu, qr, ck, cv, sk, sv, wk, wv, gt, dq, dk, dv) = _inproj(x, lw["w_in"], tabs, nb, nq, n, False)
    c_cmp_k, c_cmp_v, c_sel_k, c_sel_v, c_diff_k, c_diff_v = caches
    st_wk, st_wv, st_conv = states
    ext = jnp.concatenate([jnp.zeros((nb, CONV_PAD - (CONV_W - 1), 256), F32), st_conv, u.reshape(nb, nq, 256)], axis=1)
    yc = _conv(ext, lw["dw_w"], lw["dw_b"], lw["cln_g"], lw["cln_b"], nq, nq).reshape(n, 256)
    ocmp, sel = _scmp(pt, qu, c_cmp_k, c_cmp_v, lw["cmp"], layer, nb, nq, past)
    wb = st_wk.shape[1]
    osel, owin = _sselwin(pt, qr, sel, sk, sv, wk, wv, st_wk.reshape(nb, wb, LANES), st_wv.reshape(nb, wb, LANES),
                          c_sel_k, c_sel_v, layer, nb, nq, past)
    odiff = _sdiff(pt, dq, dk, dv, lw["lams"], lw["sg_row"], c_diff_k, c_diff_v, layer, nb, nq, past, lam_init)
    x1 = _outproj(x, yc, _rows_to_tokens(ocmp, nb, nq), _rows_to_tokens(osel, nb, nq), _rows_to_tokens(owin, nb, nq),
                  gt, odiff, lw["gate_e"], lw["w_out"], lw["ln1_g"], lw["ln1_b"], alpha, n)
    x2 = _ffn(x1, lw["w1"], lw["w3"], lw["w2"], lw["ln2_g"], lw["ln2_b"], alpha, n, 2)
    new_wk = jnp.concatenate([st_wk, wk.reshape(nb, nq, NSA_G, NSA_DH)], axis=1)[:, -wb:]
    new_wv = jnp.concatenate([st_wv, wv.reshape(nb, nq, NSA_G, NSA_DH)], axis=1)[:, -wb:]
    new_conv = jnp.concatenate([st_conv, u.reshape(nb, nq, 256)], axis=1)[:, -(CONV_W - 1):]
    news = (ck.reshape(nb, nq, NSA_G, NSA_DH), cv.reshape(nb, nq, NSA_G, NSA_DH),
            sk.reshape(nb, nq, NSA_G, NSA_DH), sv.reshape(nb, nq, NSA_G, NSA_DH),
            dk.reshape(nb, nq, DIFF_H, 2 * DIFF_DQK), dv.reshape(nb, nq, DIFF_H, DIFF_DV),
            new_wk, new_wv, new_conv)
    return x2, news


def kernel(x_prompt, x_sample, cache_nsa_cmp_k, cache_nsa_cmp_v, cache_nsa_sel_k, cache_nsa_sel_v, cache_diff_k, cache_diff_v, state_nsa_win_k, state_nsa_win_v, state_conv, page_table, w_in, conv_dw_w, conv_dw_b, conv_ln_g, conv_ln_b, cmp_pe_k, cmp_w1_k, cmp_b1_k, cmp_w2_k, cmp_pe_v, cmp_w1_v, cmp_b1_v, cmp_w2_v, diff_lq1, diff_lk1, diff_lq2, diff_lk2, diff_subln_g, w_out, ln1_g, ln1_b, ln2_g, ln2_b, ffn_w1, ffn_w3, ffn_w2):
    nb, t, d = x_prompt.shape
    sb, nq, _ = x_sample.shape
    depth = w_in.shape[0]
    n_pool = cache_nsa_cmp_k.shape[1]
    past = page_table.shape[1] * PAGE
    alpha = (2 * depth) ** 0.25
    tabs_p = _rope_tables(jnp.arange(t, dtype=jnp.int32))
    tabs_s = _rope_tables(jnp.tile(past + jnp.arange(nq, dtype=jnp.int32), sb))
    as_pages = lambda c: c.transpose(0, 1, 3, 4, 2).reshape(depth, n_pool, c.shape[3] * c.shape[4], PAGE)
    caches = tuple(as_pages(c) for c in (cache_nsa_cmp_k, cache_nsa_cmp_v, cache_nsa_sel_k, cache_nsa_sel_v,
                                         cache_diff_k, cache_diff_v))
    gate_e = _gate_expand()
    xp = x_prompt.reshape(nb * t, d)
    xs = x_sample.reshape(sb * nq, d)
    outs_p, outs_s = [], []
    for l in range(depth):
        ck = _prep_cmp(cmp_pe_k[l], cmp_w1_k[l], cmp_b1_k[l], cmp_w2_k[l])
        cv = _prep_cmp(cmp_pe_v[l], cmp_w1_v[l], cmp_b1_v[l], cmp_w2_v[l])
        names = ("pa", "pb", "wa", "wb", "b1", "w2")
        cmpw = {n_ + "k": a for n_, a in zip(names, ck)}
        cmpw.update({n_ + "v": a for n_, a in zip(names, cv)})
        lw = dict(
            w_in=_prep_w_in(w_in[l]), dw_w=conv_dw_w[l], dw_b=conv_dw_b[l][None], cln_g=conv_ln_g[l][None],
            cln_b=conv_ln_b[l][None], cmp=cmpw,
            lams=(diff_lq1[l][None], diff_lk1[l][None], diff_lq2[l][None], diff_lk2[l][None]),
            sg_col=diff_subln_g[l][:, None], sg_row=jnp.tile(diff_subln_g[l], DIFF_H)[None],
            gate_e=gate_e, w_out=w_out[l].astype(BF16), ln1_g=ln1_g[l][None], ln1_b=ln1_b[l][None],
            ln2_g=ln2_g[l][None], ln2_b=ln2_b[l][None],
            w1=ffn_w1[l].astype(BF16), w3=ffn_w3[l].astype(BF16), w2=ffn_w2[l].astype(BF16))
        lam_init = 0.8 - 0.6 * math.exp(-0.3 * l)
        xp, new_p = _prompt_layer(xp, lw, tabs_p, nb, t, lam_init, alpha)
        xs, new_s = _sample_layer(xs, lw, tabs_s, caches, (state_nsa_win_k[l], state_nsa_win_v[l], state_conv[l]),
                                  page_table, l, sb, nq, past, lam_init, alpha)
        outs_p.append(new_p)
        outs_s.append(new_s)
    stk_p = [jnp.stack([o[i] for o in outs_p]) for i in range(9)]
    stk_s = [jnp.stack([o[i] for o in outs_s]) for i in range(9)]
    return (xp.reshape(nb, t, d), xs.reshape(sb, nq, d), *stk_p, *stk_s)
```

```python
import functools
import math

import jax
import jax.numpy as jnp
import numpy as np
from jax import lax
from jax.experimental import pallas as pl
from jax.experimental.pallas import tpu as pltpu

F32 = jnp.float32
BF16 = jnp.bfloat16

CONV_W = 31
NSA_H = 8
NSA_G = 2
NSA_HPG = NSA_H // NSA_G
NSA_DH = 64
CMP_STRIDE = 16
CMP_LEN = 32
CMP_HID = 128
SEL_BLK = 64
SEL_TOPN = 16
WINDOW = 512
FORCE_SCORE = 1e9
DIFF_H = 4
DIFF_DV = 64
DIFF_DQK = 32
ROPE_THETA = 10000.0
LN_EPS = 1e-5
PAGE = 128

LANES = 128
TQ = 128
TK = 256
LOG2E = 1.4426950408889634
NEG = -1e30
VMEM_LIMIT = 52 * 1024 * 1024

C_CA, C_CG, C_Q, C_KV, C_GT, C_DQ, C_DK, C_DV, C_END = 0, 256, 512, 1024, 1792, 1920, 2176, 2432, 2688


def _cparams(sem):
    return pltpu.CompilerParams(dimension_semantics=sem, vmem_limit_bytes=VMEM_LIMIT)


def _sigmoid(x):
    return 1.0 / (1.0 + jnp.exp(-x))


def _ln_rows(x, g, b):
    mu = jnp.mean(x, axis=-1, keepdims=True)
    xc = x - mu
    var = jnp.mean(xc * xc, axis=-1, keepdims=True)
    return xc * lax.rsqrt(var + LN_EPS) * g + b


def _dot(a, b):
    return jnp.dot(a, b, preferred_element_type=F32)


def _dot_nt(a, b):
    return lax.dot_general(a, b, (((1,), (1,)), ((), ())), preferred_element_type=F32)


def _split3_dot(x, m):
    hi = x.astype(BF16)
    r1 = x - hi.astype(F32)
    mid = r1.astype(BF16)
    lo = (r1 - mid.astype(F32)).astype(BF16)
    return _dot(hi, m) + _dot(mid, m) + _dot(lo, m)


def _rope(x, cos, sin_signed, half):
    lane = lax.broadcasted_iota(jnp.int32, x.shape, 1)
    first = (lane & (2 * half - 1)) < half
    rot = jnp.where(first, pltpu.roll(x, LANES - half, 1), pltpu.roll(x, half, 1))
    return x * cos + rot * sin_signed


def _inproj_body(x_ref, w_ref, c64_ref, s64_ref, c32_ref, s32_ref,
                 u_ref, qu_ref, qr_ref, gt_ref, dq_ref, ck_ref, cv_ref, *refs, dq_scale, transposed):
    xb = x_ref[...].astype(BF16)

    def mm(lo, hi):
        return _dot(xb, w_ref[:, lo:hi])

    c64, s64, c32, s32 = c64_ref[...], s64_ref[...], c32_ref[...], s32_ref[...]
    z = mm(C_CA, C_Q)
    u_ref[...] = z[:, :256] * _sigmoid(z[:, 256:])
    lane = lax.broadcasted_iota(jnp.int32, (x_ref.shape[0], LANES), 1)
    for j in range(4):
        zq = mm(C_Q + LANES * j, C_Q + LANES * (j + 1))
        g = j // 2
        keep = (lane < 64) if g == 0 else (lane >= 64)
        for src, dst_ref in ((zq, qu_ref), (_rope(zq, c64, s64, 32), qr_ref)):
            for e in range(2):
                v = src if e == g else pltpu.roll(src, 64, 1)
                dst_ref[:, LANES * (2 * j + e):LANES * (2 * j + e + 1)] = jnp.where(keep, v, 0.0).astype(BF16)
    z = mm(C_KV, C_GT)
    ck = z[:, 0:128]
    cv = z[:, 128:256]
    ck_ref[...] = ck
    cv_ref[...] = cv
    sk = _rope(z[:, 256:384], c64, s64, 32)
    sv = z[:, 384:512]
    wk = _rope(z[:, 512:640], c64, s64, 32)
    wv = z[:, 640:768]
    gt_ref[...] = _sigmoid(mm(C_GT, C_DQ))
    for c in range(2):
        zq = _rope(mm(C_DQ + LANES * c, C_DQ + LANES * (c + 1)), c32, s32, 16) * dq_scale
        for a in range(4):
            dq_ref[:, LANES * (4 * c + a):LANES * (4 * c + a + 1)] = jnp.where((lane >> 5) == a, zq, 0.0).astype(BF16)
    z = mm(C_DK, C_DV)
    dk = [_rope(z[:, :128], c32, s32, 16), _rope(z[:, 128:], c32, s32, 16)]
    z = mm(C_DV, C_END)
    dv = [z[:, :128], z[:, 128:]]
    if not transposed:
        sk_ref, sv_ref, wk_ref, wv_ref, dk_ref, dv_ref = refs
        sk_ref[...] = sk
        sv_ref[...] = sv
        wk_ref[...] = wk
        wv_ref[...] = wv
        for c in range(2):
            dk_ref[:, LANES * c:LANES * (c + 1)] = dk[c]
            dv_ref[:, LANES * c:LANES * (c + 1)] = dv[c]
        return
    (skb_ref, wkb_ref, dkb_ref, svb_ref, wvb_ref, dvb_ref,
     ckt_ref, cvt_ref, skt_ref, svt_ref, wkt_ref, wvt_ref, dkt_ref, dvt_ref) = refs
    skb_ref[...] = sk.astype(BF16)
    wkb_ref[...] = wk.astype(BF16)
    ckt_ref[...] = ck.T
    cvt_ref[...] = cv.T
    skt_ref[...] = sk.T
    wkt_ref[...] = wk.T
    svt = sv.T
    svt_ref[...] = svt
    svb_ref[...] = svt.astype(BF16)
    wvt = wv.T
    wvt_ref[...] = wvt
    wvb_ref[...] = wvt.astype(BF16)
    for c in range(2):
        rows = slice(LANES * c, LANES * (c + 1))
        dkb_ref[:, rows] = dk[c].astype(BF16)
        dkt_ref[rows, :] = dk[c].T
        dvt = dv[c].T
        dvt_ref[rows, :] = dvt
        dvb_ref[rows, :] = dvt.astype(BF16)


def _inproj(x, w, tabs, nb, t, tm, transposed):
    n, d = x.shape
    nt = t // tm if transposed else 1
    grid = (n // tm,)
    row = lambda c: pl.BlockSpec((tm, c), lambda i: (i, 0))
    tab = pl.BlockSpec((tm, LANES), (lambda i: (i % nt, 0)) if transposed else (lambda i: (i, 0)))
    in_specs = [row(d), pl.BlockSpec((d, C_END), lambda i: (0, 0)), tab, tab, tab, tab]
    shapes = [(256, F32), (1024, BF16), (1024, BF16), (128, F32), (1024, BF16), (128, F32), (128, F32)]
    if transposed:
        shapes += [(128, BF16), (128, BF16), (256, BF16)]
    else:
        shapes += [(128, F32)] * 4 + [(256, F32)] * 2
    out_shape = [jax.ShapeDtypeStruct((n, c), dt) for c, dt in shapes]
    out_specs = [row(c) for c, _ in shapes]
    if transposed:
        for c, dt in [(128, BF16), (128, BF16), (256, BF16)] + [(128, F32)] * 6 + [(256, F32)] * 2:
            out_shape.append(jax.ShapeDtypeStruct((nb, c, t), dt))
            out_specs.append(pl.BlockSpec((None, c, tm), lambda i: (i // nt, 0, i % nt)))
    return pl.pallas_call(
        functools.partial(_inproj_body, dq_scale=DIFF_DQK ** -0.5 * LOG2E, transposed=transposed), name="inproj",
        grid=grid, in_specs=in_specs, out_specs=out_specs, out_shape=out_shape,
        compiler_params=_cparams(("parallel",)),
    )(x, w, *tabs)


CONV_PAD = 32


def _conv_body(ext_ref, w_ref, b_ref, g_ref, beta_ref, y_ref, acc_ref, *, tt, rs):
    t0 = pl.multiple_of(pl.program_id(1) * tt, 8)
    off = CONV_PAD - (CONV_W - 1)
    for c in range(2):
        cs = slice(LANES * c, LANES * (c + 1))
        for r in range(tt // rs):
            win = ext_ref[pl.ds(t0 + rs * r, rs + CONV_PAD), cs]
            acc = jnp.zeros((rs, LANES), F32)
            for k in range(CONV_W):
                acc = acc + win[off + k:off + k + rs, :] * w_ref[k:k + 1, cs]
            acc_ref[rs * r:rs * (r + 1), cs] = acc
    y = _ln_rows(acc_ref[...] + b_ref[...], g_ref[...], beta_ref[...])
    y_ref[...] = y * _sigmoid(y)


def _conv(ext, w, b, g, beta, t, tt):
    nb, le, c = ext.shape
    rs = min(tt, 64)
    vec = pl.BlockSpec((1, c), lambda bi, ti: (0, 0))
    return pl.pallas_call(
        functools.partial(_conv_body, tt=tt, rs=rs), name="conv",
        grid=(nb, t // tt),
        in_specs=[pl.BlockSpec((None, le, c), lambda bi, ti: (bi, 0, 0)),
                  pl.BlockSpec((CONV_W, c), lambda bi, ti: (0, 0)), vec, vec, vec],
        out_specs=pl.BlockSpec((None, tt, c), lambda bi, ti: (bi, ti, 0)),
        out_shape=jax.ShapeDtypeStruct((nb, t, c), F32),
        scratch_shapes=[pltpu.VMEM((tt, c), F32)],
        compiler_params=_cparams(("parallel", "parallel")),
    )(ext, w, b, g, beta)


def _compress(src_ref, n16, pea, peb, wa_ref, wb_ref, b1, w2_ref):
    x = jnp.concatenate([src_ref[pl.ds(p, n16, stride=CMP_STRIDE), :] for p in range(CMP_STRIDE)], axis=1)
    a = _dot((x + pea).astype(BF16), wa_ref[...])
    bm = _dot((x + peb).astype(BF16), wb_ref[...])
    h = a + pltpu.roll(bm, n16 - 1, 0) + b1
    gl = 0.5 * h * (1.0 + jnp.tanh(0.7978845608028654 * (h + 0.044715 * (h * h * h))))
    return _dot(gl.astype(BF16), w2_ref[...])


def _top_rows(score, ridx, k):
    cnt = jnp.zeros(score.shape, F32)
    for jp in range(score.shape[0]):
        row = score[jp:jp + 1, :]
        beats = (row > score) | ((row == score) & (ridx > jp))
        cnt = cnt + beats.astype(F32)
    return cnt < k


def _pcmp_body(qu_ref, ck_ref, cv_ref, pak_ref, pbk_ref, wak_ref, wbk_ref, b1k_ref, w2k_ref,
               pav_ref, pbv_ref, wav_ref, wbv_ref, b1v_ref, w2v_ref, mt_ref,
               o_ref, sel_ref, kcc_ref, vcct_ref, *, n16, nsel):
    ti = pl.program_id(1)

    @pl.when(ti == 0)
    def _():
        kcc = _compress(ck_ref, n16, pak_ref[...], pbk_ref[...], wak_ref, wbk_ref, b1k_ref[...], w2k_ref)
        kcc_ref[...] = kcc.astype(BF16)
        vcc = _compress(cv_ref, n16, pav_ref[...], pbv_ref[...], wav_ref, wbv_ref, b1v_ref[...], w2v_ref)
        vcct_ref[...] = vcc.T.astype(BF16)

    q = qu_ref[...]
    kcc = kcc_ref[...]
    vcct = vcct_ref[...]
    nq = 4 * TQ
    cidx = lax.broadcasted_iota(jnp.int32, (n16, nq), 0)
    qpos = ti * TQ + (lax.broadcasted_iota(jnp.int32, (n16, nq), 1) & (TQ - 1))
    vis = (CMP_STRIDE * cidx + CMP_LEN - 1 <= qpos) & (cidx < n16 - 1)
    jidx = lax.broadcasted_iota(jnp.int32, (nsel, TQ), 0)
    qp1 = ti * TQ + lax.broadcasted_iota(jnp.int32, (nsel, TQ), 1)
    cur = qp1 >> 6
    forced = (jidx == 0) | (jidx == cur) | (jidx == cur - 1)
    for g in range(NSA_G):
        qs = jnp.concatenate([q[:, LANES * (4 * g + a):LANES * (4 * g + a + 1)] for a in range(4)], axis=0)
        st = _dot_nt(kcc, qs)
        st = jnp.where(vis, st, NEG)
        m = jnp.max(st, axis=0, keepdims=True)
        p = jnp.where(vis, jnp.exp2(st - m), 0.0)
        l = jnp.sum(p, axis=0, keepdims=True)
        p = p / jnp.maximum(l, 1e-30)
        ot = _dot(vcct, p.astype(BF16))
        for a2 in range(2):
            blk = jnp.concatenate([ot[64 * g:64 * g + 64, TQ * (2 * a2 + e):TQ * (2 * a2 + e + 1)] for e in range(2)], axis=0)
            o_ref[:, 256 * g + LANES * a2:256 * g + LANES * (a2 + 1)] = blk.T
        imp = p[:, 0:TQ] + p[:, TQ:2 * TQ] + p[:, 2 * TQ:3 * TQ] + p[:, 3 * TQ:4 * TQ]
        hi = imp.astype(BF16)
        r1 = imp - hi.astype(F32)
        mid = r1.astype(BF16)
        lo = (r1 - mid.astype(F32)).astype(BF16)
        mt = mt_ref[...]
        isel = _dot(mt, hi) + _dot(mt, mid) + _dot(mt, lo)
        score = jnp.where(forced, FORCE_SCORE, isel)
        score = jnp.where(jidx <= cur, score, -jnp.inf)
        sel = _top_rows(score, jidx, min(SEL_TOPN, nsel)) & (jidx <= cur)
        sel_ref[g] = jnp.where(sel, 0.0, NEG)


def _pcmp(qu, ck, cv, cw, nb, t):
    n16 = t // CMP_STRIDE
    nsel = t // SEL_BLK
    nt = t // TQ
    mt = np.zeros((nsel, n16), np.float32)
    for j in range(nsel):
        for c in range(4 * j - 1, 4 * j + 4):
            if 0 <= c < n16 - 1:
                mt[j, c] = 1.0
    mt = jnp.asarray(mt, BF16)
    const = lambda a: pl.BlockSpec(a.shape, lambda bi, ti: (0,) * a.ndim)
    wlist = [cw[k] for k in ("pak", "pbk", "wak", "wbk", "b1k", "w2k", "pav", "pbv", "wav", "wbv", "b1v", "w2v")] + [mt]
    return pl.pallas_call(
        functools.partial(_pcmp_body, n16=n16, nsel=nsel), name="pcmp",
        grid=(nb, nt),
        in_specs=[pl.BlockSpec((TQ, 1024), lambda bi, ti: (bi * nt + ti, 0)),
                  pl.BlockSpec((None, t, LANES), lambda bi, ti: (bi, 0, 0)),
                  pl.BlockSpec((None, t, LANES), lambda bi, ti: (bi, 0, 0))] + [const(a) for a in wlist],
        out_specs=[pl.BlockSpec((TQ, 512), lambda bi, ti: (bi * nt + ti, 0)),
                   pl.BlockSpec((None, None, NSA_G, nsel, TQ), lambda bi, ti: (bi, ti, 0, 0, 0))],
        out_shape=[jax.ShapeDtypeStruct((nb * t, 512), F32),
                   jax.ShapeDtypeStruct((nb, nt, NSA_G, nsel, TQ), F32)],
        scratch_shapes=[pltpu.VMEM((n16, LANES), BF16), pltpu.VMEM((LANES, n16), BF16)],
        compiler_params=_cparams(("parallel", "arbitrary")),
    )(qu, ck.reshape(nb, t, LANES), cv.reshape(nb, t, LANES), *wlist)


def _stack_queries(q, s):
    return jnp.concatenate([q[:, LANES * (4 * s + a):LANES * (4 * s + a + 1)] for a in range(4)], axis=0)


def _flash_out(z, s, o_ref):
    for a2 in range(2):
        blk = jnp.concatenate([z[64 * s:64 * s + 64, TQ * (2 * a2 + e):TQ * (2 * a2 + e + 1)] for e in range(2)], axis=0)
        o_ref[:, 256 * s + LANES * a2:256 * s + LANES * (a2 + 1)] = blk.T


def _flash_body(*refs, mode, lam_init):
    if mode == "sel":
        q_ref, k_ref, vt_ref, selb_ref, o_ref, acc_ref, m_ref, l_ref, al_ref, s_ref, p_ref = refs
    else:
        (q_ref, k_ref, vt_ref, lq1_ref, lk1_ref, lq2_ref, lk2_ref, sg_ref, o_ref,
         acc_ref, m_ref, l_ref, al_ref, s_ref, p_ref) = refs
    ti = pl.program_id(1)
    nq = 4 * TQ
    q = q_ref[...]
    qs = [_stack_queries(q, s) for s in range(2)]
    koffs = [LANES * s if mode == "diff" else 0 for s in range(2)]
    m_ref[...] = jnp.full(m_ref.shape, NEG, F32)
    l_ref[...] = jnp.zeros(l_ref.shape, F32)
    acc_ref[...] = jnp.zeros(acc_ref.shape, F32)
    al_ref[...] = jnp.ones(al_ref.shape, F32)
    p_ref[...] = jnp.zeros(p_ref.shape, BF16)

    def scores(j, slot):
        k0 = pl.multiple_of(j * TK, TK)
        for s in range(2):
            s_ref[slot, s] = _dot_nt(k_ref[pl.ds(k0, TK), koffs[s]:koffs[s] + LANES], qs[s])

    def accumulate(j):
        k0 = pl.multiple_of(j * TK, TK)
        for s in range(2):
            acc_ref[s] = al_ref[s] * acc_ref[s] + _dot(vt_ref[koffs[s]:koffs[s] + LANES, pl.ds(k0, TK)], p_ref[s])

    def softmax(j, slot, causal):
        for s in range(2):
            st = s_ref[slot, s]
            if mode == "sel":
                rows = [jnp.broadcast_to(selb_ref[s, pl.ds((TK // SEL_BLK) * j + r, 1), :], (SEL_BLK, TQ))
                        for r in range(TK // SEL_BLK)]
                bias = jnp.concatenate(rows, axis=0)
                st = st + jnp.concatenate([bias, bias, bias, bias], axis=1)
            if causal:
                krow = lax.broadcasted_iota(jnp.int32, (TK, nq), 0)
                qcol = lax.broadcasted_iota(jnp.int32, (TK, nq), 1) & (TQ - 1)
                st = jnp.where(ti * TQ + qcol >= j * TK + krow, st, NEG)
            m_old = m_ref[s]
            m_new = jnp.maximum(m_old, jnp.max(st, axis=0, keepdims=True))
            alpha = jnp.exp2(m_old - m_new)
            p = jnp.exp2(st - m_new)
            l_ref[s] = alpha * l_ref[s] + jnp.sum(p, axis=0, keepdims=True)
            p_ref[s] = p.astype(BF16)
            al_ref[s] = alpha
            m_ref[s] = m_new

    n_full = (ti * TQ) // TK
    scores(0, 0)

    def body(j, c):
        slot = j & 1
        accumulate(jnp.maximum(j - 1, 0))
        softmax(j, slot, False)
        scores(j + 1, 1 - slot)
        return c

    lax.fori_loop(0, n_full, body, 0)
    accumulate(jnp.maximum(n_full - 1, 0))
    softmax(n_full, n_full & 1, True)
    accumulate(n_full)

    if mode == "diff":
        lam = (jnp.exp(jnp.sum(lq1_ref[...] * lk1_ref[...], keepdims=True))
               - jnp.exp(jnp.sum(lq2_ref[...] * lk2_ref[...], keepdims=True)) + lam_init)
    for s in range(2):
        z = acc_ref[s] / l_ref[s]
        if mode == "diff":
            halves = []
            for hh in range(2):
                zr = z[64 * hh:64 * hh + 64, :]
                d = zr[:, TQ * (2 * hh):TQ * (2 * hh + 1)] - lam * zr[:, TQ * (2 * hh + 1):TQ * (2 * hh + 2)]
                ms = jnp.mean(d * d, axis=0, keepdims=True)
                halves.append(d * lax.rsqrt(ms + LN_EPS) * sg_ref[...] * (1.0 - lam_init))
            o_ref[:, LANES * s:LANES * (s + 1)] = jnp.concatenate(halves, axis=0).T
        else:
            _flash_out(z, s, o_ref)


def _flash(mode, q, k, vt, nb, t, extra=(), lam_init=0.0):
    nt = t // TQ
    kw = k.shape[-1]
    ow = 256 if mode == "diff" else 512
    in_specs = [pl.BlockSpec((TQ, 1024), lambda bi, ti: (bi * nt + ti, 0)),
                pl.BlockSpec((None, t, kw), lambda bi, ti: (bi, 0, 0)),
                pl.BlockSpec((None, kw, t), lambda bi, ti: (bi, 0, 0))]
    if mode == "sel":
        nsel = t // SEL_BLK
        in_specs.append(pl.BlockSpec((None, None, NSA_G, nsel, TQ), lambda bi, ti: (bi, ti, 0, 0, 0)))
    else:
        in_specs += [pl.BlockSpec(a.shape, lambda bi, ti: (0, 0)) for a in extra]
    return pl.pallas_call(
        functools.partial(_flash_body, mode=mode, lam_init=lam_init), name="flash_" + mode,
        grid=(nb, nt), in_specs=in_specs,
        out_specs=pl.BlockSpec((TQ, ow), lambda bi, ti: (bi * nt + ti, 0)),
        out_shape=jax.ShapeDtypeStruct((nb * t, ow), F32),
        scratch_shapes=[pltpu.VMEM((2, LANES, 4 * TQ), F32)] + [pltpu.VMEM((2, 1, 4 * TQ), F32)] * 3
        + [pltpu.VMEM((2, 2, TK, 4 * TQ), F32), pltpu.VMEM((2, TK, 4 * TQ), BF16)],
        compiler_params=_cparams(("parallel", "parallel")),
    )(q, k.reshape(nb, t, kw), vt, *extra)


def _win_body(q_ref, k_ref, vt_ref, o_ref):
    ti = pl.program_id(1)
    nq = 4 * TQ
    span = WINDOW + TQ
    k0 = pl.multiple_of(jnp.maximum(ti * TQ - WINDOW, 0), TQ)
    q = q_ref[...]
    kt = k_ref[pl.ds(k0, span), :]
    vt = vt_ref[:, pl.ds(k0, span)]
    krow = lax.broadcasted_iota(jnp.int32, (span, nq), 0)
    qcol = lax.broadcasted_iota(jnp.int32, (span, nq), 1) & (TQ - 1)
    dlt = ti * TQ + qcol - (k0 + krow)
    vis = (dlt >= 0) & (dlt < WINDOW)
    for s in range(2):
        st = jnp.where(vis, _dot_nt(kt, _stack_queries(q, s)), NEG)
        m = jnp.max(st, axis=0, keepdims=True)
        p = jnp.exp2(st - m)
        l = jnp.sum(p, axis=0, keepdims=True)
        _flash_out(_dot(vt, p.astype(BF16)) / l, s, o_ref)


def _win(q, k, vt, nb, t):
    nt = t // TQ
    assert t >= WINDOW + TQ
    return pl.pallas_call(
        _win_body, name="flash_win", grid=(nb, nt),
        in_specs=[pl.BlockSpec((TQ, 1024), lambda bi, ti: (bi * nt + ti, 0)),
                  pl.BlockSpec((None, t, LANES), lambda bi, ti: (bi, 0, 0)),
                  pl.BlockSpec((None, LANES, t), lambda bi, ti: (bi, 0, 0))],
        out_specs=pl.BlockSpec((TQ, 512), lambda bi, ti: (bi * nt + ti, 0)),
        out_shape=jax.ShapeDtypeStruct((nb * t, 512), F32),
        compiler_params=_cparams(("parallel", "parallel")),
    )(q, k.reshape(nb, t, LANES), vt)


def _outproj_body(x_ref, yc_ref, oc_ref, os_ref, ow_ref, gt_ref, od_ref, e_ref, w_ref, g_ref, b_ref, o_ref, *, alpha):
    gt = gt_ref[...]
    hi = gt.astype(BF16)
    lo = (gt - hi.astype(F32)).astype(BF16)
    e = e_ref[...]
    gx = _dot(hi, e) + _dot(lo, e)
    onsa = gx[:, 0:512] * oc_ref[...] + gx[:, 512:1024] * os_ref[...] + gx[:, 1024:1536] * ow_ref[...]
    mix = (_dot(yc_ref[...].astype(BF16), w_ref[0:256, :]) + _dot(onsa.astype(BF16), w_ref[256:768, :])
           + _dot(od_ref[...].astype(BF16), w_ref[768:1024, :]))
    o_ref[...] = _ln_rows(alpha * x_ref[...] + mix, g_ref[...], b_ref[...])


def _outproj(x, yc, oc, osel, ow, gt, od, e, w, g, b, alpha, tm):
    n, d = x.shape
    row = lambda c: pl.BlockSpec((tm, c), lambda i: (i, 0))
    const = lambda a: pl.BlockSpec(a.shape, lambda i: (0, 0))
    return pl.pallas_call(
        functools.partial(_outproj_body, alpha=alpha), name="outproj",
        grid=(n // tm,),
        in_specs=[row(d), row(256), row(512), row(512), row(512), row(128), row(256), const(e), const(w), const(g), const(b)],
        out_specs=row(d), out_shape=jax.ShapeDtypeStruct((n, d), F32),
        compiler_params=_cparams(("parallel",)),
    )(x, yc, oc, osel, ow, gt, od, e, w, g, b)


def _ffn_body(x_ref, w1_ref, w3_ref, w2_ref, g_ref, b_ref, o_ref, acc_ref, *, alpha):
    f = pl.program_id(1)
    xb = x_ref[...].astype(BF16)
    h1 = _dot(xb, w1_ref[...])
    h = h1 * _sigmoid(h1) * _dot(xb, w3_ref[...])
    part = _dot(h.astype(BF16), w2_ref[...])

    @pl.when(f == 0)
    def _():
        acc_ref[...] = part

    @pl.when(f != 0)
    def _():
        acc_ref[...] += part

    @pl.when(f == pl.num_programs(1) - 1)
    def _():
        o_ref[...] = _ln_rows(alpha * x_ref[...] + acc_ref[...], g_ref[...], b_ref[...])


def _ffn(x, w1, w3, w2, g, b, alpha, tm, nf):
    n, d = x.shape
    dff = w1.shape[1]
    tf = dff // nf
    return pl.pallas_call(
        functools.partial(_ffn_body, alpha=alpha), name="ffn",
        grid=(n // tm, nf),
        in_specs=[pl.BlockSpec((tm, d), lambda i, f: (i, 0)),
                  pl.BlockSpec((d, tf), lambda i, f: (0, f)),
                  pl.BlockSpec((d, tf), lambda i, f: (0, f)),
                  pl.BlockSpec((tf, d), lambda i, f: (f, 0)),
                  pl.BlockSpec((1, d), lambda i, f: (0, 0)),
                  pl.BlockSpec((1, d), lambda i, f: (0, 0))],
        out_specs=pl.BlockSpec((tm, d), lambda i, f: (i, 0)),
        out_shape=jax.ShapeDtypeStruct((n, d), F32),
        scratch_shapes=[pltpu.VMEM((tm, d), F32)],
        compiler_params=_cparams(("parallel", "arbitrary")),
    )(x, w1, w3, w2, g, b)


def _gather_pages(pt_ref, n_pages, srcs, sem, layer):
    b = pl.program_id(0)
    slot = b & 1

    def copies(bb, sl):
        return [pltpu.make_async_copy(hbm.at[layer, pt_ref[bb, j]], buf.at[sl, :, pl.ds(PAGE * j, PAGE)], sem.at[a, sl])
                for a, (hbm, buf) in enumerate(srcs) for j in range(n_pages)]

    @pl.when(b == 0)
    def _():
        for c in copies(0, 0):
            c.start()

    @pl.when(b + 1 < pl.num_programs(0))
    def _():
        for c in copies(b + 1, 1 - slot):
            c.start()

    for c in copies(b, slot):
        c.wait()
    return slot


def _key_chunks(n):
    per = max(d for d in range(1, 17) if n % d == 0)
    return [(PAGE * per * c, PAGE * per) for c in range(n // per)]


def _stack_heads(q):
    return jnp.concatenate([q[:, LANES * h:LANES * (h + 1)] for h in range(8)], axis=0)


def _scmp_body(pt_ref, qu_ref, ck_hbm, cv_hbm, pak_ref, pbk_ref, wak_ref, wbk_ref, b1k_ref, w2k_ref,
               pav_ref, pbv_ref, wav_ref, wbv_ref, b1v_ref, w2v_ref, m_ref,
               o_ref, sel_ref, kbt, vbt, kbuf, vbuf, sem, *, layer, n_pages, past, nq):
    slot = _gather_pages(pt_ref, n_pages, [(ck_hbm, kbt), (cv_hbm, vbt)], sem, layer)
    for j in range(n_pages):
        kbuf[PAGE * j:PAGE * (j + 1), :] = kbt[slot, :, PAGE * j:PAGE * (j + 1)].T
        vbuf[PAGE * j:PAGE * (j + 1), :] = vbt[slot, :, PAGE * j:PAGE * (j + 1)].T
    n16 = past // CMP_STRIDE
    kcc = _compress(kbuf, n16, pak_ref[...], pbk_ref[...], wak_ref, wbk_ref, b1k_ref[...], w2k_ref).astype(BF16)
    vcc = _compress(vbuf, n16, pav_ref[...], pbv_ref[...], wav_ref, wbv_ref, b1v_ref[...], w2v_ref).astype(BF16)
    qa = _stack_heads(qu_ref[...])
    rows = 8 * nq
    s = _dot_nt(qa, kcc)
    cidx = lax.broadcasted_iota(jnp.int32, (rows, n16), 1)
    qpos = past + (lax.broadcasted_iota(jnp.int32, (rows, n16), 0) & (nq - 1))
    vis = (CMP_STRIDE * cidx + CMP_LEN - 1 <= qpos) & (cidx < n16 - 1)
    s = jnp.where(vis, s, NEG)
    m = jnp.max(s, axis=-1, keepdims=True)
    p = jnp.where(vis, jnp.exp2(s - m), 0.0)
    l = jnp.sum(p, axis=-1, keepdims=True)
    p = p / jnp.maximum(l, 1e-30)
    o_ref[...] = _dot(p.astype(BF16), vcc)
    imp = jnp.concatenate(
        [p[nq * 4 * g:nq * (4 * g + 1)] + p[nq * (4 * g + 1):nq * (4 * g + 2)]
         + p[nq * (4 * g + 2):nq * (4 * g + 3)] + p[nq * (4 * g + 3):nq * (4 * g + 4)] for g in range(NSA_G)], axis=0)
    isel = _split3_dot(imp, m_ref[...])
    nselp = isel.shape[1]
    jidx = lax.broadcasted_iota(jnp.int32, (NSA_G * nq, nselp), 1)
    qp = past + (lax.broadcasted_iota(jnp.int32, (NSA_G * nq, nselp), 0) & (nq - 1))
    cur = qp >> 6
    forced = (jidx == 0) | (jidx == cur) | (jidx == cur - 1)
    score = jnp.where(forced, FORCE_SCORE, isel)
    score = jnp.where(jidx <= cur, score, -jnp.inf)
    n_sel = -(-(past + nq) // SEL_BLK)
    sel = jnp.zeros(score.shape, F32)
    for _ in range(min(SEL_TOPN, n_sel)):
        mx = jnp.max(score, axis=-1, keepdims=True)
        first = jnp.min(jnp.where(score == mx, jidx, nselp), axis=-1, keepdims=True)
        hit = jidx == first
        sel = jnp.where(hit, 1.0, sel)
        score = jnp.where(hit, -jnp.inf, score)
    sel_ref[...] = jnp.where(jidx <= cur, sel, 0.0)


def _scmp(pt, qu, cache_k, cache_v, cw, layer, nb, nq, past):
    n_pages = past // PAGE
    n16 = past // CMP_STRIDE
    n_sel = -(-(past + nq) // SEL_BLK)
    nselp = -(-n_sel // LANES) * LANES
    m = np.zeros((n16, nselp), np.float32)
    for j in range(n_sel):
        for c in range(4 * j - 1, 4 * j + 4):
            if 0 <= c < n16 - 1:
                m[c, j] = 1.0
    m = jnp.asarray(m, BF16)
    wlist = [cw[k] for k in ("pak", "pbk", "wak", "wbk", "b1k", "w2k", "pav", "pbv", "wav", "wbv", "b1v", "w2v")] + [m]
    const = lambda a: pl.BlockSpec(a.shape, lambda bi, pt_: (0,) * a.ndim)
    any_spec = pl.BlockSpec(memory_space=pl.ANY)
    grid_spec = pltpu.PrefetchScalarGridSpec(
        num_scalar_prefetch=1, grid=(nb,),
        in_specs=[pl.BlockSpec((nq, 1024), lambda bi, pt_: (bi, 0)), any_spec, any_spec] + [const(a) for a in wlist],
        out_specs=[pl.BlockSpec((None, 8 * nq, LANES), lambda bi, pt_: (bi, 0, 0)),
                   pl.BlockSpec((None, NSA_G * nq, nselp), lambda bi, pt_: (bi, 0, 0))],
        scratch_shapes=[pltpu.VMEM((2, LANES, past), F32), pltpu.VMEM((2, LANES, past), F32),
                        pltpu.VMEM((past, LANES), F32), pltpu.VMEM((past, LANES), F32), pltpu.SemaphoreType.DMA((2, 2))])
    return pl.pallas_call(
        functools.partial(_scmp_body, layer=layer, n_pages=n_pages, past=past, nq=nq), name="scmp",
        grid_spec=grid_spec,
        out_shape=[jax.ShapeDtypeStruct((nb, 8 * nq, LANES), F32), jax.ShapeDtypeStruct((nb, NSA_G * nq, nselp), F32)],
        compiler_params=_cparams(("arbitrary",)),
    )(pt, qu, cache_k, cache_v, *wlist)


def _softmax_rows(s):
    m = jnp.max(s, axis=-1, keepdims=True)
    p = jnp.exp2(s - m)
    return p, jnp.sum(p, axis=-1, keepdims=True)


def _sselwin_body(pt_ref, qr_ref, sel_ref, skn_ref, svn_ref, wkn_ref, wvn_ref, wks_ref, wvs_ref, e_ref, sk_hbm, sv_hbm,
                  osel_ref, owin_ref, kbuf, vbuf, sem, *, layer, n_pages, past, nq):
    ktot = past + PAGE
    rows = 8 * nq
    slot = _gather_pages(pt_ref, n_pages, [(sk_hbm, kbuf), (sv_hbm, vbuf)], sem, layer)
    pad = jnp.zeros((PAGE - nq, LANES), F32)
    kbuf[slot, :, pl.ds(past, PAGE)] = jnp.concatenate([skn_ref[...], pad], axis=0).T
    vbuf[slot, :, pl.ds(past, PAGE)] = jnp.concatenate([svn_ref[...], pad], axis=0).T
    chunks = _key_chunks(n_pages + 1)
    qa = _stack_heads(qr_ref[...])
    s = jnp.concatenate([_dot(qa, kbuf[slot, :, pl.ds(c0, cn)].astype(BF16)) for c0, cn in chunks], axis=1)
    se = _dot(sel_ref[...].astype(BF16), e_ref[...])
    se = jnp.concatenate([se[nq * g:nq * (g + 1)] for g in range(NSA_G) for _ in range(NSA_HPG)], axis=0)
    kpos = lax.broadcasted_iota(jnp.int32, (rows, ktot), 1)
    qpos = past + (lax.broadcasted_iota(jnp.int32, (rows, ktot), 0) & (nq - 1))
    s = jnp.where((se > 0.5) & (kpos <= qpos), s, NEG)
    p, l = _softmax_rows(s)
    pb = p.astype(BF16)
    o = _dot_nt(pb[:, 0:chunks[0][1]], vbuf[slot, :, pl.ds(0, chunks[0][1])].astype(BF16))
    for c0, cn in chunks[1:]:
        o = o + _dot_nt(pb[:, c0:c0 + cn], vbuf[slot, :, pl.ds(c0, cn)].astype(BF16))
    osel_ref[...] = o / l
    wb = wks_ref.shape[0]
    kw = jnp.concatenate([wks_ref[...], wkn_ref[...], pad], axis=0).astype(BF16)
    vw = jnp.concatenate([wvs_ref[...], wvn_ref[...], pad], axis=0).astype(BF16)
    s = _dot_nt(qa, kw)
    i = lax.broadcasted_iota(jnp.int32, (rows, wb + PAGE), 1)
    kp = jnp.where(i < wb, past - wb + i, past + i - wb)
    qp = past + (lax.broadcasted_iota(jnp.int32, (rows, wb + PAGE), 0) & (nq - 1))
    dlt = qp - kp
    s = jnp.where((dlt >= 0) & (dlt < WINDOW) & (i < wb + nq), s, NEG)
    p, l = _softmax_rows(s)
    owin_ref[...] = _dot(p.astype(BF16), vw) / l


def _sselwin(pt, qr, sel, skn, svn, wkn, wvn, wks, wvs, cache_k, cache_v, layer, nb, nq, past):
    n_pages = past // PAGE
    ktot = past + PAGE
    nselp = sel.shape[-1]
    e = (np.arange(ktot)[None, :] // SEL_BLK == np.arange(nselp)[:, None]).astype(np.float32)
    e = jnp.asarray(e, BF16)
    wb = wks.shape[1]
    any_spec = pl.BlockSpec(memory_space=pl.ANY)
    new = pl.BlockSpec((nq, LANES), lambda bi, pt_: (bi, 0))
    grid_spec = pltpu.PrefetchScalarGridSpec(
        num_scalar_prefetch=1, grid=(nb,),
        in_specs=[pl.BlockSpec((nq, 1024), lambda bi, pt_: (bi, 0)),
                  pl.BlockSpec((None, NSA_G * nq, nselp), lambda bi, pt_: (bi, 0, 0)),
                  new, new, new, new,
                  pl.BlockSpec((None, wb, LANES), lambda bi, pt_: (bi, 0, 0)),
                  pl.BlockSpec((None, wb, LANES), lambda bi, pt_: (bi, 0, 0)),
                  pl.BlockSpec(e.shape, lambda bi, pt_: (0, 0)), any_spec, any_spec],
        out_specs=[pl.BlockSpec((None, 8 * nq, LANES), lambda bi, pt_: (bi, 0, 0)),
                   pl.BlockSpec((None, 8 * nq, LANES), lambda bi, pt_: (bi, 0, 0))],
        scratch_shapes=[pltpu.VMEM((2, LANES, ktot), F32), pltpu.VMEM((2, LANES, ktot), F32), pltpu.SemaphoreType.DMA((2, 2))])
    return pl.pallas_call(
        functools.partial(_sselwin_body, layer=layer, n_pages=n_pages, past=past, nq=nq), name="sselwin",
        grid_spec=grid_spec,
        out_shape=[jax.ShapeDtypeStruct((nb, 8 * nq, LANES), F32)] * 2,
        compiler_params=_cparams(("arbitrary",)),
    )(pt, qr, sel, skn, svn, wkn, wvn, wks, wvs, e, cache_k, cache_v)


def _sdiff_body(pt_ref, dq_ref, dkn_ref, dvn_ref, lq1_ref, lk1_ref, lq2_ref, lk2_ref, sg_ref, dk_hbm, dv_hbm,
                o_ref, kbuf, vbuf, sem, *, layer, n_pages, past, nq, lam_init):
    ktot = past + PAGE
    rows = 8 * nq
    slot = _gather_pages(pt_ref, n_pages, [(dk_hbm, kbuf), (dv_hbm, vbuf)], sem, layer)
    pad = jnp.zeros((PAGE - nq, 256), F32)
    kbuf[slot, :, pl.ds(past, PAGE)] = jnp.concatenate([dkn_ref[...], pad], axis=0).T
    vbuf[slot, :, pl.ds(past, PAGE)] = jnp.concatenate([dvn_ref[...], pad], axis=0).T
    chunks = _key_chunks(n_pages + 1)
    q = dq_ref[...]
    zero = jnp.zeros((nq, LANES), BF16)
    blocks = []
    for a in range(8):
        chunk = q[:, LANES * a:LANES * (a + 1)]
        blocks.append(jnp.concatenate([chunk, zero] if a < 4 else [zero, chunk], axis=1))
    qa = jnp.concatenate(blocks, axis=0)
    s = jnp.concatenate([_dot(qa, kbuf[slot, :, pl.ds(c0, cn)].astype(BF16)) for c0, cn in chunks], axis=1)
    kpos = lax.broadcasted_iota(jnp.int32, (rows, ktot), 1)
    qpos = past + (lax.broadcasted_iota(jnp.int32, (rows, ktot), 0) & (nq - 1))
    s = jnp.where(kpos <= qpos, s, NEG)
    p, l = _softmax_rows(s)
    pb = p.astype(BF16)
    o = _dot_nt(pb[:, 0:chunks[0][1]], vbuf[slot, :, pl.ds(0, chunks[0][1])].astype(BF16))
    for c0, cn in chunks[1:]:
        o = o + _dot_nt(pb[:, c0:c0 + cn], vbuf[slot, :, pl.ds(c0, cn)].astype(BF16))
    o = o / l
    lam = (jnp.exp(jnp.sum(lq1_ref[...] * lk1_ref[...], keepdims=True))
           - jnp.exp(jnp.sum(lq2_ref[...] * lk2_ref[...], keepdims=True)) + lam_init)
    lane = lax.broadcasted_iota(jnp.int32, (nq, 256), 1)
    out = jnp.zeros((nq, 256), F32)
    for h in range(DIFF_H):
        d = o[2 * nq * h:2 * nq * h + nq] - lam * o[2 * nq * h + nq:2 * nq * (h + 1)]
        inh = (lane >> 6) == h
        ms = jnp.sum(jnp.where(inh, d * d, 0.0), axis=-1, keepdims=True) * (1.0 / DIFF_DV)
        out = out + jnp.where(inh, d * lax.rsqrt(ms + LN_EPS), 0.0)
    o_ref[...] = out * sg_ref[...] * (1.0 - lam_init)


def _sdiff(pt, dq, dkn, dvn, lams, sg_row, cache_k, cache_v, layer, nb, nq, past, lam_init):
    n_pages = past // PAGE
    ktot = past + PAGE
    any_spec = pl.BlockSpec(memory_space=pl.ANY)
    new = pl.BlockSpec((nq, 256), lambda bi, pt_: (bi, 0))
    small = [pl.BlockSpec(a.shape, lambda bi, pt_: (0, 0)) for a in (*lams, sg_row)]
    grid_spec = pltpu.PrefetchScalarGridSpec(
        num_scalar_prefetch=1, grid=(nb,),
        in_specs=[pl.BlockSpec((nq, 1024), lambda bi, pt_: (bi, 0)), new, new] + small + [any_spec, any_spec],
        out_specs=pl.BlockSpec((nq, 256), lambda bi, pt_: (bi, 0)),
        scratch_shapes=[pltpu.VMEM((2, 256, ktot), F32), pltpu.VMEM((2, 256, ktot), F32), pltpu.SemaphoreType.DMA((2, 2))])
    return pl.pallas_call(
        functools.partial(_sdiff_body, layer=layer, n_pages=n_pages, past=past, nq=nq, lam_init=lam_init), name="sdiff",
        grid_spec=grid_spec,
        out_shape=jax.ShapeDtypeStruct((nb * nq, 256), F32),
        compiler_params=_cparams(("arbitrary",)),
    )(pt, dq, dkn, dvn, *lams, sg_row, cache_k, cache_v)


def _prep_w_in(w):
    pts = np.cumsum([256, 256, 512, 128, 128, 128, 128, 128, 128, 24, 256, 256, 256])[:-1].tolist()
    ca, cg, nq, ck, cv, sk, sv, wk, wv, gt, dq, dk, dv = jnp.split(w, pts, axis=1)
    gtp = jnp.pad(gt, ((0, 0), (0, LANES - gt.shape[1])))
    return jnp.concatenate([ca, cg, nq * (NSA_DH ** -0.5 * LOG2E), ck, cv, sk, sv, wk, wv, gtp, dq, dk, dv], axis=1).astype(BF16)


def _rope_tables(pos):
    out = []
    lane = np.arange(LANES)
    for half in (32, 16):
        inv = ROPE_THETA ** (-jnp.arange(half, dtype=F32) / half)
        ang = pos.astype(F32)[:, None] * inv[None, :]
        idx = lane % half
        sign = jnp.asarray(np.where(lane % (2 * half) < half, -1.0, 1.0), F32)
        out += [jnp.cos(ang)[:, idx], jnp.sin(ang)[:, idx] * sign[None, :]]
    return out


def _prep_cmp(pe, w1, b1, w2):
    def halves(x):
        res = []
        for part in (x[:16 * NSA_DH], x[16 * NSA_DH:]):
            p4 = part.reshape(CMP_STRIDE, 1, NSA_DH, 1, -1)
            eye = jnp.eye(NSA_G, dtype=F32)[None, :, None, :, None]
            res.append((p4 * eye).reshape(CMP_STRIDE * NSA_G * NSA_DH, NSA_G * part.shape[-1]))
        return res
    wa, wb = halves(w1)
    pea = jnp.tile(pe[:16, None, :], (1, NSA_G, 1)).reshape(1, -1)
    peb = jnp.tile(pe[16:, None, :], (1, NSA_G, 1)).reshape(1, -1)
    w2bd = (w2[None, :, None, :] * jnp.eye(NSA_G, dtype=F32)[:, None, :, None]).reshape(NSA_G * CMP_HID, NSA_G * NSA_DH)
    return pea, peb, wa.astype(BF16), wb.astype(BF16), jnp.tile(b1, NSA_G)[None, :], w2bd.astype(BF16)


def _gate_expand():
    e = np.zeros((LANES, 3 * 512), np.float32)
    for h in range(NSA_H):
        for br in range(3):
            e[3 * h + br, 512 * br + 64 * h:512 * br + 64 * (h + 1)] = 1.0
    return jnp.asarray(e, BF16)


def _rows_to_tokens(o, nb, nq):
    o6 = o.reshape(nb, NSA_G, NSA_HPG, nq, NSA_G, NSA_DH)
    pick = jnp.stack([o6[:, g, :, :, g, :] for g in range(NSA_G)], axis=1)
    return pick.transpose(0, 3, 1, 2, 4).reshape(nb * nq, NSA_H * NSA_DH)


def _prompt_layer(x, lw, tabs, nb, t, lam_init, alpha):
    n = nb * t
    (u, qu, qr, gt, dq, ck, cv, skb, wkb, dkb, svb, wvb, dvb,
     ckt, cvt, skt, svt, wkt, wvt, dkt, dvt) = _inproj(x, lw["w_in"], tabs, nb, t, 256, True)
    ext = jnp.pad(u.reshape(nb, t, 256), ((0, 0), (CONV_PAD, 0), (0, 0)))
    yc = _conv(ext, lw["dw_w"], lw["dw_b"], lw["cln_g"], lw["cln_b"], t, 256).reshape(n, 256)
    ocmp, sel = _pcmp(qu, ck, cv, lw["cmp"], nb, t)
    osel = _flash("sel", qr, skb, svb, nb, t, extra=(sel,))
    owin = _win(qr, wkb, wvb, nb, t)
    odiff = _flash("diff", dq, dkb, dvb, nb, t, extra=lw["lams"] + (lw["sg_col"],), lam_init=lam_init)
    x1 = _outproj(x, yc, ocmp, osel, owin, gt, odiff, lw["gate_e"], lw["w_out"], lw["ln1_g"], lw["ln1_b"], alpha, 256)
    x2 = _ffn(x1, lw["w1"], lw["w3"], lw["w2"], lw["ln2_g"], lw["ln2_b"], alpha, 512, 2)
    nk = min(WINDOW, t)
    rows_major = lambda a: a.reshape(nb, a.shape[1] // 64, 64, a.shape[2]).transpose(0, 3, 1, 2)
    news = (rows_major(ckt), rows_major(cvt), rows_major(skt), rows_major(svt), rows_major(dkt), rows_major(dvt),
            rows_major(wkt[:, :, t - nk:]), rows_major(wvt[:, :, t - nk:]),
            u.reshape(nb, t, 256)[:, t - (CONV_W - 1):])
    return x2, news


def _sample_layer(x, lw, tabs, caches, states, pt, layer, nb, nq, past, lam_init, alpha):
    n = nb * nq
    (u, qu, qr, gt, dq, ck, cv, sk, sv, wk, wv, dk, dv) = _inproj(x, lw["w_in"], tabs, nb, nq, n, False)
    c_cmp_k, c_cmp_v, c_sel_k, c_sel_v, c_diff_k, c_diff_v = caches
    st_wk, st_wv, st_conv = states
    ext = jnp.concatenate([jnp.zeros((nb, CONV_PAD - (CONV_W - 1), 256), F32), st_conv, u.reshape(nb, nq, 256)], axis=1)
    yc = _conv(ext, lw["dw_w"], lw["dw_b"], lw["cln_g"], lw["cln_b"], nq, nq).reshape(n, 256)
    ocmp, sel = _scmp(pt, qu, c_cmp_k, c_cmp_v, lw["cmp"], layer, nb, nq, past)
    wb = st_wk.shape[1]
    osel, owin = _sselwin(pt, qr, sel, sk, sv, wk, wv, st_wk.reshape(nb, wb, LANES), st_wv.reshape(nb, wb, LANES),
                          c_sel_k, c_sel_v, layer, nb, nq, past)
    odiff = _sdiff(pt, dq, dk, dv, lw["lams"], lw["sg_row"], c_diff_k, c_diff_v, layer, nb, nq, past, lam_init)
    x1 = _outproj(x, yc, _rows_to_tokens(ocmp, nb, nq), _rows_to_tokens(osel, nb, nq), _rows_to_tokens(owin, nb, nq),
                  gt, odiff, lw["gate_e"], lw["w_out"], lw["ln1_g"], lw["ln1_b"], alpha, n)
    x2 = _ffn(x1, lw["w1"], lw["w3"], lw["w2"], lw["ln2_g"], lw["ln2_b"], alpha, n, 2)
    new_wk = jnp.concatenate([st_wk, wk.reshape(nb, nq, NSA_G, NSA_DH)], axis=1)[:, -wb:]
    new_wv = jnp.concatenate([st_wv, wv.reshape(nb, nq, NSA_G, NSA_DH)], axis=1)[:, -wb:]
    new_conv = jnp.concatenate([st_conv, u.reshape(nb, nq, 256)], axis=1)[:, -(CONV_W - 1):]
    news = (ck.reshape(nb, nq, NSA_G, NSA_DH), cv.reshape(nb, nq, NSA_G, NSA_DH),
            sk.reshape(nb, nq, NSA_G, NSA_DH), sv.reshape(nb, nq, NSA_G, NSA_DH),
            dk.reshape(nb, nq, DIFF_H, 2 * DIFF_DQK), dv.reshape(nb, nq, DIFF_H, DIFF_DV),
            new_wk, new_wv, new_conv)
    return x2, news


def kernel(x_prompt, x_sample, cache_nsa_cmp_k, cache_nsa_cmp_v, cache_nsa_sel_k, cache_nsa_sel_v, cache_diff_k, cache_diff_v, state_nsa_win_k, state_nsa_win_v, state_conv, page_table, w_in, conv_dw_w, conv_dw_b, conv_ln_g, conv_ln_b, cmp_pe_k, cmp_w1_k, cmp_b1_k, cmp_w2_k, cmp_pe_v, cmp_w1_v, cmp_b1_v, cmp_w2_v, diff_lq1, diff_lk1, diff_lq2, diff_lk2, diff_subln_g, w_out, ln1_g, ln1_b, ln2_g, ln2_b, ffn_w1, ffn_w3, ffn_w2):
    nb, t, d = x_prompt.shape
    sb, nq, _ = x_sample.shape
    depth = w_in.shape[0]
    n_pool = cache_nsa_cmp_k.shape[1]
    past = page_table.shape[1] * PAGE
    alpha = (2 * depth) ** 0.25
    tabs_p = _rope_tables(jnp.arange(t, dtype=jnp.int32))
    tabs_s = _rope_tables(jnp.tile(past + jnp.arange(nq, dtype=jnp.int32), sb))
    as_pages = lambda c: c.transpose(0, 1, 3, 4, 2).reshape(depth, n_pool, c.shape[3] * c.shape[4], PAGE)
    caches = tuple(as_pages(c) for c in (cache_nsa_cmp_k, cache_nsa_cmp_v, cache_nsa_sel_k, cache_nsa_sel_v,
                                         cache_diff_k, cache_diff_v))
    gate_e = _gate_expand()
    xp = x_prompt.reshape(nb * t, d)
    xs = x_sample.reshape(sb * nq, d)
    outs_p, outs_s = [], []
    for l in range(depth):
        ck = _prep_cmp(cmp_pe_k[l], cmp_w1_k[l], cmp_b1_k[l], cmp_w2_k[l])
        cv = _prep_cmp(cmp_pe_v[l], cmp_w1_v[l], cmp_b1_v[l], cmp_w2_v[l])
        names = ("pa", "pb", "wa", "wb", "b1", "w2")
        cmpw = {n_ + "k": a for n_, a in zip(names, ck)}
        cmpw.update({n_ + "v": a for n_, a in zip(names, cv)})
        lw = dict(
            w_in=_prep_w_in(w_in[l]), dw_w=conv_dw_w[l], dw_b=conv_dw_b[l][None], cln_g=conv_ln_g[l][None],
            cln_b=conv_ln_b[l][None], cmp=cmpw,
            lams=(diff_lq1[l][None], diff_lk1[l][None], diff_lq2[l][None], diff_lk2[l][None]),
            sg_col=diff_subln_g[l][:, None], sg_row=jnp.tile(diff_subln_g[l], DIFF_H)[None],
            gate_e=gate_e, w_out=w_out[l].astype(BF16), ln1_g=ln1_g[l][None], ln1_b=ln1_b[l][None],
            ln2_g=ln2_g[l][None], ln2_b=ln2_b[l][None],
            w1=ffn_w1[l].astype(BF16), w3=ffn_w3[l].astype(BF16), w2=ffn_w2[l].astype(BF16))
        lam_init = 0.8 - 0.6 * math.exp(-0.3 * l)
        xp, new_p = _prompt_layer(xp, lw, tabs_p, nb, t, lam_init, alpha)
        xs, new_s = _sample_layer(xs, lw, tabs_s, caches, (state_nsa_win_k[l], state_nsa_win_v[l], state_conv[l]),
                                  page_table, l, sb, nq, past, lam_init, alpha)
        outs_p.append(new_p)
        outs_s.append(new_s)
    stk_p = [jnp.stack([o[i] for o in outs_p]) for i in range(9)]
    stk_s = [jnp.stack([o[i] for o in outs_s]) for i in range(9)]
    return (xp.reshape(nb, t, d), xs.reshape(sb, nq, d), *stk_p, *stk_s)
```

```python
import functools
import math

import jax
import jax.numpy as jnp
import numpy as np
from jax import lax
from jax.experimental import pallas as pl
from jax.experimental.pallas import tpu as pltpu

F32 = jnp.float32
BF16 = jnp.bfloat16

CONV_W = 31
NSA_H = 8
NSA_G = 2
NSA_HPG = NSA_H // NSA_G
NSA_DH = 64
CMP_STRIDE = 16
CMP_LEN = 32
CMP_HID = 128
SEL_BLK = 64
SEL_TOPN = 16
WINDOW = 512
FORCE_SCORE = 1e9
DIFF_H = 4
DIFF_DV = 64
DIFF_DQK = 32
ROPE_THETA = 10000.0
LN_EPS = 1e-5
PAGE = 128

LANES = 128
TQ = 128
TK = 256
LOG2E = 1.4426950408889634
NEG = -1e30
VMEM_LIMIT = 52 * 1024 * 1024

C_CA, C_CG, C_Q, C_KV, C_GT, C_DQ, C_DK, C_DV, C_END = 0, 256, 512, 1024, 1792, 1920, 2176, 2432, 2688


def _cparams(sem):
    return pltpu.CompilerParams(dimension_semantics=sem, vmem_limit_bytes=VMEM_LIMIT)


def _sigmoid(x):
    return 1.0 / (1.0 + jnp.exp(-x))


def _ln_rows(x, g, b):
    mu = jnp.mean(x, axis=-1, keepdims=True)
    xc = x - mu
    var = jnp.mean(xc * xc, axis=-1, keepdims=True)
    return xc * lax.rsqrt(var + LN_EPS) * g + b


def _dot(a, b):
    return jnp.dot(a, b, preferred_element_type=F32)


def _dot_nt(a, b):
    return lax.dot_general(a, b, (((1,), (1,)), ((), ())), preferred_element_type=F32)


def _split3_dot(x, m):
    hi = x.astype(BF16)
    r1 = x - hi.astype(F32)
    mid = r1.astype(BF16)
    lo = (r1 - mid.astype(F32)).astype(BF16)
    return _dot(hi, m) + _dot(mid, m) + _dot(lo, m)


def _rope(x, cos, sin_signed, half):
    lane = lax.broadcasted_iota(jnp.int32, x.shape, 1)
    first = (lane & (2 * half - 1)) < half
    rot = jnp.where(first, pltpu.roll(x, LANES - half, 1), pltpu.roll(x, half, 1))
    return x * cos + rot * sin_signed


def _inproj_body(x_ref, w_ref, c64_ref, s64_ref, c32_ref, s32_ref,
                 u_ref, qu_ref, qr_ref, gt_ref, dq_ref, ck_ref, cv_ref, *refs, dq_scale, transposed):
    xb = x_ref[...].astype(BF16)

    def mm(lo, hi):
        return _dot(xb, w_ref[:, lo:hi])

    c64, s64, c32, s32 = c64_ref[...], s64_ref[...], c32_ref[...], s32_ref[...]
    z = mm(C_CA, C_Q)
    u_ref[...] = z[:, :256] * _sigmoid(z[:, 256:])
    lane = lax.broadcasted_iota(jnp.int32, (x_ref.shape[0], LANES), 1)
    for j in range(4):
        zq = mm(C_Q + LANES * j, C_Q + LANES * (j + 1))
        g = j // 2
        keep = (lane < 64) if g == 0 else (lane >= 64)
        for src, dst_ref in ((zq, qu_ref), (_rope(zq, c64, s64, 32), qr_ref)):
            for e in range(2):
                v = src if e == g else pltpu.roll(src, 64, 1)
                dst_ref[:, LANES * (2 * j + e):LANES * (2 * j + e + 1)] = jnp.where(keep, v, 0.0).astype(BF16)
    z = mm(C_KV, C_GT)
    ck = z[:, 0:128]
    cv = z[:, 128:256]
    ck_ref[...] = ck
    cv_ref[...] = cv
    sk = _rope(z[:, 256:384], c64, s64, 32)
    sv = z[:, 384:512]
    wk = _rope(z[:, 512:640], c64, s64, 32)
    wv = z[:, 640:768]
    gt_ref[...] = _sigmoid(mm(C_GT, C_DQ))
    for c in range(2):
        zq = _rope(mm(C_DQ + LANES * c, C_DQ + LANES * (c + 1)), c32, s32, 16) * dq_scale
        for a in range(4):
            dq_ref[:, LANES * (4 * c + a):LANES * (4 * c + a + 1)] = jnp.where((lane >> 5) == a, zq, 0.0).astype(BF16)
    z = mm(C_DK, C_DV)
    dk = [_rope(z[:, :128], c32, s32, 16), _rope(z[:, 128:], c32, s32, 16)]
    z = mm(C_DV, C_END)
    dv = [z[:, :128], z[:, 128:]]
    if not transposed:
        sk_ref, sv_ref, wk_ref, wv_ref, dk_ref, dv_ref = refs
        sk_ref[...] = sk
        sv_ref[...] = sv
        wk_ref[...] = wk
        wv_ref[...] = wv
        for c in range(2):
            dk_ref[:, LANES * c:LANES * (c + 1)] = dk[c]
            dv_ref[:, LANES * c:LANES * (c + 1)] = dv[c]
        return
    (skb_ref, wkb_ref, dkb_ref, svb_ref, wvb_ref, dvb_ref,
     ckt_ref, cvt_ref, skt_ref, svt_ref, wkt_ref, wvt_ref, dkt_ref, dvt_ref) = refs
    skb_ref[...] = sk.astype(BF16)
    wkb_ref[...] = wk.astype(BF16)
    ckt_ref[...] = ck.T
    cvt_ref[...] = cv.T
    skt_ref[...] = sk.T
    wkt_ref[...] = wk.T
    svt = sv.T
    svt_ref[...] = svt
    svb_ref[...] = svt.astype(BF16)
    wvt = wv.T
    wvt_ref[...] = wvt
    wvb_ref[...] = wvt.astype(BF16)
    for c in range(2):
        rows = slice(LANES * c, LANES * (c + 1))
        dkb_ref[:, rows] = dk[c].astype(BF16)
        dkt_ref[rows, :] = dk[c].T
        dvt = dv[c].T
        dvt_ref[rows, :] = dvt
        dvb_ref[rows, :] = dvt.astype(BF16)


def _inproj(x, w, tabs, nb, t, tm, transposed):
    n, d = x.shape
    nt = t // tm if transposed else 1
    grid = (n // tm,)
    row = lambda c: pl.BlockSpec((tm, c), lambda i: (i, 0))
    tab = pl.BlockSpec((tm, LANES), (lambda i: (i % nt, 0)) if transposed else (lambda i: (i, 0)))
    in_specs = [row(d), pl.BlockSpec((d, C_END), lambda i: (0, 0)), tab, tab, tab, tab]
    shapes = [(256, F32), (1024, BF16), (1024, BF16), (128, F32), (1024, BF16), (128, F32), (128, F32)]
    if transposed:
        shapes += [(128, BF16), (128, BF16), (256, BF16)]
    else:
        shapes += [(128, F32)] * 4 + [(256, F32)] * 2
    out_shape = [jax.ShapeDtypeStruct((n, c), dt) for c, dt in shapes]
    out_specs = [row(c) for c, _ in shapes]
    if transposed:
        for c, dt in [(128, BF16), (128, BF16), (256, BF16)] + [(128, F32)] * 6 + [(256, F32)] * 2:
            out_shape.append(jax.ShapeDtypeStruct((nb, c, t), dt))
            out_specs.append(pl.BlockSpec((None, c, tm), lambda i: (i // nt, 0, i % nt)))
    return pl.pallas_call(
        functools.partial(_inproj_body, dq_scale=DIFF_DQK ** -0.5 * LOG2E, transposed=transposed), name="inproj",
        grid=grid, in_specs=in_specs, out_specs=out_specs, out_shape=out_shape,
        compiler_params=_cparams(("parallel",)),
    )(x, w, *tabs)


CONV_PAD = 32


def _conv_body(ext_ref, w_ref, b_ref, g_ref, beta_ref, y_ref, acc_ref, sh_ref, *, tt, rs):
    t0 = pl.multiple_of(pl.program_id(1) * tt, 8)
    off = CONV_PAD - (CONV_W - 1)
    for c in range(2):
        cs = slice(LANES * c, LANES * (c + 1))
        for r in range(tt // rs):
            win = ext_ref[pl.ds(t0 + rs * r, rs + CONV_PAD), cs]
            acc = jnp.zeros((rs, LANES), F32)
            for r8 in range(8):
                taps = range(r8, CONV_W, 8)
                rows = 8 * (len(taps) - 1) + rs
                sh_ref[0:rows, :] = win[off + r8:off + r8 + rows, :]
                for a, k in enumerate(taps):
                    acc = acc + sh_ref[8 * a:8 * a + rs, :] * w_ref[k:k + 1, cs]
            acc_ref[rs * r:rs * (r + 1), cs] = acc
    y = _ln_rows(acc_ref[...] + b_ref[...], g_ref[...], beta_ref[...])
    y_ref[...] = y * _sigmoid(y)


def _conv(ext, w, b, g, beta, t, tt):
    nb, le, c = ext.shape
    rs = min(tt, 64)
    vec = pl.BlockSpec((1, c), lambda bi, ti: (0, 0))
    return pl.pallas_call(
        functools.partial(_conv_body, tt=tt, rs=rs), name="conv",
        grid=(nb, t // tt),
        in_specs=[pl.BlockSpec((None, le, c), lambda bi, ti: (bi, 0, 0)),
                  pl.BlockSpec((CONV_W, c), lambda bi, ti: (0, 0)), vec, vec, vec],
        out_specs=pl.BlockSpec((None, tt, c), lambda bi, ti: (bi, ti, 0)),
        out_shape=jax.ShapeDtypeStruct((nb, t, c), F32),
        scratch_shapes=[pltpu.VMEM((tt, c), F32), pltpu.VMEM((rs + CONV_PAD, LANES), F32)],
        compiler_params=_cparams(("parallel", "parallel")),
    )(ext, w, b, g, beta)


def _compress(src_ref, n16, pea, peb, wa_ref, wb_ref, b1, w2_ref):
    x = jnp.concatenate([src_ref[pl.ds(p, n16, stride=CMP_STRIDE), :] for p in range(CMP_STRIDE)], axis=1)
    a = _dot((x + pea).astype(BF16), wa_ref[...])
    bm = _dot((x + peb).astype(BF16), wb_ref[...])
    h = a + pltpu.roll(bm, n16 - 1, 0) + b1
    gl = 0.5 * h * (1.0 + jnp.tanh(0.7978845608028654 * (h + 0.044715 * (h * h * h))))
    return _dot(gl.astype(BF16), w2_ref[...])


def _top_rows(score, ridx, k):
    nv = score.shape[0] // 8
    tiles = [score[8 * v:8 * (v + 1), :] for v in range(nv)]
    cnts = [jnp.zeros(t.shape, F32) for t in tiles]
    for jp in range(score.shape[0]):
        row = score[jp:jp + 1, :]
        for v in range(nv):
            if 8 * v > jp:
                beats = row >= tiles[v]
            elif 8 * v + 7 < jp:
                beats = row > tiles[v]
            else:
                beats = (row > tiles[v]) | ((row == tiles[v]) & (ridx[8 * v:8 * (v + 1), :] > jp))
            cnts[v] = cnts[v] + jnp.where(beats, 1.0, 0.0)
    return jnp.concatenate(cnts, axis=0) < k


def _pcmp_body(qu_ref, ck_ref, cv_ref, pak_ref, pbk_ref, wak_ref, wbk_ref, b1k_ref, w2k_ref,
               pav_ref, pbv_ref, wav_ref, wbv_ref, b1v_ref, w2v_ref, mt_ref,
               o_ref, sel_ref, kcc_ref, vcct_ref, *, n16, nsel):
    ti = pl.program_id(1)

    @pl.when(ti == 0)
    def _():
        kcc = _compress(ck_ref, n16, pak_ref[...], pbk_ref[...], wak_ref, wbk_ref, b1k_ref[...], w2k_ref)
        kcc_ref[...] = kcc.astype(BF16)
        vcc = _compress(cv_ref, n16, pav_ref[...], pbv_ref[...], wav_ref, wbv_ref, b1v_ref[...], w2v_ref)
        vcct_ref[...] = vcc.T.astype(BF16)

    q = qu_ref[...]
    kcc = kcc_ref[...]
    vcct = vcct_ref[...]
    nq = 4 * TQ
    cidx = lax.broadcasted_iota(jnp.int32, (n16, nq), 0)
    qpos = ti * TQ + (lax.broadcasted_iota(jnp.int32, (n16, nq), 1) & (TQ - 1))
    vis = (CMP_STRIDE * cidx + CMP_LEN - 1 <= qpos) & (cidx < n16 - 1)
    jidx = lax.broadcasted_iota(jnp.int32, (nsel, TQ), 0)
    qp1 = ti * TQ + lax.broadcasted_iota(jnp.int32, (nsel, TQ), 1)
    cur = qp1 >> 6
    forced = (jidx == 0) | (jidx == cur) | (jidx == cur - 1)
    for g in range(NSA_G):
        qs = jnp.concatenate([q[:, LANES * (4 * g + a):LANES * (4 * g + a + 1)] for a in range(4)], axis=0)
        st = _dot_nt(kcc, qs)
        st = jnp.where(vis, st, NEG)
        m = jnp.max(st, axis=0, keepdims=True)
        p = jnp.where(vis, jnp.exp2(st - m), 0.0)
        l = jnp.sum(p, axis=0, keepdims=True)
        p = p / jnp.maximum(l, 1e-30)
        ot = _dot(vcct, p.astype(BF16))
        for a2 in range(2):
            blk = jnp.concatenate([ot[64 * g:64 * g + 64, TQ * (2 * a2 + e):TQ * (2 * a2 + e + 1)] for e in range(2)], axis=0)
            o_ref[:, 256 * g + LANES * a2:256 * g + LANES * (a2 + 1)] = blk.T
        imp = p[:, 0:TQ] + p[:, TQ:2 * TQ] + p[:, 2 * TQ:3 * TQ] + p[:, 3 * TQ:4 * TQ]
        hi = imp.astype(BF16)
        r1 = imp - hi.astype(F32)
        mid = r1.astype(BF16)
        lo = (r1 - mid.astype(F32)).astype(BF16)
        mt = mt_ref[...]
        isel = _dot(mt, hi) + _dot(mt, mid) + _dot(mt, lo)
        score = jnp.where(forced, FORCE_SCORE, isel)
        score = jnp.where(jidx <= cur, score, -jnp.inf)
        sel = _top_rows(score, jidx, min(SEL_TOPN, nsel)) & (jidx <= cur)
        sel_ref[g] = jnp.where(sel, 0.0, NEG)


def _pcmp(qu, ck, cv, cw, nb, t):
    n16 = t // CMP_STRIDE
    nsel = t // SEL_BLK
    nt = t // TQ
    mt = np.zeros((nsel, n16), np.float32)
    for j in range(nsel):
        for c in range(4 * j - 1, 4 * j + 4):
            if 0 <= c < n16 - 1:
                mt[j, c] = 1.0
    mt = jnp.asarray(mt, BF16)
    const = lambda a: pl.BlockSpec(a.shape, lambda bi, ti: (0,) * a.ndim)
    wlist = [cw[k] for k in ("pak", "pbk", "wak", "wbk", "b1k", "w2k", "pav", "pbv", "wav", "wbv", "b1v", "w2v")] + [mt]
    return pl.pallas_call(
        functools.partial(_pcmp_body, n16=n16, nsel=nsel), name="pcmp",
        grid=(nb, nt),
        in_specs=[pl.BlockSpec((TQ, 1024), lambda bi, ti: (bi * nt + ti, 0)),
                  pl.BlockSpec((None, t, LANES), lambda bi, ti: (bi, 0, 0)),
                  pl.BlockSpec((None, t, LANES), lambda bi, ti: (bi, 0, 0))] + [const(a) for a in wlist],
        out_specs=[pl.BlockSpec((TQ, 512), lambda bi, ti: (bi * nt + ti, 0)),
                   pl.BlockSpec((None, None, NSA_G, nsel, TQ), lambda bi, ti: (bi, ti, 0, 0, 0))],
        out_shape=[jax.ShapeDtypeStruct((nb * t, 512), F32),
                   jax.ShapeDtypeStruct((nb, nt, NSA_G, nsel, TQ), F32)],
        scratch_shapes=[pltpu.VMEM((n16, LANES), BF16), pltpu.VMEM((LANES, n16), BF16)],
        compiler_params=_cparams(("parallel", "arbitrary")),
    )(qu, ck.reshape(nb, t, LANES), cv.reshape(nb, t, LANES), *wlist)


def _stack_queries(q, s):
    return jnp.concatenate([q[:, LANES * (4 * s + a):LANES * (4 * s + a + 1)] for a in range(4)], axis=0)


def _flash_out(z, s, o_ref):
    for a2 in range(2):
        blk = jnp.concatenate([z[64 * s:64 * s + 64, TQ * (2 * a2 + e):TQ * (2 * a2 + e + 1)] for e in range(2)], axis=0)
        o_ref[:, 256 * s + LANES * a2:256 * s + LANES * (a2 + 1)] = blk.T


def _flash_body(*refs, mode, lam_init):
    if mode == "sel":
        q_ref, k_ref, vt_ref, selb_ref, o_ref, acc_ref, m_ref, l_ref, al_ref, s_ref, p_ref = refs
    else:
        (q_ref, k_ref, vt_ref, lq1_ref, lk1_ref, lq2_ref, lk2_ref, sg_ref, o_ref,
         acc_ref, m_ref, l_ref, al_ref, s_ref, p_ref) = refs
    ti = pl.program_id(1)
    nq = 4 * TQ
    q = q_ref[...]
    qs = [_stack_queries(q, s) for s in range(2)]
    koffs = [LANES * s if mode == "diff" else 0 for s in range(2)]
    m_ref[...] = jnp.full(m_ref.shape, NEG, F32)
    l_ref[...] = jnp.zeros(l_ref.shape, F32)
    acc_ref[...] = jnp.zeros(acc_ref.shape, F32)
    al_ref[...] = jnp.ones(al_ref.shape, F32)
    p_ref[...] = jnp.zeros(p_ref.shape, BF16)

    def scores(j, slot):
        k0 = pl.multiple_of(j * TK, TK)
        for s in range(2):
            s_ref[slot, s] = _dot_nt(k_ref[pl.ds(k0, TK), koffs[s]:koffs[s] + LANES], qs[s])

    def accumulate(j):
        k0 = pl.multiple_of(j * TK, TK)
        for s in range(2):
            acc_ref[s] = al_ref[s] * acc_ref[s] + _dot(vt_ref[koffs[s]:koffs[s] + LANES, pl.ds(k0, TK)], p_ref[s])

    def softmax(j, slot, causal):
        for s in range(2):
            st = s_ref[slot, s]
            if mode == "sel":
                rows = [jnp.broadcast_to(selb_ref[s, pl.ds((TK // SEL_BLK) * j + r, 1), :], (SEL_BLK, TQ))
                        for r in range(TK // SEL_BLK)]
                bias = jnp.concatenate(rows, axis=0)
                st = st + jnp.concatenate([bias, bias, bias, bias], axis=1)
            if causal:
                krow = lax.broadcasted_iota(jnp.int32, (TK, nq), 0)
                qcol = lax.broadcasted_iota(jnp.int32, (TK, nq), 1) & (TQ - 1)
                st = jnp.where(ti * TQ + qcol >= j * TK + krow, st, NEG)
            m_old = m_ref[s]
            m_new = jnp.maximum(m_old, jnp.max(st, axis=0, keepdims=True))
            alpha = jnp.exp2(m_old - m_new)
            p = jnp.exp2(st - m_new)
            l_ref[s] = alpha * l_ref[s] + jnp.sum(p, axis=0, keepdims=True)
            p_ref[s] = p.astype(BF16)
            al_ref[s] = alpha
            m_ref[s] = m_new

    n_full = (ti * TQ) // TK
    scores(0, 0)

    def body(j, c):
        slot = j & 1
        accumulate(jnp.maximum(j - 1, 0))
        softmax(j, slot, False)
        scores(j + 1, 1 - slot)
        return c

    lax.fori_loop(0, n_full, body, 0)
    accumulate(jnp.maximum(n_full - 1, 0))
    softmax(n_full, n_full & 1, True)
    accumulate(n_full)

    if mode == "diff":
        lam = (jnp.exp(jnp.sum(lq1_ref[...] * lk1_ref[...], keepdims=True))
               - jnp.exp(jnp.sum(lq2_ref[...] * lk2_ref[...], keepdims=True)) + lam_init)
    for s in range(2):
        z = acc_ref[s] / l_ref[s]
        if mode == "diff":
            halves = []
            for hh in range(2):
                zr = z[64 * hh:64 * hh + 64, :]
                d = zr[:, TQ * (2 * hh):TQ * (2 * hh + 1)] - lam * zr[:, TQ * (2 * hh + 1):TQ * (2 * hh + 2)]
                ms = jnp.mean(d * d, axis=0, keepdims=True)
                halves.append(d * lax.rsqrt(ms + LN_EPS) * sg_ref[...] * (1.0 - lam_init))
            o_ref[:, LANES * s:LANES * (s + 1)] = jnp.concatenate(halves, axis=0).T
        else:
            _flash_out(z, s, o_ref)


def _flash(mode, q, k, vt, nb, t, extra=(), lam_init=0.0):
    nt = t // TQ
    kw = k.shape[-1]
    ow = 256 if mode == "diff" else 512
    in_specs = [pl.BlockSpec((TQ, 1024), lambda bi, ti: (bi * nt + ti, 0)),
                pl.BlockSpec((None, t, kw), lambda bi, ti: (bi, 0, 0)),
                pl.BlockSpec((None, kw, t), lambda bi, ti: (bi, 0, 0))]
    if mode == "sel":
        nsel = t // SEL_BLK
        in_specs.append(pl.BlockSpec((None, None, NSA_G, nsel, TQ), lambda bi, ti: (bi, ti, 0, 0, 0)))
    else:
        in_specs += [pl.BlockSpec(a.shape, lambda bi, ti: (0, 0)) for a in extra]
    return pl.pallas_call(
        functools.partial(_flash_body, mode=mode, lam_init=lam_init), name="flash_" + mode,
        grid=(nb, nt), in_specs=in_specs,
        out_specs=pl.BlockSpec((TQ, ow), lambda bi, ti: (bi * nt + ti, 0)),
        out_shape=jax.ShapeDtypeStruct((nb * t, ow), F32),
        scratch_shapes=[pltpu.VMEM((2, LANES, 4 * TQ), F32)] + [pltpu.VMEM((2, 1, 4 * TQ), F32)] * 3
        + [pltpu.VMEM((2, 2, TK, 4 * TQ), F32), pltpu.VMEM((2, TK, 4 * TQ), BF16)],
        compiler_params=_cparams(("parallel", "parallel")),
    )(q, k.reshape(nb, t, kw), vt, *extra)


def _win_body(q_ref, k_ref, vt_ref, o_ref):
    nq = 4 * TQ
    span = WINDOW + TQ
    krow = lax.broadcasted_iota(jnp.int32, (span, nq), 0)
    qcol = lax.broadcasted_iota(jnp.int32, (span, nq), 1) & (TQ - 1)
    for u in range(WIN_TILES):
        ti = pl.program_id(1) * WIN_TILES + u
        k0 = pl.multiple_of(jnp.maximum(ti * TQ - WINDOW, 0), TQ)
        q = q_ref[TQ * u:TQ * (u + 1), :]
        kt = k_ref[pl.ds(k0, span), :]
        vt = vt_ref[:, pl.ds(k0, span)]
        dlt = ti * TQ + qcol - (k0 + krow)
        vis = (dlt >= 0) & (dlt < WINDOW)
        for s in range(2):
            st = jnp.where(vis, _dot_nt(kt, _stack_queries(q, s)), NEG)
            m = jnp.max(st, axis=0, keepdims=True)
            p = jnp.exp2(st - m)
            l = jnp.sum(p, axis=0, keepdims=True)
            _flash_out(_dot(vt, p.astype(BF16)) / l, s, o_ref.at[TQ * u:TQ * (u + 1), :])


WIN_TILES = 2


def _win(q, k, vt, nb, t):
    nt = t // (TQ * WIN_TILES)
    assert t >= WINDOW + TQ and t % (TQ * WIN_TILES) == 0
    return pl.pallas_call(
        _win_body, name="flash_win", grid=(nb, nt),
        in_specs=[pl.BlockSpec((TQ * WIN_TILES, 1024), lambda bi, ti: (bi * nt + ti, 0)),
                  pl.BlockSpec((None, t, LANES), lambda bi, ti: (bi, 0, 0)),
                  pl.BlockSpec((None, LANES, t), lambda bi, ti: (bi, 0, 0))],
        out_specs=pl.BlockSpec((TQ * WIN_TILES, 512), lambda bi, ti: (bi * nt + ti, 0)),
        out_shape=jax.ShapeDtypeStruct((nb * t, 512), F32),
        compiler_params=_cparams(("parallel", "parallel")),
    )(q, k.reshape(nb, t, LANES), vt)


def _outproj_body(x_ref, yc_ref, oc_ref, os_ref, ow_ref, gt_ref, od_ref, e_ref, w_ref, g_ref, b_ref, o_ref, *, alpha):
    gt = gt_ref[...]
    hi = gt.astype(BF16)
    lo = (gt - hi.astype(F32)).astype(BF16)
    e = e_ref[...]
    gx = _dot(hi, e) + _dot(lo, e)
    onsa = gx[:, 0:512] * oc_ref[...] + gx[:, 512:1024] * os_ref[...] + gx[:, 1024:1536] * ow_ref[...]
    mix = (_dot(yc_ref[...].astype(BF16), w_ref[0:256, :]) + _dot(onsa.astype(BF16), w_ref[256:768, :])
           + _dot(od_ref[...].astype(BF16), w_ref[768:1024, :]))
    o_ref[...] = _ln_rows(alpha * x_ref[...] + mix, g_ref[...], b_ref[...])


def _outproj(x, yc, oc, osel, ow, gt, od, e, w, g, b, alpha, tm):
    n, d = x.shape
    row = lambda c: pl.BlockSpec((tm, c), lambda i: (i, 0))
    const = lambda a: pl.BlockSpec(a.shape, lambda i: (0, 0))
    return pl.pallas_call(
        functools.partial(_outproj_body, alpha=alpha), name="outproj",
        grid=(n // tm,),
        in_specs=[row(d), row(256), row(512), row(512), row(512), row(128), row(256), const(e), const(w), const(g), const(b)],
        out_specs=row(d), out_shape=jax.ShapeDtypeStruct((n, d), F32),
        compiler_params=_cparams(("parallel",)),
    )(x, yc, oc, osel, ow, gt, od, e, w, g, b)


def _ffn_body(x_ref, w1_ref, w3_ref, w2_ref, g_ref, b_ref, o_ref, acc_ref, *, alpha):
    f = pl.program_id(1)
    xb = x_ref[...].astype(BF16)
    h1 = _dot(xb, w1_ref[...])
    h = h1 * _sigmoid(h1) * _dot(xb, w3_ref[...])
    part = _dot(h.astype(BF16), w2_ref[...])

    @pl.when(f == 0)
    def _():
        acc_ref[...] = part

    @pl.when(f != 0)
    def _():
        acc_ref[...] += part

    @pl.when(f == pl.num_programs(1) - 1)
    def _():
        o_ref[...] = _ln_rows(alpha * x_ref[...] + acc_ref[...], g_ref[...], b_ref[...])


def _ffn(x, w1, w3, w2, g, b, alpha, tm, nf):
    n, d = x.shape
    dff = w1.shape[1]
    tf = dff // nf
    return pl.pallas_call(
        functools.partial(_ffn_body, alpha=alpha), name="ffn",
        grid=(n // tm, nf),
        in_specs=[pl.BlockSpec((tm, d), lambda i, f: (i, 0)),
                  pl.BlockSpec((d, tf), lambda i, f: (0, f)),
                  pl.BlockSpec((d, tf), lambda i, f: (0, f)),
                  pl.BlockSpec((tf, d), lambda i, f: (f, 0)),
                  pl.BlockSpec((1, d), lambda i, f: (0, 0)),
                  pl.BlockSpec((1, d), lambda i, f: (0, 0))],
        out_specs=pl.BlockSpec((tm, d), lambda i, f: (i, 0)),
        out_shape=jax.ShapeDtypeStruct((n, d), F32),
        scratch_shapes=[pltpu.VMEM((tm, d), F32)],
        compiler_params=_cparams(("parallel", "arbitrary")),
    )(x, w1, w3, w2, g, b)


def _gather_pages(pt_ref, n_pages, srcs, sem, layer):
    b = pl.program_id(0)
    slot = b & 1

    def copies(bb, sl):
        return [pltpu.make_async_copy(hbm.at[layer, pt_ref[bb, j]], buf.at[sl, :, pl.ds(PAGE * j, PAGE)], sem.at[a, sl])
                for a, (hbm, buf) in enumerate(srcs) for j in range(n_pages)]

    @pl.when(b == 0)
    def _():
        for c in copies(0, 0):
            c.start()

    @pl.when(b + 1 < pl.num_programs(0))
    def _():
        for c in copies(b + 1, 1 - slot):
            c.start()

    for c in copies(b, slot):
        c.wait()
    return slot


def _key_chunks(n):
    per = max(d for d in range(1, 17) if n % d == 0)
    return [(PAGE * per * c, PAGE * per) for c in range(n // per)]


def _stack_heads(q):
    return jnp.concatenate([q[:, LANES * h:LANES * (h + 1)] for h in range(8)], axis=0)


def _scmp_body(pt_ref, qu_ref, ck_hbm, cv_hbm, pak_ref, pbk_ref, wak_ref, wbk_ref, b1k_ref, w2k_ref,
               pav_ref, pbv_ref, wav_ref, wbv_ref, b1v_ref, w2v_ref, m_ref,
               o_ref, sel_ref, kbt, vbt, kbuf, vbuf, sem, *, layer, n_pages, past, nq):
    slot = _gather_pages(pt_ref, n_pages, [(ck_hbm, kbt), (cv_hbm, vbt)], sem, layer)
    for j in range(n_pages):
        kbuf[PAGE * j:PAGE * (j + 1), :] = kbt[slot, :, PAGE * j:PAGE * (j + 1)].T
        vbuf[PAGE * j:PAGE * (j + 1), :] = vbt[slot, :, PAGE * j:PAGE * (j + 1)].T
    n16 = past // CMP_STRIDE
    kcc = _compress(kbuf, n16, pak_ref[...], pbk_ref[...], wak_ref, wbk_ref, b1k_ref[...], w2k_ref).astype(BF16)
    vcc = _compress(vbuf, n16, pav_ref[...], pbv_ref[...], wav_ref, wbv_ref, b1v_ref[...], w2v_ref).astype(BF16)
    qa = _stack_heads(qu_ref[...])
    rows = 8 * nq
    s = _dot_nt(qa, kcc)
    cidx = lax.broadcasted_iota(jnp.int32, (rows, n16), 1)
    qpos = past + (lax.broadcasted_iota(jnp.int32, (rows, n16), 0) & (nq - 1))
    vis = (CMP_STRIDE * cidx + CMP_LEN - 1 <= qpos) & (cidx < n16 - 1)
    s = jnp.where(vis, s, NEG)
    m = jnp.max(s, axis=-1, keepdims=True)
    p = jnp.where(vis, jnp.exp2(s - m), 0.0)
    l = jnp.sum(p, axis=-1, keepdims=True)
    p = p / jnp.maximum(l, 1e-30)
    o_ref[...] = _dot(p.astype(BF16), vcc)
    imp = jnp.concatenate(
        [p[nq * 4 * g:nq * (4 * g + 1)] + p[nq * (4 * g + 1):nq * (4 * g + 2)]
         + p[nq * (4 * g + 2):nq * (4 * g + 3)] + p[nq * (4 * g + 3):nq * (4 * g + 4)] for g in range(NSA_G)], axis=0)
    isel = _split3_dot(imp, m_ref[...])
    nselp = isel.shape[1]
    jidx = lax.broadcasted_iota(jnp.int32, (NSA_G * nq, nselp), 1)
    qp = past + (lax.broadcasted_iota(jnp.int32, (NSA_G * nq, nselp), 0) & (nq - 1))
    cur = qp >> 6
    forced = (jidx == 0) | (jidx == cur) | (jidx == cur - 1)
    score = jnp.where(forced, FORCE_SCORE, isel)
    sel_ref[...] = jnp.where(jidx <= cur, score, -jnp.inf)


def _stopk_body(score_ref, sel_ref, *, k, past, nq):
    score = score_ref[...]
    nselp = score.shape[1]
    jidx = lax.broadcasted_iota(jnp.int32, score.shape, 1)
    cur = (past + (lax.broadcasted_iota(jnp.int32, score.shape, 0) & (nq - 1))) >> 6
    sel = jnp.zeros(score.shape, F32)
    for _ in range(k):
        mx = jnp.max(score, axis=-1, keepdims=True)
        first = jnp.min(jnp.where(score == mx, jidx, nselp), axis=-1, keepdims=True)
        hit = jidx == first
        sel = jnp.where(hit, 1.0, sel)
        score = jnp.where(hit, -jnp.inf, score)
    sel_ref[...] = jnp.where(jidx <= cur, sel, 0.0)


def _stopk(score, past, nq):
    nb, r, nselp = score.shape
    n_sel = -(-(past + nq) // SEL_BLK)
    full = pl.BlockSpec((nb * r, nselp), lambda i: (0, 0))
    sel = pl.pallas_call(
        functools.partial(_stopk_body, k=min(SEL_TOPN, n_sel), past=past, nq=nq), name="stopk",
        grid=(1,), in_specs=[full], out_specs=full,
        out_shape=jax.ShapeDtypeStruct((nb * r, nselp), F32),
        compiler_params=_cparams(("arbitrary",)),
    )(score.reshape(nb * r, nselp))
    return sel.reshape(nb, r, nselp)


def _scmp(pt, qu, cache_k, cache_v, cw, layer, nb, nq, past):
    n_pages = past // PAGE
    n16 = past // CMP_STRIDE
    n_sel = -(-(past + nq) // SEL_BLK)
    nselp = -(-n_sel // LANES) * LANES
    m = np.zeros((n16, nselp), np.float32)
    for j in range(n_sel):
        for c in range(4 * j - 1, 4 * j + 4):
            if 0 <= c < n16 - 1:
                m[c, j] = 1.0
    m = jnp.asarray(m, BF16)
    wlist = [cw[k] for k in ("pak", "pbk", "wak", "wbk", "b1k", "w2k", "pav", "pbv", "wav", "wbv", "b1v", "w2v")] + [m]
    const = lambda a: pl.BlockSpec(a.shape, lambda bi, pt_: (0,) * a.ndim)
    any_spec = pl.BlockSpec(memory_space=pl.ANY)
    grid_spec = pltpu.PrefetchScalarGridSpec(
        num_scalar_prefetch=1, grid=(nb,),
        in_specs=[pl.BlockSpec((nq, 1024), lambda bi, pt_: (bi, 0)), any_spec, any_spec] + [const(a) for a in wlist],
        out_specs=[pl.BlockSpec((None, 8 * nq, LANES), lambda bi, pt_: (bi, 0, 0)),
                   pl.BlockSpec((None, NSA_G * nq, nselp), lambda bi, pt_: (bi, 0, 0))],
        scratch_shapes=[pltpu.VMEM((2, LANES, past), F32), pltpu.VMEM((2, LANES, past), F32),
                        pltpu.VMEM((past, LANES), F32), pltpu.VMEM((past, LANES), F32), pltpu.SemaphoreType.DMA((2, 2))])
    return pl.pallas_call(
        functools.partial(_scmp_body, layer=layer, n_pages=n_pages, past=past, nq=nq), name="scmp",
        grid_spec=grid_spec,
        out_shape=[jax.ShapeDtypeStruct((nb, 8 * nq, LANES), F32), jax.ShapeDtypeStruct((nb, NSA_G * nq, nselp), F32)],
        compiler_params=_cparams(("arbitrary",)),
    )(pt, qu, cache_k, cache_v, *wlist)


def _softmax_rows(s):
    m = jnp.max(s, axis=-1, keepdims=True)
    p = jnp.exp2(s - m)
    return p, jnp.sum(p, axis=-1, keepdims=True)


def _sselwin_body(pt_ref, qr_ref, sel_ref, skn_ref, svn_ref, wkn_ref, wvn_ref, wks_ref, wvs_ref, e_ref, sk_hbm, sv_hbm,
                  osel_ref, owin_ref, kbuf, vbuf, sem, *, layer, n_pages, past, nq):
    ktot = past + PAGE
    rows = 8 * nq
    slot = _gather_pages(pt_ref, n_pages, [(sk_hbm, kbuf), (sv_hbm, vbuf)], sem, layer)
    pad = jnp.zeros((PAGE - nq, LANES), F32)
    kbuf[slot, :, pl.ds(past, PAGE)] = jnp.concatenate([skn_ref[...], pad], axis=0).T
    vbuf[slot, :, pl.ds(past, PAGE)] = jnp.concatenate([svn_ref[...], pad], axis=0).T
    chunks = _key_chunks(n_pages + 1)
    qa = _stack_heads(qr_ref[...])
    s = jnp.concatenate([_dot(qa, kbuf[slot, :, pl.ds(c0, cn)].astype(BF16)) for c0, cn in chunks], axis=1)
    se = _dot(sel_ref[...].astype(BF16), e_ref[...])
    se = jnp.concatenate([se[nq * g:nq * (g + 1)] for g in range(NSA_G) for _ in range(NSA_HPG)], axis=0)
    kpos = lax.broadcasted_iota(jnp.int32, (rows, ktot), 1)
    qpos = past + (lax.broadcasted_iota(jnp.int32, (rows, ktot), 0) & (nq - 1))
    s = jnp.where((se > 0.5) & (kpos <= qpos), s, NEG)
    p, l = _softmax_rows(s)
    pb = p.astype(BF16)
    o = _dot_nt(pb[:, 0:chunks[0][1]], vbuf[slot, :, pl.ds(0, chunks[0][1])].astype(BF16))
    for c0, cn in chunks[1:]:
        o = o + _dot_nt(pb[:, c0:c0 + cn], vbuf[slot, :, pl.ds(c0, cn)].astype(BF16))
    osel_ref[...] = o / l
    wb = wks_ref.shape[0]
    kw = jnp.concatenate([wks_ref[...], wkn_ref[...], pad], axis=0).astype(BF16)
    vw = jnp.concatenate([wvs_ref[...], wvn_ref[...], pad], axis=0).astype(BF16)
    s = _dot_nt(qa, kw)
    i = lax.broadcasted_iota(jnp.int32, (rows, wb + PAGE), 1)
    kp = jnp.where(i < wb, past - wb + i, past + i - wb)
    qp = past + (lax.broadcasted_iota(jnp.int32, (rows, wb + PAGE), 0) & (nq - 1))
    dlt = qp - kp
    s = jnp.where((dlt >= 0) & (dlt < WINDOW) & (i < wb + nq), s, NEG)
    p, l = _softmax_rows(s)
    owin_ref[...] = _dot(p.astype(BF16), vw) / l


def _sselwin(pt, qr, sel, skn, svn, wkn, wvn, wks, wvs, cache_k, cache_v, layer, nb, nq, past):
    n_pages = past // PAGE
    ktot = past + PAGE
    nselp = sel.shape[-1]
    e = (np.arange(ktot)[None, :] // SEL_BLK == np.arange(nselp)[:, None]).astype(np.float32)
    e = jnp.asarray(e, BF16)
    wb = wks.shape[1]
    any_spec = pl.BlockSpec(memory_space=pl.ANY)
    new = pl.BlockSpec((nq, LANES), lambda bi, pt_: (bi, 0))
    grid_spec = pltpu.PrefetchScalarGridSpec(
        num_scalar_prefetch=1, grid=(nb,),
        in_specs=[pl.BlockSpec((nq, 1024), lambda bi, pt_: (bi, 0)),
                  pl.BlockSpec((None, NSA_G * nq, nselp), lambda bi, pt_: (bi, 0, 0)),
                  new, new, new, new,
                  pl.BlockSpec((None, wb, LANES), lambda bi, pt_: (bi, 0, 0)),
                  pl.BlockSpec((None, wb, LANES), lambda bi, pt_: (bi, 0, 0)),
                  pl.BlockSpec(e.shape, lambda bi, pt_: (0, 0)), any_spec, any_spec],
        out_specs=[pl.BlockSpec((None, 8 * nq, LANES), lambda bi, pt_: (bi, 0, 0)),
                   pl.BlockSpec((None, 8 * nq, LANES), lambda bi, pt_: (bi, 0, 0))],
        scratch_shapes=[pltpu.VMEM((2, LANES, ktot), F32), pltpu.VMEM((2, LANES, ktot), F32), pltpu.SemaphoreType.DMA((2, 2))])
    return pl.pallas_call(
        functools.partial(_sselwin_body, layer=layer, n_pages=n_pages, past=past, nq=nq), name="sselwin",
        grid_spec=grid_spec,
        out_shape=[jax.ShapeDtypeStruct((nb, 8 * nq, LANES), F32)] * 2,
        compiler_params=_cparams(("arbitrary",)),
    )(pt, qr, sel, skn, svn, wkn, wvn, wks, wvs, e, cache_k, cache_v)


def _sdiff_body(pt_ref, dq_ref, dkn_ref, dvn_ref, lq1_ref, lk1_ref, lq2_ref, lk2_ref, sg_ref, dk_hbm, dv_hbm,
                o_ref, kbuf, vbuf, sem, *, layer, n_pages, past, nq, lam_init):
    ktot = past + PAGE
    rows = 8 * nq
    slot = _gather_pages(pt_ref, n_pages, [(dk_hbm, kbuf), (dv_hbm, vbuf)], sem, layer)
    pad = jnp.zeros((PAGE - nq, 256), F32)
    kbuf[slot, :, pl.ds(past, PAGE)] = jnp.concatenate([dkn_ref[...], pad], axis=0).T
    vbuf[slot, :, pl.ds(past, PAGE)] = jnp.concatenate([dvn_ref[...], pad], axis=0).T
    chunks = _key_chunks(n_pages + 1)
    q = dq_ref[...]
    zero = jnp.zeros((nq, LANES), BF16)
    blocks = []
    for a in range(8):
        chunk = q[:, LANES * a:LANES * (a + 1)]
        blocks.append(jnp.concatenate([chunk, zero] if a < 4 else [zero, chunk], axis=1))
    qa = jnp.concatenate(blocks, axis=0)
    s = jnp.concatenate([_dot(qa, kbuf[slot, :, pl.ds(c0, cn)].astype(BF16)) for c0, cn in chunks], axis=1)
    kpos = lax.broadcasted_iota(jnp.int32, (rows, ktot), 1)
    qpos = past + (lax.broadcasted_iota(jnp.int32, (rows, ktot), 0) & (nq - 1))
    s = jnp.where(kpos <= qpos, s, NEG)
    p, l = _softmax_rows(s)
    pb = p.astype(BF16)
    o = _dot_nt(pb[:, 0:chunks[0][1]], vbuf[slot, :, pl.ds(0, chunks[0][1])].astype(BF16))
    for c0, cn in chunks[1:]:
        o = o + _dot_nt(pb[:, c0:c0 + cn], vbuf[slot, :, pl.ds(c0, cn)].astype(BF16))
    o = o / l
    lam = (jnp.exp(jnp.sum(lq1_ref[...] * lk1_ref[...], keepdims=True))
           - jnp.exp(jnp.sum(lq2_ref[...] * lk2_ref[...], keepdims=True)) + lam_init)
    lane = lax.broadcasted_iota(jnp.int32, (nq, 256), 1)
    out = jnp.zeros((nq, 256), F32)
    for h in range(DIFF_H):
        d = o[2 * nq * h:2 * nq * h + nq] - lam * o[2 * nq * h + nq:2 * nq * (h + 1)]
        inh = (lane >> 6) == h
        ms = jnp.sum(jnp.where(inh, d * d, 0.0), axis=-1, keepdims=True) * (1.0 / DIFF_DV)
        out = out + jnp.where(inh, d * lax.rsqrt(ms + LN_EPS), 0.0)
    o_ref[...] = out * sg_ref[...] * (1.0 - lam_init)


def _sdiff(pt, dq, dkn, dvn, lams, sg_row, cache_k, cache_v, layer, nb, nq, past, lam_init):
    n_pages = past // PAGE
    ktot = past + PAGE
    any_spec = pl.BlockSpec(memory_space=pl.ANY)
    new = pl.BlockSpec((nq, 256), lambda bi, pt_: (bi, 0))
    small = [pl.BlockSpec(a.shape, lambda bi, pt_: (0, 0)) for a in (*lams, sg_row)]
    grid_spec = pltpu.PrefetchScalarGridSpec(
        num_scalar_prefetch=1, grid=(nb,),
        in_specs=[pl.BlockSpec((nq, 1024), lambda bi, pt_: (bi, 0)), new, new] + small + [any_spec, any_spec],
        out_specs=pl.BlockSpec((nq, 256), lambda bi, pt_: (bi, 0)),
        scratch_shapes=[pltpu.VMEM((2, 256, ktot), F32), pltpu.VMEM((2, 256, ktot), F32), pltpu.SemaphoreType.DMA((2, 2))])
    return pl.pallas_call(
        functools.partial(_sdiff_body, layer=layer, n_pages=n_pages, past=past, nq=nq, lam_init=lam_init), name="sdiff",
        grid_spec=grid_spec,
        out_shape=jax.ShapeDtypeStruct((nb * nq, 256), F32),
        compiler_params=_cparams(("arbitrary",)),
    )(pt, dq, dkn, dvn, *lams, sg_row, cache_k, cache_v)


def _prep_w_in(w):
    pts = np.cumsum([256, 256, 512, 128, 128, 128, 128, 128, 128, 24, 256, 256, 256])[:-1].tolist()
    ca, cg, nq, ck, cv, sk, sv, wk, wv, gt, dq, dk, dv = jnp.split(w, pts, axis=1)
    gtp = jnp.pad(gt, ((0, 0), (0, LANES - gt.shape[1])))
    return jnp.concatenate([ca, cg, nq * (NSA_DH ** -0.5 * LOG2E), ck, cv, sk, sv, wk, wv, gtp, dq, dk, dv], axis=1).astype(BF16)


def _rope_tables(pos):
    out = []
    lane = np.arange(LANES)
    for half in (32, 16):
        inv = ROPE_THETA ** (-jnp.arange(half, dtype=F32) / half)
        ang = pos.astype(F32)[:, None] * inv[None, :]
        idx = lane % half
        sign = jnp.asarray(np.where(lane % (2 * half) < half, -1.0, 1.0), F32)
        out += [jnp.cos(ang)[:, idx], jnp.sin(ang)[:, idx] * sign[None, :]]
    return out


def _prep_cmp(pe, w1, b1, w2):
    def halves(x):
        res = []
        for part in (x[:16 * NSA_DH], x[16 * NSA_DH:]):
            p4 = part.reshape(CMP_STRIDE, 1, NSA_DH, 1, -1)
            eye = jnp.eye(NSA_G, dtype=F32)[None, :, None, :, None]
            res.append((p4 * eye).reshape(CMP_STRIDE * NSA_G * NSA_DH, NSA_G * part.shape[-1]))
        return res
    wa, wb = halves(w1)
    pea = jnp.tile(pe[:16, None, :], (1, NSA_G, 1)).reshape(1, -1)
    peb = jnp.tile(pe[16:, None, :], (1, NSA_G, 1)).reshape(1, -1)
    w2bd = (w2[None, :, None, :] * jnp.eye(NSA_G, dtype=F32)[:, None, :, None]).reshape(NSA_G * CMP_HID, NSA_G * NSA_DH)
    return pea, peb, wa.astype(BF16), wb.astype(BF16), jnp.tile(b1, NSA_G)[None, :], w2bd.astype(BF16)


def _gate_expand():
    e = np.zeros((LANES, 3 * 512), np.float32)
    for h in range(NSA_H):
        for br in range(3):
            e[3 * h + br, 512 * br + 64 * h:512 * br + 64 * (h + 1)] = 1.0
    return jnp.asarray(e, BF16)


def _rows_to_tokens(o, nb, nq):
    o6 = o.reshape(nb, NSA_G, NSA_HPG, nq, NSA_G, NSA_DH)
    pick = jnp.stack([o6[:, g, :, :, g, :] for g in range(NSA_G)], axis=1)
    return pick.transpose(0, 3, 1, 2, 4).reshape(nb * nq, NSA_H * NSA_DH)


def _prompt_layer(x, lw, tabs, nb, t, lam_init, alpha):
    n = nb * t
    (u, qu, qr, gt, dq, ck, cv, skb, wkb, dkb, svb, wvb, dvb,
     ckt, cvt, skt, svt, wkt, wvt, dkt, dvt) = _inproj(x, lw["w_in"], tabs, nb, t, 256, True)
    ext = jnp.pad(u.reshape(nb, t, 256), ((0, 0), (CONV_PAD, 0), (0, 0)))
    yc = _conv(ext, lw["dw_w"], lw["dw_b"], lw["cln_g"], lw["cln_b"], t, 256).reshape(n, 256)
    ocmp, sel = _pcmp(qu, ck, cv, lw["cmp"], nb, t)
    osel = _flash("sel", qr, skb, svb, nb, t, extra=(sel,))
    owin = _win(qr, wkb, wvb, nb, t)
    odiff = _flash("diff", dq, dkb, dvb, nb, t, extra=lw["lams"] + (lw["sg_col"],), lam_init=lam_init)
    x1 = _outproj(x, yc, ocmp, osel, owin, gt, odiff, lw["gate_e"], lw["w_out"], lw["ln1_g"], lw["ln1_b"], alpha, 256)
    x2 = _ffn(x1, lw["w1"], lw["w3"], lw["w2"], lw["ln2_g"], lw["ln2_b"], alpha, 512, 2)
    nk = min(WINDOW, t)
    rows_major = lambda a: a.reshape(nb, a.shape[1] // 64, 64, a.shape[2]).transpose(0, 3, 1, 2)
    news = (rows_major(ckt), rows_major(cvt), rows_major(skt), rows_major(svt), rows_major(dkt), rows_major(dvt),
            rows_major(wkt[:, :, t - nk:]), rows_major(wvt[:, :, t - nk:]),
            u.reshape(nb, t, 256)[:, t - (CONV_W - 1):])
    return x2, news


def _sample_layer(x, lw, tabs, caches, states, pt, layer, nb, nq, past, lam_init, alpha):
    n = nb * nq
    (u, qu, qr, gt, dq, ck, cv, sk, sv, wk, wv, dk, dv) = _inproj(x, lw["w_in"], tabs, nb, nq, n, False)
    c_cmp_k, c_cmp_v, c_sel_k, c_sel_v, c_diff_k, c_diff_v = caches
    st_wk, st_wv, st_conv = states
    ext = jnp.concatenate([jnp.zeros((nb, CONV_PAD - (CONV_W - 1), 256), F32), st_conv, u.reshape(nb, nq, 256)], axis=1)
    yc = _conv(ext, lw["dw_w"], lw["dw_b"], lw["cln_g"], lw["cln_b"], nq, nq).reshape(n, 256)
    ocmp, score = _scmp(pt, qu, c_cmp_k, c_cmp_v, lw["cmp"], layer, nb, nq, past)
    sel = _stopk(score, past, nq)
    wb = st_wk.shape[1]
    osel, owin = _sselwin(pt, qr, sel, sk, sv, wk, wv, st_wk.reshape(nb, wb, LANES), st_wv.reshape(nb, wb, LANES),
                          c_sel_k, c_sel_v, layer, nb, nq, past)
    odiff = _sdiff(pt, dq, dk, dv, lw["lams"], lw["sg_row"], c_diff_k, c_diff_v, layer, nb, nq, past, lam_init)
    x1 = _outproj(x, yc, _rows_to_tokens(ocmp, nb, nq), _rows_to_tokens(osel, nb, nq), _rows_to_tokens(owin, nb, nq),
                  gt, odiff, lw["gate_e"], lw["w_out"], lw["ln1_g"], lw["ln1_b"], alpha, n)
    x2 = _ffn(x1, lw["w1"], lw["w3"], lw["w2"], lw["ln2_g"], lw["ln2_b"], alpha, n, 2)
    new_wk = jnp.concatenate([st_wk, wk.reshape(nb, nq, NSA_G, NSA_DH)], axis=1)[:, -wb:]
    new_wv = jnp.concatenate([st_wv, wv.reshape(nb, nq, NSA_G, NSA_DH)], axis=1)[:, -wb:]
    new_conv = jnp.concatenate([st_conv, u.reshape(nb, nq, 256)], axis=1)[:, -(CONV_W - 1):]
    news = (ck.reshape(nb, nq, NSA_G, NSA_DH), cv.reshape(nb, nq, NSA_G, NSA_DH),
            sk.reshape(nb, nq, NSA_G, NSA_DH), sv.reshape(nb, nq, NSA_G, NSA_DH),
            dk.reshape(nb, nq, DIFF_H, 2 * DIFF_DQK), dv.reshape(nb, nq, DIFF_H, DIFF_DV),
            new_wk, new_wv, new_conv)
    return x2, news


def kernel(x_prompt, x_sample, cache_nsa_cmp_k, cache_nsa_cmp_v, cache_nsa_sel_k, cache_nsa_sel_v, cache_diff_k, cache_diff_v, state_nsa_win_k, state_nsa_win_v, state_conv, page_table, w_in, conv_dw_w, conv_dw_b, conv_ln_g, conv_ln_b, cmp_pe_k, cmp_w1_k, cmp_b1_k, cmp_w2_k, cmp_pe_v, cmp_w1_v, cmp_b1_v, cmp_w2_v, diff_lq1, diff_lk1, diff_lq2, diff_lk2, diff_subln_g, w_out, ln1_g, ln1_b, ln2_g, ln2_b, ffn_w1, ffn_w3, ffn_w2):
    nb, t, d = x_prompt.shape
    sb, nq, _ = x_sample.shape
    depth = w_in.shape[0]
    n_pool = cache_nsa_cmp_k.shape[1]
    past = page_table.shape[1] * PAGE
    alpha = (2 * depth) ** 0.25
    tabs_p = _rope_tables(jnp.arange(t, dtype=jnp.int32))
    tabs_s = _rope_tables(jnp.tile(past + jnp.arange(nq, dtype=jnp.int32), sb))
    as_pages = lambda c: c.transpose(0, 1, 3, 4, 2).reshape(depth, n_pool, c.shape[3] * c.shape[4], PAGE)
    caches = tuple(as_pages(c) for c in (cache_nsa_cmp_k, cache_nsa_cmp_v, cache_nsa_sel_k, cache_nsa_sel_v,
                                         cache_diff_k, cache_diff_v))
    gate_e = _gate_expand()
    xp = x_prompt.reshape(nb * t, d)
    xs = x_sample.reshape(sb * nq, d)
    outs_p, outs_s = [], []
    for l in range(depth):
        ck = _prep_cmp(cmp_pe_k[l], cmp_w1_k[l], cmp_b1_k[l], cmp_w2_k[l])
        cv = _prep_cmp(cmp_pe_v[l], cmp_w1_v[l], cmp_b1_v[l], cmp_w2_v[l])
        names = ("pa", "pb", "wa", "wb", "b1", "w2")
        cmpw = {n_ + "k": a for n_, a in zip(names, ck)}
        cmpw.update({n_ + "v": a for n_, a in zip(names, cv)})
        lw = dict(
            w_in=_prep_w_in(w_in[l]), dw_w=conv_dw_w[l], dw_b=conv_dw_b[l][None], cln_g=conv_ln_g[l][None],
            cln_b=conv_ln_b[l][None], cmp=cmpw,
            lams=(diff_lq1[l][None], diff_lk1[l][None], diff_lq2[l][None], diff_lk2[l][None]),
            sg_col=diff_subln_g[l][:, None], sg_row=jnp.tile(diff_subln_g[l], DIFF_H)[None],
            gate_e=gate_e, w_out=w_out[l].astype(BF16), ln1_g=ln1_g[l][None], ln1_b=ln1_b[l][None],
            ln2_g=ln2_g[l][None], ln2_b=ln2_b[l][None],
            w1=ffn_w1[l].astype(BF16), w3=ffn_w3[l].astype(BF16), w2=ffn_w2[l].astype(BF16))
        lam_init = 0.8 - 0.6 * math.exp(-0.3 * l)
        xp, new_p = _prompt_layer(xp, lw, tabs_p, nb, t, lam_init, alpha)
        xs, new_s = _sample_layer(xs, lw, tabs_s, caches, (state_nsa_win_k[l], state_nsa_win_v[l], state_conv[l]),
                                  page_table, l, sb, nq, past, lam_init, alpha)
        outs_p.append(new_p)
        outs_s.append(new_s)
    stk_p = [jnp.stack([o[i] for o in outs_p]) for i in range(9)]
    stk_s = [jnp.stack([o[i] for o in outs_s]) for i in range(9)]
    return (xp.reshape(nb, t, d), xs.reshape(sb, nq, d), *stk_p, *stk_s)
```

```python
import functools
import math

import jax
import jax.numpy as jnp
import numpy as np
from jax import lax
from jax.experimental import pallas as pl
from jax.experimental.pallas import tpu as pltpu

F32 = jnp.float32
BF16 = jnp.bfloat16

CONV_W = 31
NSA_H = 8
NSA_G = 2
NSA_HPG = NSA_H // NSA_G
NSA_DH = 64
CMP_STRIDE = 16
CMP_LEN = 32
CMP_HID = 128
SEL_BLK = 64
SEL_TOPN = 16
WINDOW = 512
FORCE_SCORE = 1e9
DIFF_H = 4
DIFF_DV = 64
DIFF_DQK = 32
ROPE_THETA = 10000.0
LN_EPS = 1e-5
PAGE = 128

LANES = 128
TQ = 128
TK = 256
LOG2E = 1.4426950408889634
NEG = -1e30
VMEM_LIMIT = 52 * 1024 * 1024

C_CA, C_CG, C_Q, C_KV, C_GT, C_DQ, C_DK, C_DV, C_END = 0, 256, 512, 1024, 1792, 1920, 2176, 2432, 2688


def _cparams(sem):
    return pltpu.CompilerParams(dimension_semantics=sem, vmem_limit_bytes=VMEM_LIMIT)


def _sigmoid(x):
    return 1.0 / (1.0 + jnp.exp(-x))


def _ln_rows(x, g, b):
    mu = jnp.mean(x, axis=-1, keepdims=True)
    xc = x - mu
    var = jnp.mean(xc * xc, axis=-1, keepdims=True)
    return xc * lax.rsqrt(var + LN_EPS) * g + b


def _dot(a, b):
    return jnp.dot(a, b, preferred_element_type=F32)


def _dot_nt(a, b):
    return lax.dot_general(a, b, (((1,), (1,)), ((), ())), preferred_element_type=F32)


def _split3_dot(x, m):
    hi = x.astype(BF16)
    r1 = x - hi.astype(F32)
    mid = r1.astype(BF16)
    lo = (r1 - mid.astype(F32)).astype(BF16)
    return _dot(hi, m) + _dot(mid, m) + _dot(lo, m)


def _rope(x, cos, sin_signed, half):
    lane = lax.broadcasted_iota(jnp.int32, x.shape, 1)
    first = (lane & (2 * half - 1)) < half
    rot = jnp.where(first, pltpu.roll(x, LANES - half, 1), pltpu.roll(x, half, 1))
    return x * cos + rot * sin_signed


def _inproj_body(x_ref, w_ref, c64_ref, s64_ref, c32_ref, s32_ref,
                 u_ref, qu_ref, qr_ref, gt_ref, dq_ref, ck_ref, cv_ref, *refs, dq_scale, transposed):
    xb = x_ref[...].astype(BF16)

    def mm(lo, hi):
        return _dot(xb, w_ref[:, lo:hi])

    c64, s64, c32, s32 = c64_ref[...], s64_ref[...], c32_ref[...], s32_ref[...]
    z = mm(C_CA, C_Q)
    u_ref[...] = z[:, :256] * _sigmoid(z[:, 256:])
    lane = lax.broadcasted_iota(jnp.int32, (x_ref.shape[0], LANES), 1)
    for j in range(4):
        zq = mm(C_Q + LANES * j, C_Q + LANES * (j + 1))
        g = j // 2
        keep = (lane < 64) if g == 0 else (lane >= 64)
        for src, dst_ref in ((zq, qu_ref), (_rope(zq, c64, s64, 32), qr_ref)):
            for e in range(2):
                v = src if e == g else pltpu.roll(src, 64, 1)
                dst_ref[:, LANES * (2 * j + e):LANES * (2 * j + e + 1)] = jnp.where(keep, v, 0.0).astype(BF16)
    z = mm(C_KV, C_GT)
    ck = z[:, 0:128]
    cv = z[:, 128:256]
    ck_ref[...] = ck
    cv_ref[...] = cv
    sk = _rope(z[:, 256:384], c64, s64, 32)
    sv = z[:, 384:512]
    wk = _rope(z[:, 512:640], c64, s64, 32)
    wv = z[:, 640:768]
    gt_ref[...] = _sigmoid(mm(C_GT, C_DQ))
    for c in range(2):
        zq = _rope(mm(C_DQ + LANES * c, C_DQ + LANES * (c + 1)), c32, s32, 16) * dq_scale
        for a in range(4):
            dq_ref[:, LANES * (4 * c + a):LANES * (4 * c + a + 1)] = jnp.where((lane >> 5) == a, zq, 0.0).astype(BF16)
    z = mm(C_DK, C_DV)
    dk = [_rope(z[:, :128], c32, s32, 16), _rope(z[:, 128:], c32, s32, 16)]
    z = mm(C_DV, C_END)
    dv = [z[:, :128], z[:, 128:]]
    if not transposed:
        sk_ref, sv_ref, wk_ref, wv_ref, dk_ref, dv_ref = refs
        sk_ref[...] = sk
        sv_ref[...] = sv
        wk_ref[...] = wk
        wv_ref[...] = wv
        for c in range(2):
            dk_ref[:, LANES * c:LANES * (c + 1)] = dk[c]
            dv_ref[:, LANES * c:LANES * (c + 1)] = dv[c]
        return
    (skb_ref, wkb_ref, dkb_ref, svb_ref, wvb_ref, dvb_ref,
     ckt_ref, cvt_ref, skt_ref, svt_ref, wkt_ref, wvt_ref, dkt_ref, dvt_ref) = refs
    skb_ref[...] = sk.astype(BF16)
    wkb_ref[...] = wk.astype(BF16)
    ckt_ref[...] = ck.T
    cvt_ref[...] = cv.T
    skt_ref[...] = sk.T
    wkt_ref[...] = wk.T
    svt = sv.T
    svt_ref[...] = svt
    svb_ref[...] = svt.astype(BF16)
    wvt = wv.T
    wvt_ref[...] = wvt
    wvb_ref[...] = wvt.astype(BF16)
    for c in range(2):
        rows = slice(LANES * c, LANES * (c + 1))
        dkb_ref[:, rows] = dk[c].astype(BF16)
        dkt_ref[rows, :] = dk[c].T
        dvt = dv[c].T
        dvt_ref[rows, :] = dvt
        dvb_ref[rows, :] = dvt.astype(BF16)


def _inproj(x, w, tabs, nb, t, tm, transposed):
    n, d = x.shape
    nt = t // tm if transposed else 1
    grid = (n // tm,)
    row = lambda c: pl.BlockSpec((tm, c), lambda i: (i, 0))
    tab = pl.BlockSpec((tm, LANES), (lambda i: (i % nt, 0)) if transposed else (lambda i: (i, 0)))
    in_specs = [row(d), pl.BlockSpec((d, C_END), lambda i: (0, 0)), tab, tab, tab, tab]
    shapes = [(256, F32), (1024, BF16), (1024, BF16), (128, F32), (1024, BF16), (128, F32), (128, F32)]
    if transposed:
        shapes += [(128, BF16), (128, BF16), (256, BF16)]
    else:
        shapes += [(128, F32)] * 4 + [(256, F32)] * 2
    out_shape = [jax.ShapeDtypeStruct((n, c), dt) for c, dt in shapes]
    out_specs = [row(c) for c, _ in shapes]
    if transposed:
        for c, dt in [(128, BF16), (128, BF16), (256, BF16)] + [(128, F32)] * 6 + [(256, F32)] * 2:
            out_shape.append(jax.ShapeDtypeStruct((nb, c, t), dt))
            out_specs.append(pl.BlockSpec((None, c, tm), lambda i: (i // nt, 0, i % nt)))
    return pl.pallas_call(
        functools.partial(_inproj_body, dq_scale=DIFF_DQK ** -0.5 * LOG2E, transposed=transposed), name="inproj",
        grid=grid, in_specs=in_specs, out_specs=out_specs, out_shape=out_shape,
        compiler_params=_cparams(("parallel",)),
    )(x, w, *tabs)


CONV_PAD = 32


def _conv_body(ext_ref, w_ref, b_ref, g_ref, beta_ref, y_ref, acc_ref, sh_ref, *, tt, rs):
    t0 = pl.multiple_of(pl.program_id(1) * tt, 8)
    off = CONV_PAD - (CONV_W - 1)
    for c in range(2):
        cs = slice(LANES * c, LANES * (c + 1))
        for r in range(tt // rs):
            win = ext_ref[pl.ds(t0 + rs * r, rs + CONV_PAD), cs]
            acc = jnp.zeros((rs, LANES), F32)
            for r8 in range(8):
                taps = range(r8, CONV_W, 8)
                rows = 8 * (len(taps) - 1) + rs
                sh_ref[0:rows, :] = win[off + r8:off + r8 + rows, :]
                for a, k in enumerate(taps):
                    acc = acc + sh_ref[8 * a:8 * a + rs, :] * w_ref[k:k + 1, cs]
            acc_ref[rs * r:rs * (r + 1), cs] = acc
    y = _ln_rows(acc_ref[...] + b_ref[...], g_ref[...], beta_ref[...])
    y_ref[...] = y * _sigmoid(y)


def _conv(ext, w, b, g, beta, t, tt):
    nb, le, c = ext.shape
    rs = min(tt, 64)
    vec = pl.BlockSpec((1, c), lambda bi, ti: (0, 0))
    return pl.pallas_call(
        functools.partial(_conv_body, tt=tt, rs=rs), name="conv",
        grid=(nb, t // tt),
        in_specs=[pl.BlockSpec((None, le, c), lambda bi, ti: (bi, 0, 0)),
                  pl.BlockSpec((CONV_W, c), lambda bi, ti: (0, 0)), vec, vec, vec],
        out_specs=pl.BlockSpec((None, tt, c), lambda bi, ti: (bi, ti, 0)),
        out_shape=jax.ShapeDtypeStruct((nb, t, c), F32),
        scratch_shapes=[pltpu.VMEM((tt, c), F32), pltpu.VMEM((rs + CONV_PAD, LANES), F32)],
        compiler_params=_cparams(("parallel", "parallel")),
    )(ext, w, b, g, beta)


def _compress(src_ref, n16, pea, peb, wa_ref, wb_ref, b1, w2_ref):
    x = jnp.concatenate([src_ref[pl.ds(p, n16, stride=CMP_STRIDE), :] for p in range(CMP_STRIDE)], axis=1)
    a = _dot((x + pea).astype(BF16), wa_ref[...])
    bm = _dot((x + peb).astype(BF16), wb_ref[...])
    h = a + pltpu.roll(bm, n16 - 1, 0) + b1
    gl = 0.5 * h * (1.0 + jnp.tanh(0.7978845608028654 * (h + 0.044715 * (h * h * h))))
    return _dot(gl.astype(BF16), w2_ref[...])


def _top_rows(score, ridx, k):
    nv = score.shape[0] // 8
    tiles = [score[8 * v:8 * (v + 1), :] for v in range(nv)]
    cnts = [jnp.zeros(t.shape, F32) for t in tiles]
    for jp in range(score.shape[0]):
        row = score[jp:jp + 1, :]
        for v in range(nv):
            if 8 * v > jp:
                beats = row >= tiles[v]
            elif 8 * v + 7 < jp:
                beats = row > tiles[v]
            else:
                beats = (row > tiles[v]) | ((row == tiles[v]) & (ridx[8 * v:8 * (v + 1), :] > jp))
            cnts[v] = cnts[v] + jnp.where(beats, 1.0, 0.0)
    return jnp.concatenate(cnts, axis=0) < k


def _pcmp_body(qu_ref, ck_ref, cv_ref, pak_ref, pbk_ref, wak_ref, wbk_ref, b1k_ref, w2k_ref,
               pav_ref, pbv_ref, wav_ref, wbv_ref, b1v_ref, w2v_ref, mt_ref,
               o_ref, sel_ref, kcc_ref, vcct_ref, *, n16, nsel):
    ti = pl.program_id(1)

    @pl.when(ti == 0)
    def _():
        kcc = _compress(ck_ref, n16, pak_ref[...], pbk_ref[...], wak_ref, wbk_ref, b1k_ref[...], w2k_ref)
        kcc_ref[...] = kcc.astype(BF16)
        vcc = _compress(cv_ref, n16, pav_ref[...], pbv_ref[...], wav_ref, wbv_ref, b1v_ref[...], w2v_ref)
        vcct_ref[...] = vcc.T.astype(BF16)

    q = qu_ref[...]
    kcc = kcc_ref[...]
    vcct = vcct_ref[...]
    nq = 4 * TQ
    cidx = lax.broadcasted_iota(jnp.int32, (n16, nq), 0)
    qpos = ti * TQ + (lax.broadcasted_iota(jnp.int32, (n16, nq), 1) & (TQ - 1))
    vis = (CMP_STRIDE * cidx + CMP_LEN - 1 <= qpos) & (cidx < n16 - 1)
    jidx = lax.broadcasted_iota(jnp.int32, (nsel, TQ), 0)
    qp1 = ti * TQ + lax.broadcasted_iota(jnp.int32, (nsel, TQ), 1)
    cur = qp1 >> 6
    forced = (jidx == 0) | (jidx == cur) | (jidx == cur - 1)
    for g in range(NSA_G):
        qs = jnp.concatenate([q[:, LANES * (4 * g + a):LANES * (4 * g + a + 1)] for a in range(4)], axis=0)
        st = _dot_nt(kcc, qs)
        st = jnp.where(vis, st, NEG)
        m = jnp.max(st, axis=0, keepdims=True)
        p = jnp.exp2(st - m)
        l = jnp.sum(p, axis=0, keepdims=True)
        inv = jnp.where(m > 0.5 * NEG, 1.0 / jnp.maximum(l, 1e-30), 0.0)
        p = p * inv
        ot = _dot(vcct, p.astype(BF16))
        for a2 in range(2):
            blk = jnp.concatenate([ot[64 * g:64 * g + 64, TQ * (2 * a2 + e):TQ * (2 * a2 + e + 1)] for e in range(2)], axis=0)
            o_ref[:, 256 * g + LANES * a2:256 * g + LANES * (a2 + 1)] = blk.T
        imp = p[:, 0:TQ] + p[:, TQ:2 * TQ] + p[:, 2 * TQ:3 * TQ] + p[:, 3 * TQ:4 * TQ]
        hi = imp.astype(BF16)
        r1 = imp - hi.astype(F32)
        mid = r1.astype(BF16)
        lo = (r1 - mid.astype(F32)).astype(BF16)
        mt = mt_ref[...]
        isel = _dot(mt, hi) + _dot(mt, mid) + _dot(mt, lo)
        score = jnp.where(forced, FORCE_SCORE, isel)
        score = jnp.where(jidx <= cur, score, -jnp.inf)
        sel = _top_rows(score, jidx, min(SEL_TOPN, nsel)) & (jidx <= cur)
        sel_ref[g] = jnp.where(sel, 0.0, NEG)


def _pcmp(qu, ck, cv, cw, nb, t):
    n16 = t // CMP_STRIDE
    nsel = t // SEL_BLK
    nt = t // TQ
    mt = np.zeros((nsel, n16), np.float32)
    for j in range(nsel):
        for c in range(4 * j - 1, 4 * j + 4):
            if 0 <= c < n16 - 1:
                mt[j, c] = 1.0
    mt = jnp.asarray(mt, BF16)
    const = lambda a: pl.BlockSpec(a.shape, lambda bi, ti: (0,) * a.ndim)
    wlist = [cw[k] for k in ("pak", "pbk", "wak", "wbk", "b1k", "w2k", "pav", "pbv", "wav", "wbv", "b1v", "w2v")] + [mt]
    return pl.pallas_call(
        functools.partial(_pcmp_body, n16=n16, nsel=nsel), name="pcmp",
        grid=(nb, nt),
        in_specs=[pl.BlockSpec((TQ, 1024), lambda bi, ti: (bi * nt + ti, 0)),
                  pl.BlockSpec((None, t, LANES), lambda bi, ti: (bi, 0, 0)),
                  pl.BlockSpec((None, t, LANES), lambda bi, ti: (bi, 0, 0))] + [const(a) for a in wlist],
        out_specs=[pl.BlockSpec((TQ, 512), lambda bi, ti: (bi * nt + ti, 0)),
                   pl.BlockSpec((None, None, NSA_G, nsel, TQ), lambda bi, ti: (bi, ti, 0, 0, 0))],
        out_shape=[jax.ShapeDtypeStruct((nb * t, 512), F32),
                   jax.ShapeDtypeStruct((nb, nt, NSA_G, nsel, TQ), F32)],
        scratch_shapes=[pltpu.VMEM((n16, LANES), BF16), pltpu.VMEM((LANES, n16), BF16)],
        compiler_params=_cparams(("parallel", "arbitrary")),
    )(qu, ck.reshape(nb, t, LANES), cv.reshape(nb, t, LANES), *wlist)


def _stack_queries(q, s):
    return jnp.concatenate([q[:, LANES * (4 * s + a):LANES * (4 * s + a + 1)] for a in range(4)], axis=0)


def _flash_out(z, s, o_ref):
    tq = z.shape[1] // 4
    for a2 in range(2):
        blk = jnp.concatenate([z[64 * s:64 * s + 64, tq * (2 * a2 + e):tq * (2 * a2 + e + 1)] for e in range(2)], axis=0)
        o_ref[:, 256 * s + LANES * a2:256 * s + LANES * (a2 + 1)] = blk.T


def _flash_body(*refs, mode, lam_init):
    if mode == "sel":
        q_ref, k_ref, vt_ref, selb_ref, o_ref, acc_ref, m_ref, l_ref, al_ref, s_ref, p_ref = refs
    else:
        (q_ref, k_ref, vt_ref, lq1_ref, lk1_ref, lq2_ref, lk2_ref, sg_ref, o_ref,
         acc_ref, m_ref, l_ref, al_ref, s_ref, p_ref) = refs
    ti = pl.program_id(1)
    tq = TQ * FLASH_TILES
    nq = 4 * tq
    q = q_ref[...]
    qs = [_stack_queries(q, s) for s in range(2)]
    koffs = [LANES * s if mode == "diff" else 0 for s in range(2)]
    m_ref[...] = jnp.full(m_ref.shape, NEG, F32)
    l_ref[...] = jnp.zeros(l_ref.shape, F32)
    acc_ref[...] = jnp.zeros(acc_ref.shape, F32)
    al_ref[...] = jnp.ones(al_ref.shape, F32)
    p_ref[...] = jnp.zeros(p_ref.shape, BF16)

    def scores(j, slot):
        k0 = pl.multiple_of(j * TK, TK)
        for s in range(2):
            s_ref[slot, s] = _dot_nt(k_ref[pl.ds(k0, TK), koffs[s]:koffs[s] + LANES], qs[s])

    def accumulate(j):
        k0 = pl.multiple_of(j * TK, TK)
        for s in range(2):
            acc_ref[s] = al_ref[s] * acc_ref[s] + _dot(vt_ref[koffs[s]:koffs[s] + LANES, pl.ds(k0, TK)], p_ref[s])

    def softmax(j, slot, causal):
        for s in range(2):
            st = s_ref[slot, s]
            if mode == "sel":
                rows = [jnp.broadcast_to(jnp.concatenate(
                    [selb_ref[u, s, pl.ds((TK // SEL_BLK) * j + r, 1), :] for u in range(FLASH_TILES)], axis=1), (SEL_BLK, tq))
                    for r in range(TK // SEL_BLK)]
                bias = jnp.concatenate(rows, axis=0)
                st = st + jnp.concatenate([bias, bias, bias, bias], axis=1)
            if causal:
                krow = lax.broadcasted_iota(jnp.int32, (TK, nq), 0)
                qcol = lax.broadcasted_iota(jnp.int32, (TK, nq), 1) & (tq - 1)
                st = jnp.where(ti * tq + qcol >= j * TK + krow, st, NEG)
            m_old = m_ref[s]
            m_new = jnp.maximum(m_old, jnp.max(st, axis=0, keepdims=True))
            alpha = jnp.exp2(m_old - m_new)
            p = jnp.exp2(st - m_new)
            l_ref[s] = alpha * l_ref[s] + jnp.sum(p, axis=0, keepdims=True)
            p_ref[s] = p.astype(BF16)
            al_ref[s] = alpha
            m_ref[s] = m_new

    n_full = (ti * tq) // TK
    scores(0, 0)

    def body(j, c):
        slot = j & 1
        accumulate(jnp.maximum(j - 1, 0))
        softmax(j, slot, False)
        scores(j + 1, 1 - slot)
        return c

    lax.fori_loop(0, n_full, body, 0)
    accumulate(jnp.maximum(n_full - 1, 0))
    softmax(n_full, n_full & 1, True)
    accumulate(n_full)

    if mode == "diff":
        lam = (jnp.exp(jnp.sum(lq1_ref[...] * lk1_ref[...], keepdims=True))
               - jnp.exp(jnp.sum(lq2_ref[...] * lk2_ref[...], keepdims=True)) + lam_init)
    for s in range(2):
        z = acc_ref[s] * (1.0 / l_ref[s])
        if mode == "diff":
            halves = []
            for hh in range(2):
                zr = z[64 * hh:64 * hh + 64, :]
                d = zr[:, tq * (2 * hh):tq * (2 * hh + 1)] - lam * zr[:, tq * (2 * hh + 1):tq * (2 * hh + 2)]
                ms = jnp.mean(d * d, axis=0, keepdims=True)
                halves.append(d * lax.rsqrt(ms + LN_EPS) * sg_ref[...] * (1.0 - lam_init))
            o_ref[:, LANES * s:LANES * (s + 1)] = jnp.concatenate(halves, axis=0).T
        else:
            _flash_out(z, s, o_ref)


FLASH_TILES = 2


def _flash(mode, q, k, vt, nb, t, extra=(), lam_init=0.0):
    tq = TQ * FLASH_TILES
    assert t % tq == 0 and t % TK == 0 and tq % TK == 0
    nt = t // tq
    kw = k.shape[-1]
    ow = 256 if mode == "diff" else 512
    in_specs = [pl.BlockSpec((tq, 1024), lambda bi, ti: (bi * nt + ti, 0)),
                pl.BlockSpec((None, t, kw), lambda bi, ti: (bi, 0, 0)),
                pl.BlockSpec((None, kw, t), lambda bi, ti: (bi, 0, 0))]
    if mode == "sel":
        nsel = t // SEL_BLK
        in_specs.append(pl.BlockSpec((None, FLASH_TILES, NSA_G, nsel, TQ), lambda bi, ti: (bi, ti, 0, 0, 0)))
    else:
        in_specs += [pl.BlockSpec(a.shape, lambda bi, ti: (0, 0)) for a in extra]
    return pl.pallas_call(
        functools.partial(_flash_body, mode=mode, lam_init=lam_init), name="flash_" + mode,
        grid=(nb, nt), in_specs=in_specs,
        out_specs=pl.BlockSpec((tq, ow), lambda bi, ti: (bi * nt + ti, 0)),
        out_shape=jax.ShapeDtypeStruct((nb * t, ow), F32),
        scratch_shapes=[pltpu.VMEM((2, LANES, 4 * tq), F32)] + [pltpu.VMEM((2, 1, 4 * tq), F32)] * 3
        + [pltpu.VMEM((2, 2, TK, 4 * tq), F32), pltpu.VMEM((2, TK, 4 * tq), BF16)],
        compiler_params=_cparams(("parallel", "parallel")),
    )(q, k.reshape(nb, t, kw), vt, *extra)


def _win_body(q_ref, k_ref, vt_ref, o_ref):
    nq = 4 * TQ
    span = WINDOW + TQ
    krow = lax.broadcasted_iota(jnp.int32, (span, nq), 0)
    qcol = lax.broadcasted_iota(jnp.int32, (span, nq), 1) & (TQ - 1)
    for u in range(WIN_TILES):
        ti = pl.program_id(1) * WIN_TILES + u
        k0 = pl.multiple_of(jnp.maximum(ti * TQ - WINDOW, 0), TQ)
        q = q_ref[TQ * u:TQ * (u + 1), :]
        kt = k_ref[pl.ds(k0, span), :]
        vt = vt_ref[:, pl.ds(k0, span)]
        dlt = ti * TQ + qcol - (k0 + krow)
        vis = (dlt >= 0) & (dlt < WINDOW)
        for s in range(2):
            st = jnp.where(vis, _dot_nt(kt, _stack_queries(q, s)), NEG)
            m = jnp.max(st, axis=0, keepdims=True)
            p = jnp.exp2(st - m)
            l = jnp.sum(p, axis=0, keepdims=True)
            _flash_out(_dot(vt, p.astype(BF16)) * (1.0 / l), s, o_ref.at[TQ * u:TQ * (u + 1), :])


WIN_TILES = 2


def _win(q, k, vt, nb, t):
    nt = t // (TQ * WIN_TILES)
    assert t >= WINDOW + TQ and t % (TQ * WIN_TILES) == 0
    return pl.pallas_call(
        _win_body, name="flash_win", grid=(nb, nt),
        in_specs=[pl.BlockSpec((TQ * WIN_TILES, 1024), lambda bi, ti: (bi * nt + ti, 0)),
                  pl.BlockSpec((None, t, LANES), lambda bi, ti: (bi, 0, 0)),
                  pl.BlockSpec((None, LANES, t), lambda bi, ti: (bi, 0, 0))],
        out_specs=pl.BlockSpec((TQ * WIN_TILES, 512), lambda bi, ti: (bi * nt + ti, 0)),
        out_shape=jax.ShapeDtypeStruct((nb * t, 512), F32),
        compiler_params=_cparams(("parallel", "parallel")),
    )(q, k.reshape(nb, t, LANES), vt)


def _outproj_body(x_ref, yc_ref, oc_ref, os_ref, ow_ref, gt_ref, od_ref, e_ref, w_ref, g_ref, b_ref, o_ref, *, alpha):
    gt = gt_ref[...]
    hi = gt.astype(BF16)
    lo = (gt - hi.astype(F32)).astype(BF16)
    e = e_ref[...]
    gx = _dot(hi, e) + _dot(lo, e)
    onsa = gx[:, 0:512] * oc_ref[...] + gx[:, 512:1024] * os_ref[...] + gx[:, 1024:1536] * ow_ref[...]
    mix = (_dot(yc_ref[...].astype(BF16), w_ref[0:256, :]) + _dot(onsa.astype(BF16), w_ref[256:768, :])
           + _dot(od_ref[...].astype(BF16), w_ref[768:1024, :]))
    o_ref[...] = _ln_rows(alpha * x_ref[...] + mix, g_ref[...], b_ref[...])


def _outproj(x, yc, oc, osel, ow, gt, od, e, w, g, b, alpha, tm):
    n, d = x.shape
    row = lambda c: pl.BlockSpec((tm, c), lambda i: (i, 0))
    const = lambda a: pl.BlockSpec(a.shape, lambda i: (0, 0))
    return pl.pallas_call(
        functools.partial(_outproj_body, alpha=alpha), name="outproj",
        grid=(n // tm,),
        in_specs=[row(d), row(256), row(512), row(512), row(512), row(128), row(256), const(e), const(w), const(g), const(b)],
        out_specs=row(d), out_shape=jax.ShapeDtypeStruct((n, d), F32),
        compiler_params=_cparams(("parallel",)),
    )(x, yc, oc, osel, ow, gt, od, e, w, g, b)


def _ffn_body(x_ref, w1_ref, w3_ref, w2_ref, g_ref, b_ref, o_ref, acc_ref, *, alpha):
    f = pl.program_id(1)
    xb = x_ref[...].astype(BF16)
    h1 = _dot(xb, w1_ref[...])
    h = h1 * _sigmoid(h1) * _dot(xb, w3_ref[...])
    part = _dot(h.astype(BF16), w2_ref[...])

    @pl.when(f == 0)
    def _():
        acc_ref[...] = part

    @pl.when(f != 0)
    def _():
        acc_ref[...] += part

    @pl.when(f == pl.num_programs(1) - 1)
    def _():
        o_ref[...] = _ln_rows(alpha * x_ref[...] + acc_ref[...], g_ref[...], b_ref[...])


def _ffn(x, w1, w3, w2, g, b, alpha, tm, nf):
    n, d = x.shape
    dff = w1.shape[1]
    tf = dff // nf
    return pl.pallas_call(
        functools.partial(_ffn_body, alpha=alpha), name="ffn",
        grid=(n // tm, nf),
        in_specs=[pl.BlockSpec((tm, d), lambda i, f: (i, 0)),
                  pl.BlockSpec((d, tf), lambda i, f: (0, f)),
                  pl.BlockSpec((d, tf), lambda i, f: (0, f)),
                  pl.BlockSpec((tf, d), lambda i, f: (f, 0)),
                  pl.BlockSpec((1, d), lambda i, f: (0, 0)),
                  pl.BlockSpec((1, d), lambda i, f: (0, 0))],
        out_specs=pl.BlockSpec((tm, d), lambda i, f: (i, 0)),
        out_shape=jax.ShapeDtypeStruct((n, d), F32),
        scratch_shapes=[pltpu.VMEM((tm, d), F32)],
        compiler_params=_cparams(("parallel", "arbitrary")),
    )(x, w1, w3, w2, g, b)


def _gather_pages(pt_ref, n_pages, srcs, sem, layer):
    b = pl.program_id(0)
    slot = b & 1

    def copies(bb, sl):
        return [pltpu.make_async_copy(hbm.at[layer, pt_ref[bb, j]], buf.at[sl, :, pl.ds(PAGE * j, PAGE)], sem.at[a, sl])
                for a, (hbm, buf) in enumerate(srcs) for j in range(n_pages)]

    @pl.when(b == 0)
    def _():
        for c in copies(0, 0):
            c.start()

    @pl.when(b + 1 < pl.num_programs(0))
    def _():
        for c in copies(b + 1, 1 - slot):
            c.start()

    for c in copies(b, slot):
        c.wait()
    return slot


def _key_chunks(n):
    per = max(d for d in range(1, 17) if n % d == 0)
    return [(PAGE * per * c, PAGE * per) for c in range(n // per)]


def _stack_heads(q):
    return jnp.concatenate([q[:, LANES * h:LANES * (h + 1)] for h in range(8)], axis=0)


def _scmp_body(pt_ref, qu_ref, ck_hbm, cv_hbm, pak_ref, pbk_ref, wak_ref, wbk_ref, b1k_ref, w2k_ref,
               pav_ref, pbv_ref, wav_ref, wbv_ref, b1v_ref, w2v_ref, m_ref,
               o_ref, sel_ref, kbt, vbt, kbuf, vbuf, sem, *, layer, n_pages, past, nq):
    slot = _gather_pages(pt_ref, n_pages, [(ck_hbm, kbt), (cv_hbm, vbt)], sem, layer)
    for j in range(n_pages):
        kbuf[PAGE * j:PAGE * (j + 1), :] = kbt[slot, :, PAGE * j:PAGE * (j + 1)].T
        vbuf[PAGE * j:PAGE * (j + 1), :] = vbt[slot, :, PAGE * j:PAGE * (j + 1)].T
    n16 = past // CMP_STRIDE
    kcc = _compress(kbuf, n16, pak_ref[...], pbk_ref[...], wak_ref, wbk_ref, b1k_ref[...], w2k_ref).astype(BF16)
    vcc = _compress(vbuf, n16, pav_ref[...], pbv_ref[...], wav_ref, wbv_ref, b1v_ref[...], w2v_ref).astype(BF16)
    qa = _stack_heads(qu_ref[...])
    rows = 8 * nq
    s = _dot_nt(qa, kcc)
    cidx = lax.broadcasted_iota(jnp.int32, (rows, n16), 1)
    qpos = past + (lax.broadcasted_iota(jnp.int32, (rows, n16), 0) & (nq - 1))
    vis = (CMP_STRIDE * cidx + CMP_LEN - 1 <= qpos) & (cidx < n16 - 1)
    s = jnp.where(vis, s, NEG)
    m = jnp.max(s, axis=-1, keepdims=True)
    p = jnp.where(vis, jnp.exp2(s - m), 0.0)
    l = jnp.sum(p, axis=-1, keepdims=True)
    p = p / jnp.maximum(l, 1e-30)
    o_ref[...] = _dot(p.astype(BF16), vcc)
    imp = jnp.concatenate(
        [p[nq * 4 * g:nq * (4 * g + 1)] + p[nq * (4 * g + 1):nq * (4 * g + 2)]
         + p[nq * (4 * g + 2):nq * (4 * g + 3)] + p[nq * (4 * g + 3):nq * (4 * g + 4)] for g in range(NSA_G)], axis=0)
    isel = _split3_dot(imp, m_ref[...])
    nselp = isel.shape[1]
    jidx = lax.broadcasted_iota(jnp.int32, (NSA_G * nq, nselp), 1)
    qp = past + (lax.broadcasted_iota(jnp.int32, (NSA_G * nq, nselp), 0) & (nq - 1))
    cur = qp >> 6
    forced = (jidx == 0) | (jidx == cur) | (jidx == cur - 1)
    score = jnp.where(forced, FORCE_SCORE, isel)
    sel_ref[...] = jnp.where(jidx <= cur, score, -jnp.inf)


def _stopk_body(score_ref, sel_ref, *, k, past, nq):
    score = score_ref[...]
    nselp = score.shape[1]
    jidx = lax.broadcasted_iota(jnp.int32, score.shape, 1)
    cur = (past + (lax.broadcasted_iota(jnp.int32, score.shape, 0) & (nq - 1))) >> 6
    sel = jnp.zeros(score.shape, F32)
    for _ in range(k):
        mx = jnp.max(score, axis=-1, keepdims=True)
        first = jnp.min(jnp.where(score == mx, jidx, nselp), axis=-1, keepdims=True)
        hit = jidx == first
        sel = jnp.where(hit, 1.0, sel)
        score = jnp.where(hit, -jnp.inf, score)
    sel_ref[...] = jnp.where(jidx <= cur, sel, 0.0)


def _stopk(score, past, nq):
    nb, r, nselp = score.shape
    n_sel = -(-(past + nq) // SEL_BLK)
    full = pl.BlockSpec((nb * r, nselp), lambda i: (0, 0))
    sel = pl.pallas_call(
        functools.partial(_stopk_body, k=min(SEL_TOPN, n_sel), past=past, nq=nq), name="stopk",
        grid=(1,), in_specs=[full], out_specs=full,
        out_shape=jax.ShapeDtypeStruct((nb * r, nselp), F32),
        compiler_params=_cparams(("arbitrary",)),
    )(score.reshape(nb * r, nselp))
    return sel.reshape(nb, r, nselp)


def _scmp(pt, qu, cache_k, cache_v, cw, layer, nb, nq, past):
    n_pages = past // PAGE
    n16 = past // CMP_STRIDE
    n_sel = -(-(past + nq) // SEL_BLK)
    nselp = -(-n_sel // LANES) * LANES
    m = np.zeros((n16, nselp), np.float32)
    for j in range(n_sel):
        for c in range(4 * j - 1, 4 * j + 4):
            if 0 <= c < n16 - 1:
                m[c, j] = 1.0
    m = jnp.asarray(m, BF16)
    wlist = [cw[k] for k in ("pak", "pbk", "wak", "wbk", "b1k", "w2k", "pav", "pbv", "wav", "wbv", "b1v", "w2v")] + [m]
    const = lambda a: pl.BlockSpec(a.shape, lambda bi, pt_: (0,) * a.ndim)
    any_spec = pl.BlockSpec(memory_space=pl.ANY)
    grid_spec = pltpu.PrefetchScalarGridSpec(
        num_scalar_prefetch=1, grid=(nb,),
        in_specs=[pl.BlockSpec((nq, 1024), lambda bi, pt_: (bi, 0)), any_spec, any_spec] + [const(a) for a in wlist],
        out_specs=[pl.BlockSpec((None, 8 * nq, LANES), lambda bi, pt_: (bi, 0, 0)),
                   pl.BlockSpec((None, NSA_G * nq, nselp), lambda bi, pt_: (bi, 0, 0))],
        scratch_shapes=[pltpu.VMEM((2, LANES, past), F32), pltpu.VMEM((2, LANES, past), F32),
                        pltpu.VMEM((past, LANES), F32), pltpu.VMEM((past, LANES), F32), pltpu.SemaphoreType.DMA((2, 2))])
    return pl.pallas_call(
        functools.partial(_scmp_body, layer=layer, n_pages=n_pages, past=past, nq=nq), name="scmp",
        grid_spec=grid_spec,
        out_shape=[jax.ShapeDtypeStruct((nb, 8 * nq, LANES), F32), jax.ShapeDtypeStruct((nb, NSA_G * nq, nselp), F32)],
        compiler_params=_cparams(("arbitrary",)),
    )(pt, qu, cache_k, cache_v, *wlist)


def _softmax_rows(s):
    m = jnp.max(s, axis=-1, keepdims=True)
    p = jnp.exp2(s - m)
    return p, jnp.sum(p, axis=-1, keepdims=True)


def _sselwin_body(pt_ref, qr_ref, sel_ref, skn_ref, svn_ref, wkn_ref, wvn_ref, wks_ref, wvs_ref, e_ref, sk_hbm, sv_hbm,
                  osel_ref, owin_ref, kbuf, vbuf, sem, *, layer, n_pages, past, nq):
    ktot = past + PAGE
    rows = 8 * nq
    slot = _gather_pages(pt_ref, n_pages, [(sk_hbm, kbuf), (sv_hbm, vbuf)], sem, layer)
    pad = jnp.zeros((PAGE - nq, LANES), F32)
    kbuf[slot, :, pl.ds(past, PAGE)] = jnp.concatenate([skn_ref[...], pad], axis=0).T
    vbuf[slot, :, pl.ds(past, PAGE)] = jnp.concatenate([svn_ref[...], pad], axis=0).T
    chunks = _key_chunks(n_pages + 1)
    qa = _stack_heads(qr_ref[...])
    s = jnp.concatenate([_dot(qa, kbuf[slot, :, pl.ds(c0, cn)].astype(BF16)) for c0, cn in chunks], axis=1)
    se = _dot(sel_ref[...].astype(BF16), e_ref[...])
    se = jnp.concatenate([se[nq * g:nq * (g + 1)] for g in range(NSA_G) for _ in range(NSA_HPG)], axis=0)
    kpos = lax.broadcasted_iota(jnp.int32, (rows, ktot), 1)
    qpos = past + (lax.broadcasted_iota(jnp.int32, (rows, ktot), 0) & (nq - 1))
    s = jnp.where((se > 0.5) & (kpos <= qpos), s, NEG)
    p, l = _softmax_rows(s)
    pb = p.astype(BF16)
    o = _dot_nt(pb[:, 0:chunks[0][1]], vbuf[slot, :, pl.ds(0, chunks[0][1])].astype(BF16))
    for c0, cn in chunks[1:]:
        o = o + _dot_nt(pb[:, c0:c0 + cn], vbuf[slot, :, pl.ds(c0, cn)].astype(BF16))
    osel_ref[...] = o / l
    wb = wks_ref.shape[0]
    kw = jnp.concatenate([wks_ref[...], wkn_ref[...], pad], axis=0).astype(BF16)
    vw = jnp.concatenate([wvs_ref[...], wvn_ref[...], pad], axis=0).astype(BF16)
    s = _dot_nt(qa, kw)
    i = lax.broadcasted_iota(jnp.int32, (rows, wb + PAGE), 1)
    kp = jnp.where(i < wb, past - wb + i, past + i - wb)
    qp = past + (lax.broadcasted_iota(jnp.int32, (rows, wb + PAGE), 0) & (nq - 1))
    dlt = qp - kp
    s = jnp.where((dlt >= 0) & (dlt < WINDOW) & (i < wb + nq), s, NEG)
    p, l = _softmax_rows(s)
    owin_ref[...] = _dot(p.astype(BF16), vw) / l


def _sselwin(pt, qr, sel, skn, svn, wkn, wvn, wks, wvs, cache_k, cache_v, layer, nb, nq, past):
    n_pages = past // PAGE
    ktot = past + PAGE
    nselp = sel.shape[-1]
    e = (np.arange(ktot)[None, :] // SEL_BLK == np.arange(nselp)[:, None]).astype(np.float32)
    e = jnp.asarray(e, BF16)
    wb = wks.shape[1]
    any_spec = pl.BlockSpec(memory_space=pl.ANY)
    new = pl.BlockSpec((nq, LANES), lambda bi, pt_: (bi, 0))
    grid_spec = pltpu.PrefetchScalarGridSpec(
        num_scalar_prefetch=1, grid=(nb,),
        in_specs=[pl.BlockSpec((nq, 1024), lambda bi, pt_: (bi, 0)),
                  pl.BlockSpec((None, NSA_G * nq, nselp), lambda bi, pt_: (bi, 0, 0)),
                  new, new, new, new,
                  pl.BlockSpec((None, wb, LANES), lambda bi, pt_: (bi, 0, 0)),
                  pl.BlockSpec((None, wb, LANES), lambda bi, pt_: (bi, 0, 0)),
                  pl.BlockSpec(e.shape, lambda bi, pt_: (0, 0)), any_spec, any_spec],
        out_specs=[pl.BlockSpec((None, 8 * nq, LANES), lambda bi, pt_: (bi, 0, 0)),
                   pl.BlockSpec((None, 8 * nq, LANES), lambda bi, pt_: (bi, 0, 0))],
        scratch_shapes=[pltpu.VMEM((2, LANES, ktot), F32), pltpu.VMEM((2, LANES, ktot), F32), pltpu.SemaphoreType.DMA((2, 2))])
    return pl.pallas_call(
        functools.partial(_sselwin_body, layer=layer, n_pages=n_pages, past=past, nq=nq), name="sselwin",
        grid_spec=grid_spec,
        out_shape=[jax.ShapeDtypeStruct((nb, 8 * nq, LANES), F32)] * 2,
        compiler_params=_cparams(("arbitrary",)),
    )(pt, qr, sel, skn, svn, wkn, wvn, wks, wvs, e, cache_k, cache_v)


def _sdiff_body(pt_ref, dq_ref, dkn_ref, dvn_ref, lq1_ref, lk1_ref, lq2_ref, lk2_ref, sg_ref, dk_hbm, dv_hbm,
                o_ref, kbuf, vbuf, sem, *, layer, n_pages, past, nq, lam_init):
    ktot = past + PAGE
    rows = 8 * nq
    slot = _gather_pages(pt_ref, n_pages, [(dk_hbm, kbuf), (dv_hbm, vbuf)], sem, layer)
    pad = jnp.zeros((PAGE - nq, 256), F32)
    kbuf[slot, :, pl.ds(past, PAGE)] = jnp.concatenate([dkn_ref[...], pad], axis=0).T
    vbuf[slot, :, pl.ds(past, PAGE)] = jnp.concatenate([dvn_ref[...], pad], axis=0).T
    chunks = _key_chunks(n_pages + 1)
    q = dq_ref[...]
    zero = jnp.zeros((nq, LANES), BF16)
    blocks = []
    for a in range(8):
        chunk = q[:, LANES * a:LANES * (a + 1)]
        blocks.append(jnp.concatenate([chunk, zero] if a < 4 else [zero, chunk], axis=1))
    qa = jnp.concatenate(blocks, axis=0)
    s = jnp.concatenate([_dot(qa, kbuf[slot, :, pl.ds(c0, cn)].astype(BF16)) for c0, cn in chunks], axis=1)
    kpos = lax.broadcasted_iota(jnp.int32, (rows, ktot), 1)
    qpos = past + (lax.broadcasted_iota(jnp.int32, (rows, ktot), 0) & (nq - 1))
    s = jnp.where(kpos <= qpos, s, NEG)
    p, l = _softmax_rows(s)
    pb = p.astype(BF16)
    o = _dot_nt(pb[:, 0:chunks[0][1]], vbuf[slot, :, pl.ds(0, chunks[0][1])].astype(BF16))
    for c0, cn in chunks[1:]:
        o = o + _dot_nt(pb[:, c0:c0 + cn], vbuf[slot, :, pl.ds(c0, cn)].astype(BF16))
    o = o / l
    lam = (jnp.exp(jnp.sum(lq1_ref[...] * lk1_ref[...], keepdims=True))
           - jnp.exp(jnp.sum(lq2_ref[...] * lk2_ref[...], keepdims=True)) + lam_init)
    lane = lax.broadcasted_iota(jnp.int32, (nq, 256), 1)
    out = jnp.zeros((nq, 256), F32)
    for h in range(DIFF_H):
        d = o[2 * nq * h:2 * nq * h + nq] - lam * o[2 * nq * h + nq:2 * nq * (h + 1)]
        inh = (lane >> 6) == h
        ms = jnp.sum(jnp.where(inh, d * d, 0.0), axis=-1, keepdims=True) * (1.0 / DIFF_DV)
        out = out + jnp.where(inh, d * lax.rsqrt(ms + LN_EPS), 0.0)
    o_ref[...] = out * sg_ref[...] * (1.0 - lam_init)


def _sdiff(pt, dq, dkn, dvn, lams, sg_row, cache_k, cache_v, layer, nb, nq, past, lam_init):
    n_pages = past // PAGE
    ktot = past + PAGE
    any_spec = pl.BlockSpec(memory_space=pl.ANY)
    new = pl.BlockSpec((nq, 256), lambda bi, pt_: (bi, 0))
    small = [pl.BlockSpec(a.shape, lambda bi, pt_: (0, 0)) for a in (*lams, sg_row)]
    grid_spec = pltpu.PrefetchScalarGridSpec(
        num_scalar_prefetch=1, grid=(nb,),
        in_specs=[pl.BlockSpec((nq, 1024), lambda bi, pt_: (bi, 0)), new, new] + small + [any_spec, any_spec],
        out_specs=pl.BlockSpec((nq, 256), lambda bi, pt_: (bi, 0)),
        scratch_shapes=[pltpu.VMEM((2, 256, ktot), F32), pltpu.VMEM((2, 256, ktot), F32), pltpu.SemaphoreType.DMA((2, 2))])
    return pl.pallas_call(
        functools.partial(_sdiff_body, layer=layer, n_pages=n_pages, past=past, nq=nq, lam_init=lam_init), name="sdiff",
        grid_spec=grid_spec,
        out_shape=jax.ShapeDtypeStruct((nb * nq, 256), F32),
        compiler_params=_cparams(("arbitrary",)),
    )(pt, dq, dkn, dvn, *lams, sg_row, cache_k, cache_v)


def _prep_w_in(w):
    pts = np.cumsum([256, 256, 512, 128, 128, 128, 128, 128, 128, 24, 256, 256, 256])[:-1].tolist()
    ca, cg, nq, ck, cv, sk, sv, wk, wv, gt, dq, dk, dv = jnp.split(w, pts, axis=1)
    gtp = jnp.pad(gt, ((0, 0), (0, LANES - gt.shape[1])))
    return jnp.concatenate([ca, cg, nq * (NSA_DH ** -0.5 * LOG2E), ck, cv, sk, sv, wk, wv, gtp, dq, dk, dv], axis=1).astype(BF16)


def _rope_tables(pos):
    out = []
    lane = np.arange(LANES)
    for half in (32, 16):
        inv = ROPE_THETA ** (-jnp.arange(half, dtype=F32) / half)
        ang = pos.astype(F32)[:, None] * inv[None, :]
        idx = lane % half
        sign = jnp.asarray(np.where(lane % (2 * half) < half, -1.0, 1.0), F32)
        out += [jnp.cos(ang)[:, idx], jnp.sin(ang)[:, idx] * sign[None, :]]
    return out


def _prep_cmp(pe, w1, b1, w2):
    def halves(x):
        res = []
        for part in (x[:16 * NSA_DH], x[16 * NSA_DH:]):
            p4 = part.reshape(CMP_STRIDE, 1, NSA_DH, 1, -1)
            eye = jnp.eye(NSA_G, dtype=F32)[None, :, None, :, None]
            res.append((p4 * eye).reshape(CMP_STRIDE * NSA_G * NSA_DH, NSA_G * part.shape[-1]))
        return res
    wa, wb = halves(w1)
    pea = jnp.tile(pe[:16, None, :], (1, NSA_G, 1)).reshape(1, -1)
    peb = jnp.tile(pe[16:, None, :], (1, NSA_G, 1)).reshape(1, -1)
    w2bd = (w2[None, :, None, :] * jnp.eye(NSA_G, dtype=F32)[:, None, :, None]).reshape(NSA_G * CMP_HID, NSA_G * NSA_DH)
    return pea, peb, wa.astype(BF16), wb.astype(BF16), jnp.tile(b1, NSA_G)[None, :], w2bd.astype(BF16)


def _gate_expand():
    e = np.zeros((LANES, 3 * 512), np.float32)
    for h in range(NSA_H):
        for br in range(3):
            e[3 * h + br, 512 * br + 64 * h:512 * br + 64 * (h + 1)] = 1.0
    return jnp.asarray(e, BF16)


def _rows_to_tokens(o, nb, nq):
    o6 = o.reshape(nb, NSA_G, NSA_HPG, nq, NSA_G, NSA_DH)
    pick = jnp.stack([o6[:, g, :, :, g, :] for g in range(NSA_G)], axis=1)
    return pick.transpose(0, 3, 1, 2, 4).reshape(nb * nq, NSA_H * NSA_DH)


def _prompt_layer(x, lw, tabs, nb, t, lam_init, alpha):
    n = nb * t
    (u, qu, qr, gt, dq, ck, cv, skb, wkb, dkb, svb, wvb, dvb,
     ckt, cvt, skt, svt, wkt, wvt, dkt, dvt) = _inproj(x, lw["w_in"], tabs, nb, t, 256, True)
    ext = jnp.pad(u.reshape(nb, t, 256), ((0, 0), (CONV_PAD, 0), (0, 0)))
    yc = _conv(ext, lw["dw_w"], lw["dw_b"], lw["cln_g"], lw["cln_b"], t, 256).reshape(n, 256)
    ocmp, sel = _pcmp(qu, ck, cv, lw["cmp"], nb, t)
    osel = _flash("sel", qr, skb, svb, nb, t, extra=(sel,))
    owin = _win(qr, wkb, wvb, nb, t)
    odiff = _flash("diff", dq, dkb, dvb, nb, t, extra=lw["lams"] + (lw["sg_col"],), lam_init=lam_init)
    x1 = _outproj(x, yc, ocmp, osel, owin, gt, odiff, lw["gate_e"], lw["w_out"], lw["ln1_g"], lw["ln1_b"], alpha, 256)
    x2 = _ffn(x1, lw["w1"], lw["w3"], lw["w2"], lw["ln2_g"], lw["ln2_b"], alpha, 512, 2)
    nk = min(WINDOW, t)
    rows_major = lambda a: a.reshape(nb, a.shape[1] // 64, 64, a.shape[2]).transpose(0, 3, 1, 2)
    news = (rows_major(ckt), rows_major(cvt), rows_major(skt), rows_major(svt), rows_major(dkt), rows_major(dvt),
            rows_major(wkt[:, :, t - nk:]), rows_major(wvt[:, :, t - nk:]),
            u.reshape(nb, t, 256)[:, t - (CONV_W - 1):])
    return x2, news


def _sample_layer(x, lw, tabs, caches, states, pt, layer, nb, nq, past, lam_init, alpha):
    n = nb * nq
    (u, qu, qr, gt, dq, ck, cv, sk, sv, wk, wv, dk, dv) = _inproj(x, lw["w_in"], tabs, nb, nq, n, False)
    c_cmp_k, c_cmp_v, c_sel_k, c_sel_v, c_diff_k, c_diff_v = caches
    st_wk, st_wv, st_conv = states
    ext = jnp.concatenate([jnp.zeros((nb, CONV_PAD - (CONV_W - 1), 256), F32), st_conv, u.reshape(nb, nq, 256)], axis=1)
    yc = _conv(ext, lw["dw_w"], lw["dw_b"], lw["cln_g"], lw["cln_b"], nq, nq).reshape(n, 256)
    ocmp, score = _scmp(pt, qu, c_cmp_k, c_cmp_v, lw["cmp"], layer, nb, nq, past)
    sel = _stopk(score, past, nq)
    wb = st_wk.shape[1]
    osel, owin = _sselwin(pt, qr, sel, sk, sv, wk, wv, st_wk.reshape(nb, wb, LANES), st_wv.reshape(nb, wb, LANES),
                          c_sel_k, c_sel_v, layer, nb, nq, past)
    odiff = _sdiff(pt, dq, dk, dv, lw["lams"], lw["sg_row"], c_diff_k, c_diff_v, layer, nb, nq, past, lam_init)
    x1 = _outproj(x, yc, _rows_to_tokens(ocmp, nb, nq), _rows_to_tokens(osel, nb, nq), _rows_to_tokens(owin, nb, nq),
                  gt, odiff, lw["gate_e"], lw["w_out"], lw["ln1_g"], lw["ln1_b"], alpha, n)
    x2 = _ffn(x1, lw["w1"], lw["w3"], lw["w2"], lw["ln2_g"], lw["ln2_b"], alpha, n, 2)
    new_wk = jnp.concatenate([st_wk, wk.reshape(nb, nq, NSA_G, NSA_DH)], axis=1)[:, -wb:]
    new_wv = jnp.concatenate([st_wv, wv.reshape(nb, nq, NSA_G, NSA_DH)], axis=1)[:, -wb:]
    new_conv = jnp.concatenate([st_conv, u.reshape(nb, nq, 256)], axis=1)[:, -(CONV_W - 1):]
    news = (ck.reshape(nb, nq, NSA_G, NSA_DH), cv.reshape(nb, nq, NSA_G, NSA_DH),
            sk.reshape(nb, nq, NSA_G, NSA_DH), sv.reshape(nb, nq, NSA_G, NSA_DH),
            dk.reshape(nb, nq, DIFF_H, 2 * DIFF_DQK), dv.reshape(nb, nq, DIFF_H, DIFF_DV),
            new_wk, new_wv, new_conv)
    return x2, news


def kernel(x_prompt, x_sample, cache_nsa_cmp_k, cache_nsa_cmp_v, cache_nsa_sel_k, cache_nsa_sel_v, cache_diff_k, cache_diff_v, state_nsa_win_k, state_nsa_win_v, state_conv, page_table, w_in, conv_dw_w, conv_dw_b, conv_ln_g, conv_ln_b, cmp_pe_k, cmp_w1_k, cmp_b1_k, cmp_w2_k, cmp_pe_v, cmp_w1_v, cmp_b1_v, cmp_w2_v, diff_lq1, diff_lk1, diff_lq2, diff_lk2, diff_subln_g, w_out, ln1_g, ln1_b, ln2_g, ln2_b, ffn_w1, ffn_w3, ffn_w2):
    nb, t, d = x_prompt.shape
    sb, nq, _ = x_sample.shape
    depth = w_in.shape[0]
    n_pool = cache_nsa_cmp_k.shape[1]
    past = page_table.shape[1] * PAGE
    alpha = (2 * depth) ** 0.25
    tabs_p = _rope_tables(jnp.arange(t, dtype=jnp.int32))
    tabs_s = _rope_tables(jnp.tile(past + jnp.arange(nq, dtype=jnp.int32), sb))
    as_pages = lambda c: c.transpose(0, 1, 3, 4, 2).reshape(depth, n_pool, c.shape[3] * c.shape[4], PAGE)
    caches = tuple(as_pages(c) for c in (cache_nsa_cmp_k, cache_nsa_cmp_v, cache_nsa_sel_k, cache_nsa_sel_v,
                                         cache_diff_k, cache_diff_v))
    gate_e = _gate_expand()
    xp = x_prompt.reshape(nb * t, d)
    xs = x_sample.reshape(sb * nq, d)
    outs_p, outs_s = [], []
    for l in range(depth):
        ck = _prep_cmp(cmp_pe_k[l], cmp_w1_k[l], cmp_b1_k[l], cmp_w2_k[l])
        cv = _prep_cmp(cmp_pe_v[l], cmp_w1_v[l], cmp_b1_v[l], cmp_w2_v[l])
        names = ("pa", "pb", "wa", "wb", "b1", "w2")
        cmpw = {n_ + "k": a for n_, a in zip(names, ck)}
        cmpw.update({n_ + "v": a for n_, a in zip(names, cv)})
        lw = dict(
            w_in=_prep_w_in(w_in[l]), dw_w=conv_dw_w[l], dw_b=conv_dw_b[l][None], cln_g=conv_ln_g[l][None],
            cln_b=conv_ln_b[l][None], cmp=cmpw,
            lams=(diff_lq1[l][None], diff_lk1[l][None], diff_lq2[l][None], diff_lk2[l][None]),
            sg_col=diff_subln_g[l][:, None], sg_row=jnp.tile(diff_subln_g[l], DIFF_H)[None],
            gate_e=gate_e, w_out=w_out[l].astype(BF16), ln1_g=ln1_g[l][None], ln1_b=ln1_b[l][None],
            ln2_g=ln2_g[l][None], ln2_b=ln2_b[l][None],
            w1=ffn_w1[l].astype(BF16), w3=ffn_w3[l].astype(BF16), w2=ffn_w2[l].astype(BF16))
        lam_init = 0.8 - 0.6 * math.exp(-0.3 * l)
        xp, new_p = _prompt_layer(xp, lw, tabs_p, nb, t, lam_init, alpha)
        xs, new_s = _sample_layer(xs, lw, tabs_s, caches, (state_nsa_win_k[l], state_nsa_win_v[l], state_conv[l]),
                                  page_table, l, sb, nq, past, lam_init, alpha)
        outs_p.append(new_p)
        outs_s.append(new_s)
    stk_p = [jnp.stack([o[i] for o in outs_p]) for i in range(9)]
    stk_s = [jnp.stack([o[i] for o in outs_s]) for i in range(9)]
    return (xp.reshape(nb, t, d), xs.reshape(sb, nq, d), *stk_p, *stk_s)
```

```python
import functools
import math

import jax
import jax.numpy as jnp
import numpy as np
from jax import lax
from jax.experimental import pallas as pl
from jax.experimental.pallas import tpu as pltpu

F32 = jnp.float32
BF16 = jnp.bfloat16

CONV_W = 31
NSA_H = 8
NSA_G = 2
NSA_HPG = NSA_H // NSA_G
NSA_DH = 64
CMP_STRIDE = 16
CMP_LEN = 32
CMP_HID = 128
SEL_BLK = 64
SEL_TOPN = 16
WINDOW = 512
FORCE_SCORE = 1e9
DIFF_H = 4
DIFF_DV = 64
DIFF_DQK = 32
ROPE_THETA = 10000.0
LN_EPS = 1e-5
PAGE = 128

LANES = 128
TQ = 128
TK = 256
LOG2E = 1.4426950408889634
NEG = -1e30
VMEM_LIMIT = 52 * 1024 * 1024

C_CA, C_CG, C_Q, C_KV, C_GT, C_DQ, C_DK, C_DV, C_END = 0, 256, 512, 1024, 1792, 1920, 2176, 2432, 2688


def _cparams(sem):
    return pltpu.CompilerParams(dimension_semantics=sem, vmem_limit_bytes=VMEM_LIMIT)


def _sigmoid(x):
    return 1.0 / (1.0 + jnp.exp(-x))


def _ln_rows(x, g, b):
    mu = jnp.mean(x, axis=-1, keepdims=True)
    xc = x - mu
    var = jnp.mean(xc * xc, axis=-1, keepdims=True)
    return xc * lax.rsqrt(var + LN_EPS) * g + b


def _dot(a, b):
    return jnp.dot(a, b, preferred_element_type=F32)


def _dot_nt(a, b):
    return lax.dot_general(a, b, (((1,), (1,)), ((), ())), preferred_element_type=F32)


def _split3_dot(x, m):
    hi = x.astype(BF16)
    r1 = x - hi.astype(F32)
    mid = r1.astype(BF16)
    lo = (r1 - mid.astype(F32)).astype(BF16)
    return _dot(hi, m) + _dot(mid, m) + _dot(lo, m)


def _rope(x, cos, sin_signed, half):
    lane = lax.broadcasted_iota(jnp.int32, x.shape, 1)
    first = (lane & (2 * half - 1)) < half
    rot = jnp.where(first, pltpu.roll(x, LANES - half, 1), pltpu.roll(x, half, 1))
    return x * cos + rot * sin_signed


def _inproj_body(x_ref, w_ref, c64_ref, s64_ref, c32_ref, s32_ref,
                 u_ref, qu_ref, qr_ref, gt_ref, dq_ref, ck_ref, cv_ref, *refs, dq_scale, transposed):
    xb = x_ref[...].astype(BF16)

    def mm(lo, hi):
        return _dot(xb, w_ref[:, lo:hi])

    c64, s64, c32, s32 = c64_ref[...], s64_ref[...], c32_ref[...], s32_ref[...]
    z = mm(C_CA, C_Q)
    u_ref[...] = z[:, :256] * _sigmoid(z[:, 256:])
    lane = lax.broadcasted_iota(jnp.int32, (x_ref.shape[0], LANES), 1)
    for j in range(4):
        zq = mm(C_Q + LANES * j, C_Q + LANES * (j + 1))
        g = j // 2
        keep = (lane < 64) if g == 0 else (lane >= 64)
        for src, dst_ref in ((zq, qu_ref), (_rope(zq, c64, s64, 32), qr_ref)):
            for e in range(2):
                v = src if e == g else pltpu.roll(src, 64, 1)
                dst_ref[:, LANES * (2 * j + e):LANES * (2 * j + e + 1)] = jnp.where(keep, v, 0.0).astype(BF16)
    z = mm(C_KV, C_GT)
    ck = z[:, 0:128]
    cv = z[:, 128:256]
    ck_ref[...] = ck
    cv_ref[...] = cv
    sk = _rope(z[:, 256:384], c64, s64, 32)
    sv = z[:, 384:512]
    wk = _rope(z[:, 512:640], c64, s64, 32)
    wv = z[:, 640:768]
    gt_ref[...] = _sigmoid(mm(C_GT, C_DQ))
    for c in range(2):
        zq = _rope(mm(C_DQ + LANES * c, C_DQ + LANES * (c + 1)), c32, s32, 16) * dq_scale
        for a in range(4):
            dq_ref[:, LANES * (4 * c + a):LANES * (4 * c + a + 1)] = jnp.where((lane >> 5) == a, zq, 0.0).astype(BF16)
    z = mm(C_DK, C_DV)
    dk = [_rope(z[:, :128], c32, s32, 16), _rope(z[:, 128:], c32, s32, 16)]
    z = mm(C_DV, C_END)
    dv = [z[:, :128], z[:, 128:]]
    if not transposed:
        sk_ref, sv_ref, wk_ref, wv_ref, dk_ref, dv_ref = refs
        sk_ref[...] = sk
        sv_ref[...] = sv
        wk_ref[...] = wk
        wv_ref[...] = wv
        for c in range(2):
            dk_ref[:, LANES * c:LANES * (c + 1)] = dk[c]
            dv_ref[:, LANES * c:LANES * (c + 1)] = dv[c]
        return
    (skb_ref, wkb_ref, dkb_ref, svb_ref, wvb_ref, dvb_ref,
     ckt_ref, cvt_ref, skt_ref, svt_ref, wkt_ref, wvt_ref, dkt_ref, dvt_ref) = refs
    skb_ref[...] = sk.astype(BF16)
    wkb_ref[...] = wk.astype(BF16)
    ckt_ref[...] = ck.T
    cvt_ref[...] = cv.T
    skt_ref[...] = sk.T
    wkt_ref[...] = wk.T
    svt = sv.T
    svt_ref[...] = svt
    svb_ref[...] = svt.astype(BF16)
    wvt = wv.T
    wvt_ref[...] = wvt
    wvb_ref[...] = wvt.astype(BF16)
    for c in range(2):
        rows = slice(LANES * c, LANES * (c + 1))
        dkb_ref[:, rows] = dk[c].astype(BF16)
        dkt_ref[rows, :] = dk[c].T
        dvt = dv[c].T
        dvt_ref[rows, :] = dvt
        dvb_ref[rows, :] = dvt.astype(BF16)


def _inproj(x, w, tabs, nb, t, tm, transposed):
    n, d = x.shape
    nt = t // tm if transposed else 1
    grid = (n // tm,)
    row = lambda c: pl.BlockSpec((tm, c), lambda i: (i, 0))
    tab = pl.BlockSpec((tm, LANES), (lambda i: (i % nt, 0)) if transposed else (lambda i: (i, 0)))
    in_specs = [row(d), pl.BlockSpec((d, C_END), lambda i: (0, 0)), tab, tab, tab, tab]
    shapes = [(256, F32), (1024, BF16), (1024, BF16), (128, F32), (1024, BF16), (128, F32), (128, F32)]
    if transposed:
        shapes += [(128, BF16), (128, BF16), (256, BF16)]
    else:
        shapes += [(128, F32)] * 4 + [(256, F32)] * 2
    out_shape = [jax.ShapeDtypeStruct((n, c), dt) for c, dt in shapes]
    out_specs = [row(c) for c, _ in shapes]
    if transposed:
        for c, dt in [(128, BF16), (128, BF16), (256, BF16)] + [(128, F32)] * 6 + [(256, F32)] * 2:
            out_shape.append(jax.ShapeDtypeStruct((nb, c, t), dt))
            out_specs.append(pl.BlockSpec((None, c, tm), lambda i: (i // nt, 0, i % nt)))
    return pl.pallas_call(
        functools.partial(_inproj_body, dq_scale=DIFF_DQK ** -0.5 * LOG2E, transposed=transposed), name="inproj",
        grid=grid, in_specs=in_specs, out_specs=out_specs, out_shape=out_shape,
        compiler_params=_cparams(("parallel",)),
    )(x, w, *tabs)


CONV_PAD = 32


def _conv_body(ext_ref, w_ref, b_ref, g_ref, beta_ref, y_ref, acc_ref, sh_ref, *, tt, rs):
    t0 = pl.multiple_of(pl.program_id(1) * tt, 8)
    off = CONV_PAD - (CONV_W - 1)
    for c in range(2):
        cs = slice(LANES * c, LANES * (c + 1))
        for r in range(tt // rs):
            win = ext_ref[pl.ds(t0 + rs * r, rs + CONV_PAD), cs]
            acc = jnp.zeros((rs, LANES), F32)
            for r8 in range(8):
                taps = range(r8, CONV_W, 8)
                rows = 8 * (len(taps) - 1) + rs
                sh_ref[0:rows, :] = win[off + r8:off + r8 + rows, :]
                for a, k in enumerate(taps):
                    acc = acc + sh_ref[8 * a:8 * a + rs, :] * w_ref[k:k + 1, cs]
            acc_ref[rs * r:rs * (r + 1), cs] = acc
    y = _ln_rows(acc_ref[...] + b_ref[...], g_ref[...], beta_ref[...])
    y_ref[...] = y * _sigmoid(y)


def _conv(ext, w, b, g, beta, t, tt):
    nb, le, c = ext.shape
    rs = min(tt, 64)
    vec = pl.BlockSpec((1, c), lambda bi, ti: (0, 0))
    return pl.pallas_call(
        functools.partial(_conv_body, tt=tt, rs=rs), name="conv",
        grid=(nb, t // tt),
        in_specs=[pl.BlockSpec((None, le, c), lambda bi, ti: (bi, 0, 0)),
                  pl.BlockSpec((CONV_W, c), lambda bi, ti: (0, 0)), vec, vec, vec],
        out_specs=pl.BlockSpec((None, tt, c), lambda bi, ti: (bi, ti, 0)),
        out_shape=jax.ShapeDtypeStruct((nb, t, c), F32),
        scratch_shapes=[pltpu.VMEM((tt, c), F32), pltpu.VMEM((rs + CONV_PAD, LANES), F32)],
        compiler_params=_cparams(("parallel", "parallel")),
    )(ext, w, b, g, beta)


def _compress(src_ref, n16, pea, peb, wa_ref, wb_ref, b1, w2_ref):
    x = jnp.concatenate([src_ref[pl.ds(p, n16, stride=CMP_STRIDE), :] for p in range(CMP_STRIDE)], axis=1)
    a = _dot((x + pea).astype(BF16), wa_ref[...])
    bm = _dot((x + peb).astype(BF16), wb_ref[...])
    h = a + pltpu.roll(bm, n16 - 1, 0) + b1
    gl = 0.5 * h * (1.0 + jnp.tanh(0.7978845608028654 * (h + 0.044715 * (h * h * h))))
    return _dot(gl.astype(BF16), w2_ref[...])


def _top_rows(score, ridx, k):
    nv = score.shape[0] // 8
    tiles = [score[8 * v:8 * (v + 1), :] for v in range(nv)]
    cnts = [jnp.zeros(t.shape, F32) for t in tiles]
    for jp in range(score.shape[0]):
        row = score[jp:jp + 1, :]
        for v in range(nv):
            if 8 * v > jp:
                beats = row >= tiles[v]
            elif 8 * v + 7 < jp:
                beats = row > tiles[v]
            else:
                beats = (row > tiles[v]) | ((row == tiles[v]) & (ridx[8 * v:8 * (v + 1), :] > jp))
            cnts[v] = cnts[v] + jnp.where(beats, 1.0, 0.0)
    return jnp.concatenate(cnts, axis=0) < k


def _pcmp_body(qu_ref, ck_ref, cv_ref, pak_ref, pbk_ref, wak_ref, wbk_ref, b1k_ref, w2k_ref,
               pav_ref, pbv_ref, wav_ref, wbv_ref, b1v_ref, w2v_ref, mt_ref,
               o_ref, sel_ref, kcc_ref, vcct_ref, *, n16, nsel):
    @pl.when(pl.program_id(1) == 0)
    def _():
        kcc = _compress(ck_ref, n16, pak_ref[...], pbk_ref[...], wak_ref, wbk_ref, b1k_ref[...], w2k_ref)
        kcc_ref[...] = kcc.astype(BF16)
        vcc = _compress(cv_ref, n16, pav_ref[...], pbv_ref[...], wav_ref, wbv_ref, b1v_ref[...], w2v_ref)
        vcct_ref[...] = vcc.T.astype(BF16)

    for u in range(PCMP_TILES):
        _pcmp_tile(pl.program_id(1) * PCMP_TILES + u, qu_ref.at[TQ * u:TQ * (u + 1), :], kcc_ref[...], vcct_ref[...],
                   mt_ref[...], o_ref.at[TQ * u:TQ * (u + 1), :], sel_ref.at[u], n16, nsel)


PCMP_TILES = 2


def _pcmp_tile(ti, qu_ref, kcc, vcct, mt, o_ref, sel_ref, n16, nsel):
    q = qu_ref[...]
    nq = 4 * TQ
    cidx = lax.broadcasted_iota(jnp.int32, (n16, nq), 0)
    qpos = ti * TQ + (lax.broadcasted_iota(jnp.int32, (n16, nq), 1) & (TQ - 1))
    vis = (CMP_STRIDE * cidx + CMP_LEN - 1 <= qpos) & (cidx < n16 - 1)
    jidx = lax.broadcasted_iota(jnp.int32, (nsel, TQ), 0)
    qp1 = ti * TQ + lax.broadcasted_iota(jnp.int32, (nsel, TQ), 1)
    cur = qp1 >> 6
    forced = (jidx == 0) | (jidx == cur) | (jidx == cur - 1)
    for g in range(NSA_G):
        qs = jnp.concatenate([q[:, LANES * (4 * g + a):LANES * (4 * g + a + 1)] for a in range(4)], axis=0)
        st = _dot_nt(kcc, qs)
        st = jnp.where(vis, st, NEG)
        m = jnp.max(st, axis=0, keepdims=True)
        p = jnp.exp2(st - m)
        l = jnp.sum(p, axis=0, keepdims=True)
        inv = jnp.where(m > 0.5 * NEG, 1.0 / jnp.maximum(l, 1e-30), 0.0)
        p = p * inv
        ot = _dot(vcct, p.astype(BF16))
        for a2 in range(2):
            blk = jnp.concatenate([ot[64 * g:64 * g + 64, TQ * (2 * a2 + e):TQ * (2 * a2 + e + 1)] for e in range(2)], axis=0)
            o_ref[:, 256 * g + LANES * a2:256 * g + LANES * (a2 + 1)] = blk.T
        imp = p[:, 0:TQ] + p[:, TQ:2 * TQ] + p[:, 2 * TQ:3 * TQ] + p[:, 3 * TQ:4 * TQ]
        hi = imp.astype(BF16)
        r1 = imp - hi.astype(F32)
        mid = r1.astype(BF16)
        lo = (r1 - mid.astype(F32)).astype(BF16)
        isel = _dot(mt, hi) + _dot(mt, mid) + _dot(mt, lo)
        score = jnp.where(forced, FORCE_SCORE, isel)
        score = jnp.where(jidx <= cur, score, -jnp.inf)
        sel = _top_rows(score, jidx, min(SEL_TOPN, nsel)) & (jidx <= cur)
        sel_ref[g] = jnp.where(sel, 0.0, NEG)


def _pcmp(qu, ck, cv, cw, nb, t):
    n16 = t // CMP_STRIDE
    nsel = t // SEL_BLK
    nt = t // (TQ * PCMP_TILES)
    mt = np.zeros((nsel, n16), np.float32)
    for j in range(nsel):
        for c in range(4 * j - 1, 4 * j + 4):
            if 0 <= c < n16 - 1:
                mt[j, c] = 1.0
    mt = jnp.asarray(mt, BF16)
    const = lambda a: pl.BlockSpec(a.shape, lambda bi, ti: (0,) * a.ndim)
    wlist = [cw[k] for k in ("pak", "pbk", "wak", "wbk", "b1k", "w2k", "pav", "pbv", "wav", "wbv", "b1v", "w2v")] + [mt]
    return pl.pallas_call(
        functools.partial(_pcmp_body, n16=n16, nsel=nsel), name="pcmp",
        grid=(nb, nt),
        in_specs=[pl.BlockSpec((TQ * PCMP_TILES, 1024), lambda bi, ti: (bi * nt + ti, 0)),
                  pl.BlockSpec((None, t, LANES), lambda bi, ti: (bi, 0, 0)),
                  pl.BlockSpec((None, t, LANES), lambda bi, ti: (bi, 0, 0))] + [const(a) for a in wlist],
        out_specs=[pl.BlockSpec((TQ * PCMP_TILES, 512), lambda bi, ti: (bi * nt + ti, 0)),
                   pl.BlockSpec((None, PCMP_TILES, NSA_G, nsel, TQ), lambda bi, ti: (bi, ti, 0, 0, 0))],
        out_shape=[jax.ShapeDtypeStruct((nb * t, 512), F32),
                   jax.ShapeDtypeStruct((nb, t // TQ, NSA_G, nsel, TQ), F32)],
        scratch_shapes=[pltpu.VMEM((n16, LANES), BF16), pltpu.VMEM((LANES, n16), BF16)],
        compiler_params=_cparams(("parallel", "arbitrary")),
    )(qu, ck.reshape(nb, t, LANES), cv.reshape(nb, t, LANES), *wlist)


def _stack_queries(q, s):
    return jnp.concatenate([q[:, LANES * (4 * s + a):LANES * (4 * s + a + 1)] for a in range(4)], axis=0)


def _flash_out(z, s, o_ref):
    tq = z.shape[1] // 4
    for a2 in range(2):
        blk = jnp.concatenate([z[64 * s:64 * s + 64, tq * (2 * a2 + e):tq * (2 * a2 + e + 1)] for e in range(2)], axis=0)
        o_ref[:, 256 * s + LANES * a2:256 * s + LANES * (a2 + 1)] = blk.T


def _flash_body(*refs, mode, lam_init):
    if mode == "sel":
        q_ref, k_ref, vt_ref, selb_ref, o_ref, acc_ref, m_ref, l_ref, al_ref, s_ref, p_ref = refs
    else:
        (q_ref, k_ref, vt_ref, lq1_ref, lk1_ref, lq2_ref, lk2_ref, sg_ref, o_ref,
         acc_ref, m_ref, l_ref, al_ref, s_ref, p_ref) = refs
    ti = pl.program_id(1)
    tq = TQ * FLASH_TILES
    nq = 4 * tq
    q = q_ref[...]
    qs = [_stack_queries(q, s) for s in range(2)]
    koffs = [LANES * s if mode == "diff" else 0 for s in range(2)]
    m_ref[...] = jnp.full(m_ref.shape, NEG, F32)
    l_ref[...] = jnp.zeros(l_ref.shape, F32)
    acc_ref[...] = jnp.zeros(acc_ref.shape, F32)
    al_ref[...] = jnp.ones(al_ref.shape, F32)
    p_ref[...] = jnp.zeros(p_ref.shape, BF16)

    def scores(j, slot):
        k0 = pl.multiple_of(j * TK, TK)
        for s in range(2):
            s_ref[slot, s] = _dot_nt(k_ref[pl.ds(k0, TK), koffs[s]:koffs[s] + LANES], qs[s])

    def accumulate(j):
        k0 = pl.multiple_of(j * TK, TK)
        for s in range(2):
            acc_ref[s] = al_ref[s] * acc_ref[s] + _dot(vt_ref[koffs[s]:koffs[s] + LANES, pl.ds(k0, TK)], p_ref[s])

    def softmax(j, slot, causal):
        for s in range(2):
            st = s_ref[slot, s]
            if mode == "sel":
                rows = [jnp.broadcast_to(jnp.concatenate(
                    [selb_ref[u, s, pl.ds((TK // SEL_BLK) * j + r, 1), :] for u in range(FLASH_TILES)], axis=1), (SEL_BLK, tq))
                    for r in range(TK // SEL_BLK)]
                bias = jnp.concatenate(rows, axis=0)
                st = st + jnp.concatenate([bias, bias, bias, bias], axis=1)
            if causal:
                krow = lax.broadcasted_iota(jnp.int32, (TK, nq), 0)
                qcol = lax.broadcasted_iota(jnp.int32, (TK, nq), 1) & (tq - 1)
                st = jnp.where(ti * tq + qcol >= j * TK + krow, st, NEG)
            m_old = m_ref[s]
            m_new = jnp.maximum(m_old, jnp.max(st, axis=0, keepdims=True))
            alpha = jnp.exp2(m_old - m_new)
            p = jnp.exp2(st - m_new)
            l_ref[s] = alpha * l_ref[s] + jnp.sum(p, axis=0, keepdims=True)
            p_ref[s] = p.astype(BF16)
            al_ref[s] = alpha
            m_ref[s] = m_new

    n_full = (ti * tq) // TK
    scores(0, 0)

    def body(j, c):
        slot = j & 1
        accumulate(jnp.maximum(j - 1, 0))
        softmax(j, slot, False)
        scores(j + 1, 1 - slot)
        return c

    lax.fori_loop(0, n_full, body, 0)
    accumulate(jnp.maximum(n_full - 1, 0))
    softmax(n_full, n_full & 1, True)
    accumulate(n_full)

    if mode == "diff":
        lam = (jnp.exp(jnp.sum(lq1_ref[...] * lk1_ref[...], keepdims=True))
               - jnp.exp(jnp.sum(lq2_ref[...] * lk2_ref[...], keepdims=True)) + lam_init)
    for s in range(2):
        z = acc_ref[s] * (1.0 / l_ref[s])
        if mode == "diff":
            halves = []
            for hh in range(2):
                zr = z[64 * hh:64 * hh + 64, :]
                d = zr[:, tq * (2 * hh):tq * (2 * hh + 1)] - lam * zr[:, tq * (2 * hh + 1):tq * (2 * hh + 2)]
                ms = jnp.mean(d * d, axis=0, keepdims=True)
                halves.append(d * lax.rsqrt(ms + LN_EPS) * sg_ref[...] * (1.0 - lam_init))
            o_ref[:, LANES * s:LANES * (s + 1)] = jnp.concatenate(halves, axis=0).T
        else:
            _flash_out(z, s, o_ref)


FLASH_TILES = 1


def _flash(mode, q, k, vt, nb, t, extra=(), lam_init=0.0):
    tq = TQ * FLASH_TILES
    assert t % tq == 0 and t % TK == 0 and (tq % TK == 0 or TK % tq == 0)
    nt = t // tq
    kw = k.shape[-1]
    ow = 256 if mode == "diff" else 512
    in_specs = [pl.BlockSpec((tq, 1024), lambda bi, ti: (bi * nt + ti, 0)),
                pl.BlockSpec((None, t, kw), lambda bi, ti: (bi, 0, 0)),
                pl.BlockSpec((None, kw, t), lambda bi, ti: (bi, 0, 0))]
    if mode == "sel":
        nsel = t // SEL_BLK
        in_specs.append(pl.BlockSpec((None, FLASH_TILES, NSA_G, nsel, TQ), lambda bi, ti: (bi, ti, 0, 0, 0)))
    else:
        in_specs += [pl.BlockSpec(a.shape, lambda bi, ti: (0, 0)) for a in extra]
    return pl.pallas_call(
        functools.partial(_flash_body, mode=mode, lam_init=lam_init), name="flash_" + mode,
        grid=(nb, nt), in_specs=in_specs,
        out_specs=pl.BlockSpec((tq, ow), lambda bi, ti: (bi * nt + ti, 0)),
        out_shape=jax.ShapeDtypeStruct((nb * t, ow), F32),
        scratch_shapes=[pltpu.VMEM((2, LANES, 4 * tq), F32)] + [pltpu.VMEM((2, 1, 4 * tq), F32)] * 3
        + [pltpu.VMEM((2, 2, TK, 4 * tq), F32), pltpu.VMEM((2, TK, 4 * tq), BF16)],
        compiler_params=_cparams(("parallel", "parallel")),
    )(q, k.reshape(nb, t, kw), vt, *extra)


def _win_body(q_ref, k_ref, vt_ref, o_ref):
    nq = 4 * TQ
    span = WINDOW + TQ

    def tiles(interior):
        if interior:
            krow = lax.broadcasted_iota(jnp.int32, (TQ, nq), 0)
            qcol = lax.broadcasted_iota(jnp.int32, (TQ, nq), 1) & (TQ - 1)
        else:
            krow = lax.broadcasted_iota(jnp.int32, (span, nq), 0)
            qcol = lax.broadcasted_iota(jnp.int32, (span, nq), 1) & (TQ - 1)
        for u in range(WIN_TILES):
            ti = pl.program_id(1) * WIN_TILES + u
            k0 = pl.multiple_of(jnp.maximum(ti * TQ - WINDOW, 0), TQ)
            q = q_ref[TQ * u:TQ * (u + 1), :]
            kt = k_ref[pl.ds(k0, span), :]
            vt = vt_ref[:, pl.ds(k0, span)]
            for s in range(2):
                st = _dot_nt(kt, _stack_queries(q, s))
                if interior:
                    st = jnp.concatenate([jnp.where(krow > qcol, st[:TQ], NEG), st[TQ:WINDOW],
                                          jnp.where(krow <= qcol, st[WINDOW:], NEG)], axis=0)
                else:
                    dlt = ti * TQ + qcol - (k0 + krow)
                    st = jnp.where((dlt >= 0) & (dlt < WINDOW), st, NEG)
                m = jnp.max(st, axis=0, keepdims=True)
                p = jnp.exp2(st - m)
                l = jnp.sum(p, axis=0, keepdims=True)
                _flash_out(_dot(vt, p.astype(BF16)) * (1.0 / l), s, o_ref.at[TQ * u:TQ * (u + 1), :])

    first_interior = -(-WINDOW // (TQ * WIN_TILES))

    @pl.when(pl.program_id(1) >= first_interior)
    def _():
        tiles(True)

    @pl.when(pl.program_id(1) < first_interior)
    def _():
        tiles(False)


WIN_TILES = 4


def _win(q, k, vt, nb, t):
    nt = t // (TQ * WIN_TILES)
    assert t >= WINDOW + TQ and t % (TQ * WIN_TILES) == 0
    return pl.pallas_call(
        _win_body, name="flash_win", grid=(nb, nt),
        in_specs=[pl.BlockSpec((TQ * WIN_TILES, 1024), lambda bi, ti: (bi * nt + ti, 0)),
                  pl.BlockSpec((None, t, LANES), lambda bi, ti: (bi, 0, 0)),
                  pl.BlockSpec((None, LANES, t), lambda bi, ti: (bi, 0, 0))],
        out_specs=pl.BlockSpec((TQ * WIN_TILES, 512), lambda bi, ti: (bi * nt + ti, 0)),
        out_shape=jax.ShapeDtypeStruct((nb * t, 512), F32),
        compiler_params=_cparams(("parallel", "parallel")),
    )(q, k.reshape(nb, t, LANES), vt)


def _outproj_body(x_ref, yc_ref, oc_ref, os_ref, ow_ref, gt_ref, od_ref, e_ref, w_ref, g_ref, b_ref, o_ref, *, alpha):
    gt = gt_ref[...]
    hi = gt.astype(BF16)
    lo = (gt - hi.astype(F32)).astype(BF16)
    e = e_ref[...]
    gx = _dot(hi, e) + _dot(lo, e)
    onsa = gx[:, 0:512] * oc_ref[...] + gx[:, 512:1024] * os_ref[...] + gx[:, 1024:1536] * ow_ref[...]
    mix = (_dot(yc_ref[...].astype(BF16), w_ref[0:256, :]) + _dot(onsa.astype(BF16), w_ref[256:768, :])
           + _dot(od_ref[...].astype(BF16), w_ref[768:1024, :]))
    o_ref[...] = _ln_rows(alpha * x_ref[...] + mix, g_ref[...], b_ref[...])


def _outproj(x, yc, oc, osel, ow, gt, od, e, w, g, b, alpha, tm):
    n, d = x.shape
    row = lambda c: pl.BlockSpec((tm, c), lambda i: (i, 0))
    const = lambda a: pl.BlockSpec(a.shape, lambda i: (0, 0))
    return pl.pallas_call(
        functools.partial(_outproj_body, alpha=alpha), name="outproj",
        grid=(n // tm,),
        in_specs=[row(d), row(256), row(512), row(512), row(512), row(128), row(256), const(e), const(w), const(g), const(b)],
        out_specs=row(d), out_shape=jax.ShapeDtypeStruct((n, d), F32),
        compiler_params=_cparams(("parallel",)),
    )(x, yc, oc, osel, ow, gt, od, e, w, g, b)


def _ffn_body(x_ref, w1_ref, w3_ref, w2_ref, g_ref, b_ref, o_ref, acc_ref, *, alpha):
    f = pl.program_id(1)
    xb = x_ref[...].astype(BF16)
    h1 = _dot(xb, w1_ref[...])
    h = h1 * _sigmoid(h1) * _dot(xb, w3_ref[...])
    part = _dot(h.astype(BF16), w2_ref[...])

    @pl.when(f == 0)
    def _():
        acc_ref[...] = part

    @pl.when(f != 0)
    def _():
        acc_ref[...] += part

    @pl.when(f == pl.num_programs(1) - 1)
    def _():
        o_ref[...] = _ln_rows(alpha * x_ref[...] + acc_ref[...], g_ref[...], b_ref[...])


def _ffn(x, w1, w3, w2, g, b, alpha, tm, nf):
    n, d = x.shape
    dff = w1.shape[1]
    tf = dff // nf
    return pl.pallas_call(
        functools.partial(_ffn_body, alpha=alpha), name="ffn",
        grid=(n // tm, nf),
        in_specs=[pl.BlockSpec((tm, d), lambda i, f: (i, 0)),
                  pl.BlockSpec((d, tf), lambda i, f: (0, f)),
                  pl.BlockSpec((d, tf), lambda i, f: (0, f)),
                  pl.BlockSpec((tf, d), lambda i, f: (f, 0)),
                  pl.BlockSpec((1, d), lambda i, f: (0, 0)),
                  pl.BlockSpec((1, d), lambda i, f: (0, 0))],
        out_specs=pl.BlockSpec((tm, d), lambda i, f: (i, 0)),
        out_shape=jax.ShapeDtypeStruct((n, d), F32),
        scratch_shapes=[pltpu.VMEM((tm, d), F32)],
        compiler_params=_cparams(("parallel", "arbitrary")),
    )(x, w1, w3, w2, g, b)


def _gather_pages(pt_ref, n_pages, srcs, sem, layer):
    b = pl.program_id(0)
    slot = b & 1

    def copies(bb, sl):
        return [pltpu.make_async_copy(hbm.at[layer, pt_ref[bb, j]], buf.at[sl, :, pl.ds(PAGE * j, PAGE)], sem.at[a, sl])
                for a, (hbm, buf) in enumerate(srcs) for j in range(n_pages)]

    @pl.when(b == 0)
    def _():
        for c in copies(0, 0):
            c.start()

    @pl.when(b + 1 < pl.num_programs(0))
    def _():
        for c in copies(b + 1, 1 - slot):
            c.start()

    for c in copies(b, slot):
        c.wait()
    return slot


def _key_chunks(n):
    per = max(d for d in range(1, 17) if n % d == 0)
    return [(PAGE * per * c, PAGE * per) for c in range(n // per)]


def _stack_heads(q):
    return jnp.concatenate([q[:, LANES * h:LANES * (h + 1)] for h in range(8)], axis=0)


def _scmp_body(pt_ref, qu_ref, ck_hbm, cv_hbm, pak_ref, pbk_ref, wak_ref, wbk_ref, b1k_ref, w2k_ref,
               pav_ref, pbv_ref, wav_ref, wbv_ref, b1v_ref, w2v_ref, m_ref,
               o_ref, sel_ref, kbt, vbt, kbuf, vbuf, sem, *, layer, n_pages, past, nq):
    slot = _gather_pages(pt_ref, n_pages, [(ck_hbm, kbt), (cv_hbm, vbt)], sem, layer)
    for j in range(n_pages):
        kbuf[PAGE * j:PAGE * (j + 1), :] = kbt[slot, :, PAGE * j:PAGE * (j + 1)].T
        vbuf[PAGE * j:PAGE * (j + 1), :] = vbt[slot, :, PAGE * j:PAGE * (j + 1)].T
    n16 = past // CMP_STRIDE
    kcc = _compress(kbuf, n16, pak_ref[...], pbk_ref[...], wak_ref, wbk_ref, b1k_ref[...], w2k_ref).astype(BF16)
    vcc = _compress(vbuf, n16, pav_ref[...], pbv_ref[...], wav_ref, wbv_ref, b1v_ref[...], w2v_ref).astype(BF16)
    qa = _stack_heads(qu_ref[...])
    rows = 8 * nq
    s = _dot_nt(qa, kcc)
    cidx = lax.broadcasted_iota(jnp.int32, (rows, n16), 1)
    qpos = past + (lax.broadcasted_iota(jnp.int32, (rows, n16), 0) & (nq - 1))
    vis = (CMP_STRIDE * cidx + CMP_LEN - 1 <= qpos) & (cidx < n16 - 1)
    s = jnp.where(vis, s, NEG)
    m = jnp.max(s, axis=-1, keepdims=True)
    p = jnp.where(vis, jnp.exp2(s - m), 0.0)
    l = jnp.sum(p, axis=-1, keepdims=True)
    p = p / jnp.maximum(l, 1e-30)
    o_ref[...] = _dot(p.astype(BF16), vcc)
    imp = jnp.concatenate(
        [p[nq * 4 * g:nq * (4 * g + 1)] + p[nq * (4 * g + 1):nq * (4 * g + 2)]
         + p[nq * (4 * g + 2):nq * (4 * g + 3)] + p[nq * (4 * g + 3):nq * (4 * g + 4)] for g in range(NSA_G)], axis=0)
    isel = _split3_dot(imp, m_ref[...])
    nselp = isel.shape[1]
    jidx = lax.broadcasted_iota(jnp.int32, (NSA_G * nq, nselp), 1)
    qp = past + (lax.broadcasted_iota(jnp.int32, (NSA_G * nq, nselp), 0) & (nq - 1))
    cur = qp >> 6
    forced = (jidx == 0) | (jidx == cur) | (jidx == cur - 1)
    score = jnp.where(forced, FORCE_SCORE, isel)
    sel_ref[...] = jnp.where(jidx <= cur, score, -jnp.inf)


def _stopk_body(score_ref, sel_ref, *, k, past, nq):
    score = score_ref[...]
    nselp = score.shape[1]
    jidx = lax.broadcasted_iota(jnp.int32, score.shape, 1)
    cur = (past + (lax.broadcasted_iota(jnp.int32, score.shape, 0) & (nq - 1))) >> 6
    sel = jnp.zeros(score.shape, F32)
    for _ in range(k):
        mx = jnp.max(score, axis=-1, keepdims=True)
        first = jnp.min(jnp.where(score == mx, jidx, nselp), axis=-1, keepdims=True)
        hit = jidx == first
        sel = jnp.where(hit, 1.0, sel)
        score = jnp.where(hit, -jnp.inf, score)
    sel_ref[...] = jnp.where(jidx <= cur, sel, 0.0)


def _stopk(score, past, nq):
    nb, r, nselp = score.shape
    n_sel = -(-(past + nq) // SEL_BLK)
    full = pl.BlockSpec((nb * r, nselp), lambda i: (0, 0))
    sel = pl.pallas_call(
        functools.partial(_stopk_body, k=min(SEL_TOPN, n_sel), past=past, nq=nq), name="stopk",
        grid=(1,), in_specs=[full], out_specs=full,
        out_shape=jax.ShapeDtypeStruct((nb * r, nselp), F32),
        compiler_params=_cparams(("arbitrary",)),
    )(score.reshape(nb * r, nselp))
    return sel.reshape(nb, r, nselp)


def _scmp(pt, qu, cache_k, cache_v, cw, layer, nb, nq, past):
    n_pages = past // PAGE
    n16 = past // CMP_STRIDE
    n_sel = -(-(past + nq) // SEL_BLK)
    nselp = -(-n_sel // LANES) * LANES
    m = np.zeros((n16, nselp), np.float32)
    for j in range(n_sel):
        for c in range(4 * j - 1, 4 * j + 4):
            if 0 <= c < n16 - 1:
                m[c, j] = 1.0
    m = jnp.asarray(m, BF16)
    wlist = [cw[k] for k in ("pak", "pbk", "wak", "wbk", "b1k", "w2k", "pav", "pbv", "wav", "wbv", "b1v", "w2v")] + [m]
    const = lambda a: pl.BlockSpec(a.shape, lambda bi, pt_: (0,) * a.ndim)
    any_spec = pl.BlockSpec(memory_space=pl.ANY)
    grid_spec = pltpu.PrefetchScalarGridSpec(
        num_scalar_prefetch=1, grid=(nb,),
        in_specs=[pl.BlockSpec((nq, 1024), lambda bi, pt_: (bi, 0)), any_spec, any_spec] + [const(a) for a in wlist],
        out_specs=[pl.BlockSpec((None, 8 * nq, LANES), lambda bi, pt_: (bi, 0, 0)),
                   pl.BlockSpec((None, NSA_G * nq, nselp), lambda bi, pt_: (bi, 0, 0))],
        scratch_shapes=[pltpu.VMEM((2, LANES, past), F32), pltpu.VMEM((2, LANES, past), F32),
                        pltpu.VMEM((past, LANES), F32), pltpu.VMEM((past, LANES), F32), pltpu.SemaphoreType.DMA((2, 2))])
    return pl.pallas_call(
        functools.partial(_scmp_body, layer=layer, n_pages=n_pages, past=past, nq=nq), name="scmp",
        grid_spec=grid_spec,
        out_shape=[jax.ShapeDtypeStruct((nb, 8 * nq, LANES), F32), jax.ShapeDtypeStruct((nb, NSA_G * nq, nselp), F32)],
        compiler_params=_cparams(("arbitrary",)),
    )(pt, qu, cache_k, cache_v, *wlist)


def _softmax_rows(s):
    m = jnp.max(s, axis=-1, keepdims=True)
    p = jnp.exp2(s - m)
    return p, jnp.sum(p, axis=-1, keepdims=True)


def _sselwin_body(pt_ref, qr_ref, sel_ref, skn_ref, svn_ref, wkn_ref, wvn_ref, wks_ref, wvs_ref, e_ref, sk_hbm, sv_hbm,
                  osel_ref, owin_ref, kbuf, vbuf, sem, *, layer, n_pages, past, nq):
    ktot = past + PAGE
    rows = 8 * nq
    slot = _gather_pages(pt_ref, n_pages, [(sk_hbm, kbuf), (sv_hbm, vbuf)], sem, layer)
    pad = jnp.zeros((PAGE - nq, LANES), F32)
    kbuf[slot, :, pl.ds(past, PAGE)] = jnp.concatenate([skn_ref[...], pad], axis=0).T
    vbuf[slot, :, pl.ds(past, PAGE)] = jnp.concatenate([svn_ref[...], pad], axis=0).T
    chunks = _key_chunks(n_pages + 1)
    qa = _stack_heads(qr_ref[...])
    s = jnp.concatenate([_dot(qa, kbuf[slot, :, pl.ds(c0, cn)].astype(BF16)) for c0, cn in chunks], axis=1)
    se = _dot(sel_ref[...].astype(BF16), e_ref[...])
    se = jnp.concatenate([se[nq * g:nq * (g + 1)] for g in range(NSA_G) for _ in range(NSA_HPG)], axis=0)
    kpos = lax.broadcasted_iota(jnp.int32, (rows, ktot), 1)
    qpos = past + (lax.broadcasted_iota(jnp.int32, (rows, ktot), 0) & (nq - 1))
    s = jnp.where((se > 0.5) & (kpos <= qpos), s, NEG)
    p, l = _softmax_rows(s)
    pb = p.astype(BF16)
    o = _dot_nt(pb[:, 0:chunks[0][1]], vbuf[slot, :, pl.ds(0, chunks[0][1])].astype(BF16))
    for c0, cn in chunks[1:]:
        o = o + _dot_nt(pb[:, c0:c0 + cn], vbuf[slot, :, pl.ds(c0, cn)].astype(BF16))
    osel_ref[...] = o / l
    wb = wks_ref.shape[0]
    kw = jnp.concatenate([wks_ref[...], wkn_ref[...], pad], axis=0).astype(BF16)
    vw = jnp.concatenate([wvs_ref[...], wvn_ref[...], pad], axis=0).astype(BF16)
    s = _dot_nt(qa, kw)
    i = lax.broadcasted_iota(jnp.int32, (rows, wb + PAGE), 1)
    kp = jnp.where(i < wb, past - wb + i, past + i - wb)
    qp = past + (lax.broadcasted_iota(jnp.int32, (rows, wb + PAGE), 0) & (nq - 1))
    dlt = qp - kp
    s = jnp.where((dlt >= 0) & (dlt < WINDOW) & (i < wb + nq), s, NEG)
    p, l = _softmax_rows(s)
    owin_ref[...] = _dot(p.astype(BF16), vw) / l


def _sselwin(pt, qr, sel, skn, svn, wkn, wvn, wks, wvs, cache_k, cache_v, layer, nb, nq, past):
    n_pages = past // PAGE
    ktot = past + PAGE
    nselp = sel.shape[-1]
    e = (np.arange(ktot)[None, :] // SEL_BLK == np.arange(nselp)[:, None]).astype(np.float32)
    e = jnp.asarray(e, BF16)
    wb = wks.shape[1]
    any_spec = pl.BlockSpec(memory_space=pl.ANY)
    new = pl.BlockSpec((nq, LANES), lambda bi, pt_: (bi, 0))
    grid_spec = pltpu.PrefetchScalarGridSpec(
        num_scalar_prefetch=1, grid=(nb,),
        in_specs=[pl.BlockSpec((nq, 1024), lambda bi, pt_: (bi, 0)),
                  pl.BlockSpec((None, NSA_G * nq, nselp), lambda bi, pt_: (bi, 0, 0)),
                  new, new, new, new,
                  pl.BlockSpec((None, wb, LANES), lambda bi, pt_: (bi, 0, 0)),
                  pl.BlockSpec((None, wb, LANES), lambda bi, pt_: (bi, 0, 0)),
                  pl.BlockSpec(e.shape, lambda bi, pt_: (0, 0)), any_spec, any_spec],
        out_specs=[pl.BlockSpec((None, 8 * nq, LANES), lambda bi, pt_: (bi, 0, 0)),
                   pl.BlockSpec((None, 8 * nq, LANES), lambda bi, pt_: (bi, 0, 0))],
        scratch_shapes=[pltpu.VMEM((2, LANES, ktot), F32), pltpu.VMEM((2, LANES, ktot), F32), pltpu.SemaphoreType.DMA((2, 2))])
    return pl.pallas_call(
        functools.partial(_sselwin_body, layer=layer, n_pages=n_pages, past=past, nq=nq), name="sselwin",
        grid_spec=grid_spec,
        out_shape=[jax.ShapeDtypeStruct((nb, 8 * nq, LANES), F32)] * 2,
        compiler_params=_cparams(("arbitrary",)),
    )(pt, qr, sel, skn, svn, wkn, wvn, wks, wvs, e, cache_k, cache_v)


def _sdiff_body(pt_ref, dq_ref, dkn_ref, dvn_ref, lq1_ref, lk1_ref, lq2_ref, lk2_ref, sg_ref, dk_hbm, dv_hbm,
                o_ref, kbuf, vbuf, sem, *, layer, n_pages, past, nq, lam_init):
    ktot = past + PAGE
    rows = 8 * nq
    slot = _gather_pages(pt_ref, n_pages, [(dk_hbm, kbuf), (dv_hbm, vbuf)], sem, layer)
    pad = jnp.zeros((PAGE - nq, 256), F32)
    kbuf[slot, :, pl.ds(past, PAGE)] = jnp.concatenate([dkn_ref[...], pad], axis=0).T
    vbuf[slot, :, pl.ds(past, PAGE)] = jnp.concatenate([dvn_ref[...], pad], axis=0).T
    chunks = _key_chunks(n_pages + 1)
    q = dq_ref[...]
    zero = jnp.zeros((nq, LANES), BF16)
    blocks = []
    for a in range(8):
        chunk = q[:, LANES * a:LANES * (a + 1)]
        blocks.append(jnp.concatenate([chunk, zero] if a < 4 else [zero, chunk], axis=1))
    qa = jnp.concatenate(blocks, axis=0)
    s = jnp.concatenate([_dot(qa, kbuf[slot, :, pl.ds(c0, cn)].astype(BF16)) for c0, cn in chunks], axis=1)
    kpos = lax.broadcasted_iota(jnp.int32, (rows, ktot), 1)
    qpos = past + (lax.broadcasted_iota(jnp.int32, (rows, ktot), 0) & (nq - 1))
    s = jnp.where(kpos <= qpos, s, NEG)
    p, l = _softmax_rows(s)
    pb = p.astype(BF16)
    o = _dot_nt(pb[:, 0:chunks[0][1]], vbuf[slot, :, pl.ds(0, chunks[0][1])].astype(BF16))
    for c0, cn in chunks[1:]:
        o = o + _dot_nt(pb[:, c0:c0 + cn], vbuf[slot, :, pl.ds(c0, cn)].astype(BF16))
    o = o / l
    lam = (jnp.exp(jnp.sum(lq1_ref[...] * lk1_ref[...], keepdims=True))
           - jnp.exp(jnp.sum(lq2_ref[...] * lk2_ref[...], keepdims=True)) + lam_init)
    lane = lax.broadcasted_iota(jnp.int32, (nq, 256), 1)
    out = jnp.zeros((nq, 256), F32)
    for h in range(DIFF_H):
        d = o[2 * nq * h:2 * nq * h + nq] - lam * o[2 * nq * h + nq:2 * nq * (h + 1)]
        inh = (lane >> 6) == h
        ms = jnp.sum(jnp.where(inh, d * d, 0.0), axis=-1, keepdims=True) * (1.0 / DIFF_DV)
        out = out + jnp.where(inh, d * lax.rsqrt(ms + LN_EPS), 0.0)
    o_ref[...] = out * sg_ref[...] * (1.0 - lam_init)


def _sdiff(pt, dq, dkn, dvn, lams, sg_row, cache_k, cache_v, layer, nb, nq, past, lam_init):
    n_pages = past // PAGE
    ktot = past + PAGE
    any_spec = pl.BlockSpec(memory_space=pl.ANY)
    new = pl.BlockSpec((nq, 256), lambda bi, pt_: (bi, 0))
    small = [pl.BlockSpec(a.shape, lambda bi, pt_: (0, 0)) for a in (*lams, sg_row)]
    grid_spec = pltpu.PrefetchScalarGridSpec(
        num_scalar_prefetch=1, grid=(nb,),
        in_specs=[pl.BlockSpec((nq, 1024), lambda bi, pt_: (bi, 0)), new, new] + small + [any_spec, any_spec],
        out_specs=pl.BlockSpec((nq, 256), lambda bi, pt_: (bi, 0)),
        scratch_shapes=[pltpu.VMEM((2, 256, ktot), F32), pltpu.VMEM((2, 256, ktot), F32), pltpu.SemaphoreType.DMA((2, 2))])
    return pl.pallas_call(
        functools.partial(_sdiff_body, layer=layer, n_pages=n_pages, past=past, nq=nq, lam_init=lam_init), name="sdiff",
        grid_spec=grid_spec,
        out_shape=jax.ShapeDtypeStruct((nb * nq, 256), F32),
        compiler_params=_cparams(("arbitrary",)),
    )(pt, dq, dkn, dvn, *lams, sg_row, cache_k, cache_v)


def _prep_w_in(w):
    pts = np.cumsum([256, 256, 512, 128, 128, 128, 128, 128, 128, 24, 256, 256, 256])[:-1].tolist()
    ca, cg, nq, ck, cv, sk, sv, wk, wv, gt, dq, dk, dv = jnp.split(w, pts, axis=1)
    gtp = jnp.pad(gt, ((0, 0), (0, LANES - gt.shape[1])))
    return jnp.concatenate([ca, cg, nq * (NSA_DH ** -0.5 * LOG2E), ck, cv, sk, sv, wk, wv, gtp, dq, dk, dv], axis=1).astype(BF16)


def _rope_tables(pos):
    out = []
    lane = np.arange(LANES)
    for half in (32, 16):
        inv = ROPE_THETA ** (-jnp.arange(half, dtype=F32) / half)
        ang = pos.astype(F32)[:, None] * inv[None, :]
        idx = lane % half
        sign = jnp.asarray(np.where(lane % (2 * half) < half, -1.0, 1.0), F32)
        out += [jnp.cos(ang)[:, idx], jnp.sin(ang)[:, idx] * sign[None, :]]
    return out


def _prep_cmp(pe, w1, b1, w2):
    def halves(x):
        res = []
        for part in (x[:16 * NSA_DH], x[16 * NSA_DH:]):
            p4 = part.reshape(CMP_STRIDE, 1, NSA_DH, 1, -1)
            eye = jnp.eye(NSA_G, dtype=F32)[None, :, None, :, None]
            res.append((p4 * eye).reshape(CMP_STRIDE * NSA_G * NSA_DH, NSA_G * part.shape[-1]))
        return res
    wa, wb = halves(w1)
    pea = jnp.tile(pe[:16, None, :], (1, NSA_G, 1)).reshape(1, -1)
    peb = jnp.tile(pe[16:, None, :], (1, NSA_G, 1)).reshape(1, -1)
    w2bd = (w2[None, :, None, :] * jnp.eye(NSA_G, dtype=F32)[:, None, :, None]).reshape(NSA_G * CMP_HID, NSA_G * NSA_DH)
    return pea, peb, wa.astype(BF16), wb.astype(BF16), jnp.tile(b1, NSA_G)[None, :], w2bd.astype(BF16)


def _gate_expand():
    e = np.zeros((LANES, 3 * 512), np.float32)
    for h in range(NSA_H):
        for br in range(3):
            e[3 * h + br, 512 * br + 64 * h:512 * br + 64 * (h + 1)] = 1.0
    return jnp.asarray(e, BF16)


def _rows_to_tokens(o, nb, nq):
    o6 = o.reshape(nb, NSA_G, NSA_HPG, nq, NSA_G, NSA_DH)
    pick = jnp.stack([o6[:, g, :, :, g, :] for g in range(NSA_G)], axis=1)
    return pick.transpose(0, 3, 1, 2, 4).reshape(nb * nq, NSA_H * NSA_DH)


def _prompt_layer(x, lw, tabs, nb, t, lam_init, alpha):
    n = nb * t
    (u, qu, qr, gt, dq, ck, cv, skb, wkb, dkb, svb, wvb, dvb,
     ckt, cvt, skt, svt, wkt, wvt, dkt, dvt) = _inproj(x, lw["w_in"], tabs, nb, t, 256, True)
    ext = jnp.pad(u.reshape(nb, t, 256), ((0, 0), (CONV_PAD, 0), (0, 0)))
    yc = _conv(ext, lw["dw_w"], lw["dw_b"], lw["cln_g"], lw["cln_b"], t, 256).reshape(n, 256)
    ocmp, sel = _pcmp(qu, ck, cv, lw["cmp"], nb, t)
    osel = _flash("sel", qr, skb, svb, nb, t, extra=(sel,))
    owin = _win(qr, wkb, wvb, nb, t)
    odiff = _flash("diff", dq, dkb, dvb, nb, t, extra=lw["lams"] + (lw["sg_col"],), lam_init=lam_init)
    x1 = _outproj(x, yc, ocmp, osel, owin, gt, odiff, lw["gate_e"], lw["w_out"], lw["ln1_g"], lw["ln1_b"], alpha, 256)
    x2 = _ffn(x1, lw["w1"], lw["w3"], lw["w2"], lw["ln2_g"], lw["ln2_b"], alpha, 512, 2)
    nk = min(WINDOW, t)
    rows_major = lambda a: a.reshape(nb, a.shape[1] // 64, 64, a.shape[2]).transpose(0, 3, 1, 2)
    news = (rows_major(ckt), rows_major(cvt), rows_major(skt), rows_major(svt), rows_major(dkt), rows_major(dvt),
            rows_major(wkt[:, :, t - nk:]), rows_major(wvt[:, :, t - nk:]),
            u.reshape(nb, t, 256)[:, t - (CONV_W - 1):])
    return x2, news


def _sample_layer(x, lw, tabs, caches, states, pt, layer, nb, nq, past, lam_init, alpha):
    n = nb * nq
    (u, qu, qr, gt, dq, ck, cv, sk, sv, wk, wv, dk, dv) = _inproj(x, lw["w_in"], tabs, nb, nq, n, False)
    c_cmp_k, c_cmp_v, c_sel_k, c_sel_v, c_diff_k, c_diff_v = caches
    st_wk, st_wv, st_conv = states
    ext = jnp.concatenate([jnp.zeros((nb, CONV_PAD - (CONV_W - 1), 256), F32), st_conv, u.reshape(nb, nq, 256)], axis=1)
    yc = _conv(ext, lw["dw_w"], lw["dw_b"], lw["cln_g"], lw["cln_b"], nq, nq).reshape(n, 256)
    ocmp, score = _scmp(pt, qu, c_cmp_k, c_cmp_v, lw["cmp"], layer, nb, nq, past)
    sel = _stopk(score, past, nq)
    wb = st_wk.shape[1]
    osel, owin = _sselwin(pt, qr, sel, sk, sv, wk, wv, st_wk.reshape(nb, wb, LANES), st_wv.reshape(nb, wb, LANES),
                          c_sel_k, c_sel_v, layer, nb, nq, past)
    odiff = _sdiff(pt, dq, dk, dv, lw["lams"], lw["sg_row"], c_diff_k, c_diff_v, layer, nb, nq, past, lam_init)
    x1 = _outproj(x, yc, _rows_to_tokens(ocmp, nb, nq), _rows_to_tokens(osel, nb, nq), _rows_to_tokens(owin, nb, nq),
                  gt, odiff, lw["gate_e"], lw["w_out"], lw["ln1_g"], lw["ln1_b"], alpha, n)
    x2 = _ffn(x1, lw["w1"], lw["w3"], lw["w2"], lw["ln2_g"], lw["ln2_b"], alpha, n, 2)
    new_wk = jnp.concatenate([st_wk, wk.reshape(nb, nq, NSA_G, NSA_DH)], axis=1)[:, -wb:]
    new_wv = jnp.concatenate([st_wv, wv.reshape(nb, nq, NSA_G, NSA_DH)], axis=1)[:, -wb:]
    new_conv = jnp.concatenate([st_conv, u.reshape(nb, nq, 256)], axis=1)[:, -(CONV_W - 1):]
    news = (ck.reshape(nb, nq, NSA_G, NSA_DH), cv.reshape(nb, nq, NSA_G, NSA_DH),
            sk.reshape(nb, nq, NSA_G, NSA_DH), sv.reshape(nb, nq, NSA_G, NSA_DH),
            dk.reshape(nb, nq, DIFF_H, 2 * DIFF_DQK), dv.reshape(nb, nq, DIFF_H, DIFF_DV),
            new_wk, new_wv, new_conv)
    return x2, news


def kernel(x_prompt, x_sample, cache_nsa_cmp_k, cache_nsa_cmp_v, cache_nsa_sel_k, cache_nsa_sel_v, cache_diff_k, cache_diff_v, state_nsa_win_k, state_nsa_win_v, state_conv, page_table, w_in, conv_dw_w, conv_dw_b, conv_ln_g, conv_ln_b, cmp_pe_k, cmp_w1_k, cmp_b1_k, cmp_w2_k, cmp_pe_v, cmp_w1_v, cmp_b1_v, cmp_w2_v, diff_lq1, diff_lk1, diff_lq2, diff_lk2, diff_subln_g, w_out, ln1_g, ln1_b, ln2_g, ln2_b, ffn_w1, ffn_w3, ffn_w2):
    nb, t, d = x_prompt.shape
    sb, nq, _ = x_sample.shape
    depth = w_in.shape[0]
    n_pool = cache_nsa_cmp_k.shape[1]
    past = page_table.shape[1] * PAGE
    alpha = (2 * depth) ** 0.25
    tabs_p = _rope_tables(jnp.arange(t, dtype=jnp.int32))
    tabs_s = _rope_tables(jnp.tile(past + jnp.arange(nq, dtype=jnp.int32), sb))
    as_pages = lambda c: c.transpose(0, 1, 3, 4, 2).reshape(depth, n_pool, c.shape[3] * c.shape[4], PAGE)
    caches = tuple(as_pages(c) for c in (cache_nsa_cmp_k, cache_nsa_cmp_v, cache_nsa_sel_k, cache_nsa_sel_v,
                                         cache_diff_k, cache_diff_v))
    gate_e = _gate_expand()
    xp = x_prompt.reshape(nb * t, d)
    xs = x_sample.reshape(sb * nq, d)
    outs_p, outs_s = [], []
    for l in range(depth):
        ck = _prep_cmp(cmp_pe_k[l], cmp_w1_k[l], cmp_b1_k[l], cmp_w2_k[l])
        cv = _prep_cmp(cmp_pe_v[l], cmp_w1_v[l], cmp_b1_v[l], cmp_w2_v[l])
        names = ("pa", "pb", "wa", "wb", "b1", "w2")
        cmpw = {n_ + "k": a for n_, a in zip(names, ck)}
        cmpw.update({n_ + "v": a for n_, a in zip(names, cv)})
        lw = dict(
            w_in=_prep_w_in(w_in[l]), dw_w=conv_dw_w[l], dw_b=conv_dw_b[l][None], cln_g=conv_ln_g[l][None],
            cln_b=conv_ln_b[l][None], cmp=cmpw,
            lams=(diff_lq1[l][None], diff_lk1[l][None], diff_lq2[l][None], diff_lk2[l][None]),
            sg_col=diff_subln_g[l][:, None], sg_row=jnp.tile(diff_subln_g[l], DIFF_H)[None],
            gate_e=gate_e, w_out=w_out[l].astype(BF16), ln1_g=ln1_g[l][None], ln1_b=ln1_b[l][None],
            ln2_g=ln2_g[l][None], ln2_b=ln2_b[l][None],
            w1=ffn_w1[l].astype(BF16), w3=ffn_w3[l].astype(BF16), w2=ffn_w2[l].astype(BF16))
        lam_init = 0.8 - 0.6 * math.exp(-0.3 * l)
        xp, new_p = _prompt_layer(xp, lw, tabs_p, nb, t, lam_init, alpha)
        xs, new_s = _sample_layer(xs, lw, tabs_s, caches, (state_nsa_win_k[l], state_nsa_win_v[l], state_conv[l]),
                                  page_table, l, sb, nq, past, lam_init, alpha)
        outs_p.append(new_p)
        outs_s.append(new_s)
    stk_p = [jnp.stack([o[i] for o in outs_p]) for i in range(9)]
    stk_s = [jnp.stack([o[i] for o in outs_s]) for i in range(9)]
    return (xp.reshape(nb, t, d), xs.reshape(sb, nq, d), *stk_p, *stk_s)
```

```python
import functools
import math

import jax
import jax.numpy as jnp
import numpy as np
from jax import lax
from jax.experimental import pallas as pl
from jax.experimental.pallas import tpu as pltpu

F32 = jnp.float32
BF16 = jnp.bfloat16

CONV_W = 31
NSA_H = 8
NSA_G = 2
NSA_HPG = NSA_H // NSA_G
NSA_DH = 64
CMP_STRIDE = 16
CMP_LEN = 32
CMP_HID = 128
SEL_BLK = 64
SEL_TOPN = 16
WINDOW = 512
FORCE_SCORE = 1e9
DIFF_H = 4
DIFF_DV = 64
DIFF_DQK = 32
ROPE_THETA = 10000.0
LN_EPS = 1e-5
PAGE = 128

LANES = 128
TQ = 128
TK = 256
LOG2E = 1.4426950408889634
NEG = -1e30
VMEM_LIMIT = 52 * 1024 * 1024

C_CA, C_CG, C_Q, C_KV, C_GT, C_DQ, C_DK, C_DV, C_END = 0, 256, 512, 1024, 1792, 1920, 2176, 2432, 2688


def _cparams(sem):
    return pltpu.CompilerParams(dimension_semantics=sem, vmem_limit_bytes=VMEM_LIMIT)


def _sigmoid(x):
    return 1.0 / (1.0 + jnp.exp(-x))


def _ln_rows(x, g, b):
    mu = jnp.mean(x, axis=-1, keepdims=True)
    xc = x - mu
    var = jnp.mean(xc * xc, axis=-1, keepdims=True)
    return xc * lax.rsqrt(var + LN_EPS) * g + b


def _dot(a, b):
    return jnp.dot(a, b, preferred_element_type=F32)


def _dot_nt(a, b):
    return lax.dot_general(a, b, (((1,), (1,)), ((), ())), preferred_element_type=F32)


def _split3_dot(x, m):
    hi = x.astype(BF16)
    r1 = x - hi.astype(F32)
    mid = r1.astype(BF16)
    lo = (r1 - mid.astype(F32)).astype(BF16)
    return _dot(hi, m) + _dot(mid, m) + _dot(lo, m)


def _rope(x, cos, sin_signed, half):
    lane = lax.broadcasted_iota(jnp.int32, x.shape, 1)
    first = (lane & (2 * half - 1)) < half
    rot = jnp.where(first, pltpu.roll(x, LANES - half, 1), pltpu.roll(x, half, 1))
    return x * cos + rot * sin_signed


def _inproj_body(x_ref, w_ref, c64_ref, s64_ref, c32_ref, s32_ref,
                 u_ref, qu_ref, qr_ref, gt_ref, dq_ref, ck_ref, cv_ref, *refs, dq_scale, transposed):
    xb = x_ref[...].astype(BF16)

    def mm(lo, hi):
        return _dot(xb, w_ref[:, lo:hi])

    c64, s64, c32, s32 = c64_ref[...], s64_ref[...], c32_ref[...], s32_ref[...]
    z = mm(C_CA, C_Q)
    u_ref[...] = z[:, :256] * _sigmoid(z[:, 256:])
    lane = lax.broadcasted_iota(jnp.int32, (x_ref.shape[0], LANES), 1)
    for j in range(4):
        zq = mm(C_Q + LANES * j, C_Q + LANES * (j + 1))
        g = j // 2
        keep = (lane < 64) if g == 0 else (lane >= 64)
        for src, dst_ref in ((zq, qu_ref), (_rope(zq, c64, s64, 32), qr_ref)):
            for e in range(2):
                v = src if e == g else pltpu.roll(src, 64, 1)
                dst_ref[:, LANES * (2 * j + e):LANES * (2 * j + e + 1)] = jnp.where(keep, v, 0.0).astype(BF16)
    z = mm(C_KV, C_GT)
    ck = z[:, 0:128]
    cv = z[:, 128:256]
    ck_ref[...] = ck
    cv_ref[...] = cv
    sk = _rope(z[:, 256:384], c64, s64, 32)
    sv = z[:, 384:512]
    wk = _rope(z[:, 512:640], c64, s64, 32)
    wv = z[:, 640:768]
    gt_ref[...] = _sigmoid(mm(C_GT, C_DQ))
    for c in range(2):
        zq = _rope(mm(C_DQ + LANES * c, C_DQ + LANES * (c + 1)), c32, s32, 16) * dq_scale
        for a in range(4):
            dq_ref[:, LANES * (4 * c + a):LANES * (4 * c + a + 1)] = jnp.where((lane >> 5) == a, zq, 0.0).astype(BF16)
    z = mm(C_DK, C_DV)
    dk = [_rope(z[:, :128], c32, s32, 16), _rope(z[:, 128:], c32, s32, 16)]
    z = mm(C_DV, C_END)
    dv = [z[:, :128], z[:, 128:]]
    if not transposed:
        sk_ref, sv_ref, wk_ref, wv_ref, dk_ref, dv_ref = refs
        sk_ref[...] = sk
        sv_ref[...] = sv
        wk_ref[...] = wk
        wv_ref[...] = wv
        for c in range(2):
            dk_ref[:, LANES * c:LANES * (c + 1)] = dk[c]
            dv_ref[:, LANES * c:LANES * (c + 1)] = dv[c]
        return
    (skb_ref, wkb_ref, dkb_ref, svb_ref, wvb_ref, dvb_ref,
     ckt_ref, cvt_ref, skt_ref, svt_ref, wkt_ref, wvt_ref, dkt_ref, dvt_ref) = refs
    skb_ref[...] = sk.astype(BF16)
    wkb_ref[...] = wk.astype(BF16)
    ckt_ref[...] = ck.T
    cvt_ref[...] = cv.T
    skt_ref[...] = sk.T
    wkt_ref[...] = wk.T
    svt = sv.T
    svt_ref[...] = svt
    svb_ref[...] = svt.astype(BF16)
    wvt = wv.T
    wvt_ref[...] = wvt
    wvb_ref[...] = wvt.astype(BF16)
    for c in range(2):
        rows = slice(LANES * c, LANES * (c + 1))
        dkb_ref[:, rows] = dk[c].astype(BF16)
        dkt_ref[rows, :] = dk[c].T
        dvt = dv[c].T
        dvt_ref[rows, :] = dvt
        dvb_ref[rows, :] = dvt.astype(BF16)


def _inproj(x, w, tabs, nb, t, tm, transposed):
    n, d = x.shape
    nt = t // tm if transposed else 1
    grid = (n // tm,)
    row = lambda c: pl.BlockSpec((tm, c), lambda i: (i, 0))
    tab = pl.BlockSpec((tm, LANES), (lambda i: (i % nt, 0)) if transposed else (lambda i: (i, 0)))
    in_specs = [row(d), pl.BlockSpec((d, C_END), lambda i: (0, 0)), tab, tab, tab, tab]
    shapes = [(256, F32), (1024, BF16), (1024, BF16), (128, F32), (1024, BF16), (128, F32), (128, F32)]
    if transposed:
        shapes += [(128, BF16), (128, BF16), (256, BF16)]
    else:
        shapes += [(128, F32)] * 4 + [(256, F32)] * 2
    out_shape = [jax.ShapeDtypeStruct((n, c), dt) for c, dt in shapes]
    out_specs = [row(c) for c, _ in shapes]
    if transposed:
        for c, dt in [(128, BF16), (128, BF16), (256, BF16)] + [(128, F32)] * 6 + [(256, F32)] * 2:
            out_shape.append(jax.ShapeDtypeStruct((nb, c, t), dt))
            out_specs.append(pl.BlockSpec((None, c, tm), lambda i: (i // nt, 0, i % nt)))
    return pl.pallas_call(
        functools.partial(_inproj_body, dq_scale=DIFF_DQK ** -0.5 * LOG2E, transposed=transposed), name="inproj",
        grid=grid, in_specs=in_specs, out_specs=out_specs, out_shape=out_shape,
        compiler_params=_cparams(("parallel",)),
    )(x, w, *tabs)


CONV_PAD = 32


def _conv_body(ext_ref, w_ref, b_ref, g_ref, beta_ref, y_ref, acc_ref, sh_ref, *, tt, rs):
    t0 = pl.multiple_of(pl.program_id(1) * tt, 8)
    off = CONV_PAD - (CONV_W - 1)
    for c in range(2):
        cs = slice(LANES * c, LANES * (c + 1))
        for r in range(tt // rs):
            win = ext_ref[pl.ds(t0 + rs * r, rs + CONV_PAD), cs]
            acc = jnp.zeros((rs, LANES), F32)
            for r8 in range(8):
                taps = range(r8, CONV_W, 8)
                rows = 8 * (len(taps) - 1) + rs
                sh_ref[0:rows, :] = win[off + r8:off + r8 + rows, :]
                for a, k in enumerate(taps):
                    acc = acc + sh_ref[8 * a:8 * a + rs, :] * w_ref[k:k + 1, cs]
            acc_ref[rs * r:rs * (r + 1), cs] = acc
    y = _ln_rows(acc_ref[...] + b_ref[...], g_ref[...], beta_ref[...])
    y_ref[...] = y * _sigmoid(y)


def _conv(ext, w, b, g, beta, t, tt):
    nb, le, c = ext.shape
    rs = min(tt, 64)
    vec = pl.BlockSpec((1, c), lambda bi, ti: (0, 0))
    return pl.pallas_call(
        functools.partial(_conv_body, tt=tt, rs=rs), name="conv",
        grid=(nb, t // tt),
        in_specs=[pl.BlockSpec((None, le, c), lambda bi, ti: (bi, 0, 0)),
                  pl.BlockSpec((CONV_W, c), lambda bi, ti: (0, 0)), vec, vec, vec],
        out_specs=pl.BlockSpec((None, tt, c), lambda bi, ti: (bi, ti, 0)),
        out_shape=jax.ShapeDtypeStruct((nb, t, c), F32),
        scratch_shapes=[pltpu.VMEM((tt, c), F32), pltpu.VMEM((rs + CONV_PAD, LANES), F32)],
        compiler_params=_cparams(("parallel", "parallel")),
    )(ext, w, b, g, beta)


def _compress(src_ref, n16, pea, peb, wa_ref, wb_ref, b1, w2_ref):
    x = jnp.concatenate([src_ref[pl.ds(p, n16, stride=CMP_STRIDE), :] for p in range(CMP_STRIDE)], axis=1)
    a = _dot((x + pea).astype(BF16), wa_ref[...])
    bm = _dot((x + peb).astype(BF16), wb_ref[...])
    h = a + pltpu.roll(bm, n16 - 1, 0) + b1
    gl = 0.5 * h * (1.0 + jnp.tanh(0.7978845608028654 * (h + 0.044715 * (h * h * h))))
    return _dot(gl.astype(BF16), w2_ref[...])


def _top_rows(score, ridx, k):
    nv = score.shape[0] // 8
    tiles = [score[8 * v:8 * (v + 1), :] for v in range(nv)]
    cnts = [jnp.zeros(t.shape, F32) for t in tiles]
    for jp in range(score.shape[0]):
        row = score[jp:jp + 1, :]
        for v in range(nv):
            if 8 * v > jp:
                beats = row >= tiles[v]
            elif 8 * v + 7 < jp:
                beats = row > tiles[v]
            else:
                beats = (row > tiles[v]) | ((row == tiles[v]) & (ridx[8 * v:8 * (v + 1), :] > jp))
            cnts[v] = cnts[v] + jnp.where(beats, 1.0, 0.0)
    return jnp.concatenate(cnts, axis=0) < k


def _pcmp_body(qu_ref, ck_ref, cv_ref, pak_ref, pbk_ref, wak_ref, wbk_ref, b1k_ref, w2k_ref,
               pav_ref, pbv_ref, wav_ref, wbv_ref, b1v_ref, w2v_ref, mt_ref,
               o_ref, sel_ref, kcc_ref, vcct_ref, *, n16, nsel):
    @pl.when(pl.program_id(1) == 0)
    def _():
        kcc = _compress(ck_ref, n16, pak_ref[...], pbk_ref[...], wak_ref, wbk_ref, b1k_ref[...], w2k_ref)
        kcc_ref[...] = kcc.astype(BF16)
        vcc = _compress(cv_ref, n16, pav_ref[...], pbv_ref[...], wav_ref, wbv_ref, b1v_ref[...], w2v_ref)
        vcct_ref[...] = vcc.T.astype(BF16)

    for u in range(PCMP_TILES):
        _pcmp_tile(pl.program_id(1) * PCMP_TILES + u, qu_ref.at[TQ * u:TQ * (u + 1), :], kcc_ref[...], vcct_ref[...],
                   mt_ref[...], o_ref.at[TQ * u:TQ * (u + 1), :], sel_ref.at[u], n16, nsel)


PCMP_TILES = 2


def _pcmp_tile(ti, qu_ref, kcc, vcct, mt, o_ref, sel_ref, n16, nsel):
    q = qu_ref[...]
    nq = 4 * TQ
    cidx = lax.broadcasted_iota(jnp.int32, (n16, nq), 0)
    qpos = ti * TQ + (lax.broadcasted_iota(jnp.int32, (n16, nq), 1) & (TQ - 1))
    vis = (CMP_STRIDE * cidx + CMP_LEN - 1 <= qpos) & (cidx < n16 - 1)
    jidx = lax.broadcasted_iota(jnp.int32, (nsel, TQ), 0)
    qp1 = ti * TQ + lax.broadcasted_iota(jnp.int32, (nsel, TQ), 1)
    cur = qp1 >> 6
    forced = (jidx == 0) | (jidx == cur) | (jidx == cur - 1)
    for g in range(NSA_G):
        qs = jnp.concatenate([q[:, LANES * (4 * g + a):LANES * (4 * g + a + 1)] for a in range(4)], axis=0)
        st = _dot_nt(kcc, qs)
        st = jnp.where(vis, st, NEG)
        m = jnp.max(st, axis=0, keepdims=True)
        p = jnp.exp2(st - m)
        l = jnp.sum(p, axis=0, keepdims=True)
        inv = jnp.where(m > 0.5 * NEG, 1.0 / jnp.maximum(l, 1e-30), 0.0)
        p = p * inv
        ot = _dot(vcct[64 * g:64 * (g + 1), :], p.astype(BF16))
        for a2 in range(2):
            blk = jnp.concatenate([ot[:, TQ * (2 * a2 + e):TQ * (2 * a2 + e + 1)] for e in range(2)], axis=0)
            o_ref[:, 256 * g + LANES * a2:256 * g + LANES * (a2 + 1)] = blk.T
        imp = p[:, 0:TQ] + p[:, TQ:2 * TQ] + p[:, 2 * TQ:3 * TQ] + p[:, 3 * TQ:4 * TQ]
        hi = imp.astype(BF16)
        r1 = imp - hi.astype(F32)
        mid = r1.astype(BF16)
        lo = (r1 - mid.astype(F32)).astype(BF16)
        isel = _dot(mt, hi) + _dot(mt, mid) + _dot(mt, lo)
        score = jnp.where(forced, FORCE_SCORE, isel)
        score = jnp.where(jidx <= cur, score, -jnp.inf)
        sel = _top_rows(score, jidx, min(SEL_TOPN, nsel)) & (jidx <= cur)
        sel_ref[g] = jnp.where(sel, 0.0, NEG)


def _pcmp(qu, ck, cv, cw, nb, t):
    n16 = t // CMP_STRIDE
    nsel = t // SEL_BLK
    nt = t // (TQ * PCMP_TILES)
    mt = np.zeros((nsel, n16), np.float32)
    for j in range(nsel):
        for c in range(4 * j - 1, 4 * j + 4):
            if 0 <= c < n16 - 1:
                mt[j, c] = 1.0
    mt = jnp.asarray(mt, BF16)
    const = lambda a: pl.BlockSpec(a.shape, lambda bi, ti: (0,) * a.ndim)
    wlist = [cw[k] for k in ("pak", "pbk", "wak", "wbk", "b1k", "w2k", "pav", "pbv", "wav", "wbv", "b1v", "w2v")] + [mt]
    return pl.pallas_call(
        functools.partial(_pcmp_body, n16=n16, nsel=nsel), name="pcmp",
        grid=(nb, nt),
        in_specs=[pl.BlockSpec((TQ * PCMP_TILES, 1024), lambda bi, ti: (bi * nt + ti, 0)),
                  pl.BlockSpec((None, t, LANES), lambda bi, ti: (bi, 0, 0)),
                  pl.BlockSpec((None, t, LANES), lambda bi, ti: (bi, 0, 0))] + [const(a) for a in wlist],
        out_specs=[pl.BlockSpec((TQ * PCMP_TILES, 512), lambda bi, ti: (bi * nt + ti, 0)),
                   pl.BlockSpec((None, PCMP_TILES, NSA_G, nsel, TQ), lambda bi, ti: (bi, ti, 0, 0, 0))],
        out_shape=[jax.ShapeDtypeStruct((nb * t, 512), F32),
                   jax.ShapeDtypeStruct((nb, t // TQ, NSA_G, nsel, TQ), F32)],
        scratch_shapes=[pltpu.VMEM((n16, LANES), BF16), pltpu.VMEM((LANES, n16), BF16)],
        compiler_params=_cparams(("parallel", "arbitrary")),
    )(qu, ck.reshape(nb, t, LANES), cv.reshape(nb, t, LANES), *wlist)


def _stack_queries(q, s):
    return jnp.concatenate([q[:, LANES * (4 * s + a):LANES * (4 * s + a + 1)] for a in range(4)], axis=0)


def _flash_out(z, s, o_ref):
    tq = z.shape[1] // 4
    for a2 in range(2):
        blk = jnp.concatenate([z[:, tq * (2 * a2 + e):tq * (2 * a2 + e + 1)] for e in range(2)], axis=0)
        o_ref[:, 256 * s + LANES * a2:256 * s + LANES * (a2 + 1)] = blk.T


def _flash_body(*refs, mode, lam_init):
    if mode == "sel":
        q_ref, k_ref, vt_ref, selb_ref, o_ref, acc_ref, m_ref, l_ref, al_ref, s_ref, p_ref = refs
    else:
        (q_ref, k_ref, vt_ref, lq1_ref, lk1_ref, lq2_ref, lk2_ref, sg_ref, o_ref,
         acc_ref, m_ref, l_ref, al_ref, s_ref, p_ref) = refs
    ti = pl.program_id(1)
    tq = TQ * FLASH_TILES
    nq = 4 * tq
    q = q_ref[...]
    qs = [_stack_queries(q, s) for s in range(2)]
    koffs = [LANES * s if mode == "diff" else 0 for s in range(2)]
    m_ref[...] = jnp.full(m_ref.shape, NEG, F32)
    l_ref[...] = jnp.zeros(l_ref.shape, F32)
    acc_ref[...] = jnp.zeros(acc_ref.shape, F32)
    al_ref[...] = jnp.ones(al_ref.shape, F32)
    p_ref[...] = jnp.zeros(p_ref.shape, BF16)

    def scores(j, slot):
        k0 = pl.multiple_of(j * TK, TK)
        for s in range(2):
            s_ref[slot, s] = _dot_nt(k_ref[pl.ds(k0, TK), koffs[s]:koffs[s] + LANES], qs[s])

    def accumulate(j):
        k0 = pl.multiple_of(j * TK, TK)
        for s in range(2):
            if mode == "sel":
                acc_ref[s] = al_ref[s] * acc_ref[s] + _dot(vt_ref[64 * s:64 * (s + 1), pl.ds(k0, TK)], p_ref[s])
            else:
                for hh in range(2):
                    cols = slice(2 * tq * hh, 2 * tq * (hh + 1))
                    acc_ref[s, :, cols] = (al_ref[s, :, cols] * acc_ref[s, :, cols]
                                           + _dot(vt_ref[koffs[s] + 64 * hh:koffs[s] + 64 * (hh + 1), pl.ds(k0, TK)], p_ref[s, :, cols]))

    def softmax(j, slot, causal):
        for s in range(2):
            st = s_ref[slot, s]
            if mode == "sel":
                rows = [jnp.broadcast_to(jnp.concatenate(
                    [selb_ref[u, s, pl.ds((TK // SEL_BLK) * j + r, 1), :] for u in range(FLASH_TILES)], axis=1), (SEL_BLK, tq))
                    for r in range(TK // SEL_BLK)]
                bias = jnp.concatenate(rows, axis=0)
                st = st + jnp.concatenate([bias, bias, bias, bias], axis=1)
            if causal:
                krow = lax.broadcasted_iota(jnp.int32, (TK, nq), 0)
                qcol = lax.broadcasted_iota(jnp.int32, (TK, nq), 1) & (tq - 1)
                st = jnp.where(ti * tq + qcol >= j * TK + krow, st, NEG)
            m_old = m_ref[s]
            m_new = jnp.maximum(m_old, jnp.max(st, axis=0, keepdims=True))
            alpha = jnp.exp2(m_old - m_new)
            p = jnp.exp2(st - m_new)
            l_ref[s] = alpha * l_ref[s] + jnp.sum(p, axis=0, keepdims=True)
            p_ref[s] = p.astype(BF16)
            al_ref[s] = alpha
            m_ref[s] = m_new

    n_full = (ti * tq) // TK
    scores(0, 0)

    def body(j, c):
        slot = j & 1
        accumulate(jnp.maximum(j - 1, 0))
        softmax(j, slot, False)
        scores(j + 1, 1 - slot)
        return c

    lax.fori_loop(0, n_full, body, 0)
    accumulate(jnp.maximum(n_full - 1, 0))
    softmax(n_full, n_full & 1, True)
    accumulate(n_full)

    if mode == "diff":
        lam = (jnp.exp(jnp.sum(lq1_ref[...] * lk1_ref[...], keepdims=True))
               - jnp.exp(jnp.sum(lq2_ref[...] * lk2_ref[...], keepdims=True)) + lam_init)
    for s in range(2):
        z = acc_ref[s] * (1.0 / l_ref[s])
        if mode == "diff":
            halves = []
            for hh in range(2):
                d = z[:, tq * (2 * hh):tq * (2 * hh + 1)] - lam * z[:, tq * (2 * hh + 1):tq * (2 * hh + 2)]
                ms = jnp.mean(d * d, axis=0, keepdims=True)
                halves.append(d * lax.rsqrt(ms + LN_EPS) * sg_ref[...] * (1.0 - lam_init))
            o_ref[:, LANES * s:LANES * (s + 1)] = jnp.concatenate(halves, axis=0).T
        else:
            _flash_out(z, s, o_ref)


FLASH_TILES = 1


def _flash(mode, q, k, vt, nb, t, extra=(), lam_init=0.0):
    tq = TQ * FLASH_TILES
    assert t % tq == 0 and t % TK == 0 and (tq % TK == 0 or TK % tq == 0)
    nt = t // tq
    kw = k.shape[-1]
    ow = 256 if mode == "diff" else 512
    in_specs = [pl.BlockSpec((tq, 1024), lambda bi, ti: (bi * nt + ti, 0)),
                pl.BlockSpec((None, t, kw), lambda bi, ti: (bi, 0, 0)),
                pl.BlockSpec((None, kw, t), lambda bi, ti: (bi, 0, 0))]
    if mode == "sel":
        nsel = t // SEL_BLK
        in_specs.append(pl.BlockSpec((None, FLASH_TILES, NSA_G, nsel, TQ), lambda bi, ti: (bi, ti, 0, 0, 0)))
    else:
        in_specs += [pl.BlockSpec(a.shape, lambda bi, ti: (0, 0)) for a in extra]
    return pl.pallas_call(
        functools.partial(_flash_body, mode=mode, lam_init=lam_init), name="flash_" + mode,
        grid=(nb, nt), in_specs=in_specs,
        out_specs=pl.BlockSpec((tq, ow), lambda bi, ti: (bi * nt + ti, 0)),
        out_shape=jax.ShapeDtypeStruct((nb * t, ow), F32),
        scratch_shapes=[pltpu.VMEM((2, 64, 4 * tq), F32)] + [pltpu.VMEM((2, 1, 4 * tq), F32)] * 3
        + [pltpu.VMEM((2, 2, TK, 4 * tq), F32), pltpu.VMEM((2, TK, 4 * tq), BF16)],
        compiler_params=_cparams(("parallel", "parallel")),
    )(q, k.reshape(nb, t, kw), vt, *extra)


def _win_body(q_ref, k_ref, vt_ref, o_ref):
    nq = 4 * TQ
    span = WINDOW + TQ

    def tiles(interior):
        if interior:
            krow = lax.broadcasted_iota(jnp.int32, (TQ, nq), 0)
            qcol = lax.broadcasted_iota(jnp.int32, (TQ, nq), 1) & (TQ - 1)
        else:
            krow = lax.broadcasted_iota(jnp.int32, (span, nq), 0)
            qcol = lax.broadcasted_iota(jnp.int32, (span, nq), 1) & (TQ - 1)
        for u in range(WIN_TILES):
            ti = pl.program_id(1) * WIN_TILES + u
            k0 = pl.multiple_of(jnp.maximum(ti * TQ - WINDOW, 0), TQ)
            q = q_ref[TQ * u:TQ * (u + 1), :]
            kt = k_ref[pl.ds(k0, span), :]
            vt = vt_ref[:, pl.ds(k0, span)]
            for s in range(2):
                st = _dot_nt(kt, _stack_queries(q, s))
                if interior:
                    st = jnp.concatenate([jnp.where(krow > qcol, st[:TQ], NEG), st[TQ:WINDOW],
                                          jnp.where(krow <= qcol, st[WINDOW:], NEG)], axis=0)
                else:
                    dlt = ti * TQ + qcol - (k0 + krow)
                    st = jnp.where((dlt >= 0) & (dlt < WINDOW), st, NEG)
                m = jnp.max(st, axis=0, keepdims=True)
                p = jnp.exp2(st - m)
                l = jnp.sum(p, axis=0, keepdims=True)
                _flash_out(_dot(vt[64 * s:64 * (s + 1), :], p.astype(BF16)) * (1.0 / l), s, o_ref.at[TQ * u:TQ * (u + 1), :])

    first_interior = -(-WINDOW // (TQ * WIN_TILES))

    @pl.when(pl.program_id(1) >= first_interior)
    def _():
        tiles(True)

    @pl.when(pl.program_id(1) < first_interior)
    def _():
        tiles(False)


WIN_TILES = 4


def _win(q, k, vt, nb, t):
    nt = t // (TQ * WIN_TILES)
    assert t >= WINDOW + TQ and t % (TQ * WIN_TILES) == 0
    return pl.pallas_call(
        _win_body, name="flash_win", grid=(nb, nt),
        in_specs=[pl.BlockSpec((TQ * WIN_TILES, 1024), lambda bi, ti: (bi * nt + ti, 0)),
                  pl.BlockSpec((None, t, LANES), lambda bi, ti: (bi, 0, 0)),
                  pl.BlockSpec((None, LANES, t), lambda bi, ti: (bi, 0, 0))],
        out_specs=pl.BlockSpec((TQ * WIN_TILES, 512), lambda bi, ti: (bi * nt + ti, 0)),
        out_shape=jax.ShapeDtypeStruct((nb * t, 512), F32),
        compiler_params=_cparams(("parallel", "parallel")),
    )(q, k.reshape(nb, t, LANES), vt)


def _outproj_body(x_ref, yc_ref, oc_ref, os_ref, ow_ref, gt_ref, od_ref, e_ref, w_ref, g_ref, b_ref, o_ref, *, alpha):
    gt = gt_ref[...]
    hi = gt.astype(BF16)
    lo = (gt - hi.astype(F32)).astype(BF16)
    e = e_ref[...]
    gx = _dot(hi, e) + _dot(lo, e)
    onsa = gx[:, 0:512] * oc_ref[...] + gx[:, 512:1024] * os_ref[...] + gx[:, 1024:1536] * ow_ref[...]
    mix = (_dot(yc_ref[...].astype(BF16), w_ref[0:256, :]) + _dot(onsa.astype(BF16), w_ref[256:768, :])
           + _dot(od_ref[...].astype(BF16), w_ref[768:1024, :]))
    o_ref[...] = _ln_rows(alpha * x_ref[...] + mix, g_ref[...], b_ref[...])


def _outproj(x, yc, oc, osel, ow, gt, od, e, w, g, b, alpha, tm):
    n, d = x.shape
    row = lambda c: pl.BlockSpec((tm, c), lambda i: (i, 0))
    const = lambda a: pl.BlockSpec(a.shape, lambda i: (0, 0))
    return pl.pallas_call(
        functools.partial(_outproj_body, alpha=alpha), name="outproj",
        grid=(n // tm,),
        in_specs=[row(d), row(256), row(512), row(512), row(512), row(128), row(256), const(e), const(w), const(g), const(b)],
        out_specs=row(d), out_shape=jax.ShapeDtypeStruct((n, d), F32),
        compiler_params=_cparams(("parallel",)),
    )(x, yc, oc, osel, ow, gt, od, e, w, g, b)


def _ffn_body(x_ref, w1_ref, w3_ref, w2_ref, g_ref, b_ref, o_ref, acc_ref, *, alpha):
    f = pl.program_id(1)
    xb = x_ref[...].astype(BF16)
    h1 = _dot(xb, w1_ref[...])
    h = h1 * _sigmoid(h1) * _dot(xb, w3_ref[...])
    part = _dot(h.astype(BF16), w2_ref[...])

    @pl.when(f == 0)
    def _():
        acc_ref[...] = part

    @pl.when(f != 0)
    def _():
        acc_ref[...] += part

    @pl.when(f == pl.num_programs(1) - 1)
    def _():
        o_ref[...] = _ln_rows(alpha * x_ref[...] + acc_ref[...], g_ref[...], b_ref[...])


def _ffn(x, w1, w3, w2, g, b, alpha, tm, nf):
    n, d = x.shape
    dff = w1.shape[1]
    tf = dff // nf
    return pl.pallas_call(
        functools.partial(_ffn_body, alpha=alpha), name="ffn",
        grid=(n // tm, nf),
        in_specs=[pl.BlockSpec((tm, d), lambda i, f: (i, 0)),
                  pl.BlockSpec((d, tf), lambda i, f: (0, f)),
                  pl.BlockSpec((d, tf), lambda i, f: (0, f)),
                  pl.BlockSpec((tf, d), lambda i, f: (f, 0)),
                  pl.BlockSpec((1, d), lambda i, f: (0, 0)),
                  pl.BlockSpec((1, d), lambda i, f: (0, 0))],
        out_specs=pl.BlockSpec((tm, d), lambda i, f: (i, 0)),
        out_shape=jax.ShapeDtypeStruct((n, d), F32),
        scratch_shapes=[pltpu.VMEM((tm, d), F32)],
        compiler_params=_cparams(("parallel", "arbitrary")),
    )(x, w1, w3, w2, g, b)


def _gather_pages(pt_ref, n_pages, srcs, sem, layer):
    b = pl.program_id(0)
    slot = b & 1

    def copies(bb, sl):
        return [pltpu.make_async_copy(hbm.at[layer, pt_ref[bb, j]], buf.at[sl, :, pl.ds(PAGE * j, PAGE)], sem.at[a, sl])
                for a, (hbm, buf) in enumerate(srcs) for j in range(n_pages)]

    @pl.when(b == 0)
    def _():
        for c in copies(0, 0):
            c.start()

    @pl.when(b + 1 < pl.num_programs(0))
    def _():
        for c in copies(b + 1, 1 - slot):
            c.start()

    for c in copies(b, slot):
        c.wait()
    return slot


def _key_chunks(n):
    per = max(d for d in range(1, 17) if n % d == 0)
    return [(PAGE * per * c, PAGE * per) for c in range(n // per)]


def _stack_heads(q):
    return jnp.concatenate([q[:, LANES * h:LANES * (h + 1)] for h in range(8)], axis=0)


def _scmp_body(pt_ref, qu_ref, ck_hbm, cv_hbm, pak_ref, pbk_ref, wak_ref, wbk_ref, b1k_ref, w2k_ref,
               pav_ref, pbv_ref, wav_ref, wbv_ref, b1v_ref, w2v_ref, m_ref,
               o_ref, sel_ref, kbt, vbt, kbuf, vbuf, sem, *, layer, n_pages, past, nq):
    slot = _gather_pages(pt_ref, n_pages, [(ck_hbm, kbt), (cv_hbm, vbt)], sem, layer)
    for j in range(n_pages):
        kbuf[PAGE * j:PAGE * (j + 1), :] = kbt[slot, :, PAGE * j:PAGE * (j + 1)].T
        vbuf[PAGE * j:PAGE * (j + 1), :] = vbt[slot, :, PAGE * j:PAGE * (j + 1)].T
    n16 = past // CMP_STRIDE
    kcc = _compress(kbuf, n16, pak_ref[...], pbk_ref[...], wak_ref, wbk_ref, b1k_ref[...], w2k_ref).astype(BF16)
    vcc = _compress(vbuf, n16, pav_ref[...], pbv_ref[...], wav_ref, wbv_ref, b1v_ref[...], w2v_ref).astype(BF16)
    qa = _stack_heads(qu_ref[...])
    rows = 8 * nq
    s = _dot_nt(qa, kcc)
    cidx = lax.broadcasted_iota(jnp.int32, (rows, n16), 1)
    qpos = past + (lax.broadcasted_iota(jnp.int32, (rows, n16), 0) & (nq - 1))
    vis = (CMP_STRIDE * cidx + CMP_LEN - 1 <= qpos) & (cidx < n16 - 1)
    s = jnp.where(vis, s, NEG)
    m = jnp.max(s, axis=-1, keepdims=True)
    p = jnp.where(vis, jnp.exp2(s - m), 0.0)
    l = jnp.sum(p, axis=-1, keepdims=True)
    p = p / jnp.maximum(l, 1e-30)
    o_ref[...] = _dot(p.astype(BF16), vcc)
    imp = jnp.concatenate(
        [p[nq * 4 * g:nq * (4 * g + 1)] + p[nq * (4 * g + 1):nq * (4 * g + 2)]
         + p[nq * (4 * g + 2):nq * (4 * g + 3)] + p[nq * (4 * g + 3):nq * (4 * g + 4)] for g in range(NSA_G)], axis=0)
    isel = _split3_dot(imp, m_ref[...])
    nselp = isel.shape[1]
    jidx = lax.broadcasted_iota(jnp.int32, (NSA_G * nq, nselp), 1)
    qp = past + (lax.broadcasted_iota(jnp.int32, (NSA_G * nq, nselp), 0) & (nq - 1))
    cur = qp >> 6
    forced = (jidx == 0) | (jidx == cur) | (jidx == cur - 1)
    score = jnp.where(forced, FORCE_SCORE, isel)
    sel_ref[...] = jnp.where(jidx <= cur, score, -jnp.inf)


def _stopk_body(score_ref, sel_ref, *, k, past, nq):
    score = score_ref[...]
    nselp = score.shape[1]
    jidx = lax.broadcasted_iota(jnp.int32, score.shape, 1)
    cur = (past + (lax.broadcasted_iota(jnp.int32, score.shape, 0) & (nq - 1))) >> 6
    sel = jnp.zeros(score.shape, F32)
    for _ in range(k):
        mx = jnp.max(score, axis=-1, keepdims=True)
        first = jnp.min(jnp.where(score == mx, jidx, nselp), axis=-1, keepdims=True)
        hit = jidx == first
        sel = jnp.where(hit, 1.0, sel)
        score = jnp.where(hit, -jnp.inf, score)
    sel_ref[...] = jnp.where(jidx <= cur, sel, 0.0)


def _stopk(score, past, nq):
    nb, r, nselp = score.shape
    n_sel = -(-(past + nq) // SEL_BLK)
    full = pl.BlockSpec((nb * r, nselp), lambda i: (0, 0))
    sel = pl.pallas_call(
        functools.partial(_stopk_body, k=min(SEL_TOPN, n_sel), past=past, nq=nq), name="stopk",
        grid=(1,), in_specs=[full], out_specs=full,
        out_shape=jax.ShapeDtypeStruct((nb * r, nselp), F32),
        compiler_params=_cparams(("arbitrary",)),
    )(score.reshape(nb * r, nselp))
    return sel.reshape(nb, r, nselp)


def _scmp(pt, qu, cache_k, cache_v, cw, layer, nb, nq, past):
    n_pages = past // PAGE
    n16 = past // CMP_STRIDE
    n_sel = -(-(past + nq) // SEL_BLK)
    nselp = -(-n_sel // LANES) * LANES
    m = np.zeros((n16, nselp), np.float32)
    for j in range(n_sel):
        for c in range(4 * j - 1, 4 * j + 4):
            if 0 <= c < n16 - 1:
                m[c, j] = 1.0
    m = jnp.asarray(m, BF16)
    wlist = [cw[k] for k in ("pak", "pbk", "wak", "wbk", "b1k", "w2k", "pav", "pbv", "wav", "wbv", "b1v", "w2v")] + [m]
    const = lambda a: pl.BlockSpec(a.shape, lambda bi, pt_: (0,) * a.ndim)
    any_spec = pl.BlockSpec(memory_space=pl.ANY)
    grid_spec = pltpu.PrefetchScalarGridSpec(
        num_scalar_prefetch=1, grid=(nb,),
        in_specs=[pl.BlockSpec((nq, 1024), lambda bi, pt_: (bi, 0)), any_spec, any_spec] + [const(a) for a in wlist],
        out_specs=[pl.BlockSpec((None, 8 * nq, LANES), lambda bi, pt_: (bi, 0, 0)),
                   pl.BlockSpec((None, NSA_G * nq, nselp), lambda bi, pt_: (bi, 0, 0))],
        scratch_shapes=[pltpu.VMEM((2, LANES, past), F32), pltpu.VMEM((2, LANES, past), F32),
                        pltpu.VMEM((past, LANES), F32), pltpu.VMEM((past, LANES), F32), pltpu.SemaphoreType.DMA((2, 2))])
    return pl.pallas_call(
        functools.partial(_scmp_body, layer=layer, n_pages=n_pages, past=past, nq=nq), name="scmp",
        grid_spec=grid_spec,
        out_shape=[jax.ShapeDtypeStruct((nb, 8 * nq, LANES), F32), jax.ShapeDtypeStruct((nb, NSA_G * nq, nselp), F32)],
        compiler_params=_cparams(("arbitrary",)),
    )(pt, qu, cache_k, cache_v, *wlist)


def _softmax_rows(s):
    m = jnp.max(s, axis=-1, keepdims=True)
    p = jnp.exp2(s - m)
    return p, jnp.sum(p, axis=-1, keepdims=True)


def _sselwin_body(pt_ref, qr_ref, sel_ref, skn_ref, svn_ref, wkn_ref, wvn_ref, wks_ref, wvs_ref, e_ref, sk_hbm, sv_hbm,
                  osel_ref, owin_ref, kbuf, vbuf, sem, *, layer, n_pages, past, nq):
    ktot = past + PAGE
    rows = 8 * nq
    slot = _gather_pages(pt_ref, n_pages, [(sk_hbm, kbuf), (sv_hbm, vbuf)], sem, layer)
    pad = jnp.zeros((PAGE - nq, LANES), F32)
    kbuf[slot, :, pl.ds(past, PAGE)] = jnp.concatenate([skn_ref[...], pad], axis=0).T
    vbuf[slot, :, pl.ds(past, PAGE)] = jnp.concatenate([svn_ref[...], pad], axis=0).T
    chunks = _key_chunks(n_pages + 1)
    qa = _stack_heads(qr_ref[...])
    s = jnp.concatenate([_dot(qa, kbuf[slot, :, pl.ds(c0, cn)].astype(BF16)) for c0, cn in chunks], axis=1)
    se = _dot(sel_ref[...].astype(BF16), e_ref[...])
    se = jnp.concatenate([se[nq * g:nq * (g + 1)] for g in range(NSA_G) for _ in range(NSA_HPG)], axis=0)
    kpos = lax.broadcasted_iota(jnp.int32, (rows, ktot), 1)
    qpos = past + (lax.broadcasted_iota(jnp.int32, (rows, ktot), 0) & (nq - 1))
    s = jnp.where((se > 0.5) & (kpos <= qpos), s, NEG)
    p, l = _softmax_rows(s)
    pb = p.astype(BF16)
    o = _dot_nt(pb[:, 0:chunks[0][1]], vbuf[slot, :, pl.ds(0, chunks[0][1])].astype(BF16))
    for c0, cn in chunks[1:]:
        o = o + _dot_nt(pb[:, c0:c0 + cn], vbuf[slot, :, pl.ds(c0, cn)].astype(BF16))
    osel_ref[...] = o / l
    wb = wks_ref.shape[0]
    kw = jnp.concatenate([wks_ref[...], wkn_ref[...], pad], axis=0).astype(BF16)
    vw = jnp.concatenate([wvs_ref[...], wvn_ref[...], pad], axis=0).astype(BF16)
    s = _dot_nt(qa, kw)
    i = lax.broadcasted_iota(jnp.int32, (rows, wb + PAGE), 1)
    kp = jnp.where(i < wb, past - wb + i, past + i - wb)
    qp = past + (lax.broadcasted_iota(jnp.int32, (rows, wb + PAGE), 0) & (nq - 1))
    dlt = qp - kp
    s = jnp.where((dlt >= 0) & (dlt < WINDOW) & (i < wb + nq), s, NEG)
    p, l = _softmax_rows(s)
    owin_ref[...] = _dot(p.astype(BF16), vw) / l


def _sselwin(pt, qr, sel, skn, svn, wkn, wvn, wks, wvs, cache_k, cache_v, layer, nb, nq, past):
    n_pages = past // PAGE
    ktot = past + PAGE
    nselp = sel.shape[-1]
    e = (np.arange(ktot)[None, :] // SEL_BLK == np.arange(nselp)[:, None]).astype(np.float32)
    e = jnp.asarray(e, BF16)
    wb = wks.shape[1]
    any_spec = pl.BlockSpec(memory_space=pl.ANY)
    new = pl.BlockSpec((nq, LANES), lambda bi, pt_: (bi, 0))
    grid_spec = pltpu.PrefetchScalarGridSpec(
        num_scalar_prefetch=1, grid=(nb,),
        in_specs=[pl.BlockSpec((nq, 1024), lambda bi, pt_: (bi, 0)),
                  pl.BlockSpec((None, NSA_G * nq, nselp), lambda bi, pt_: (bi, 0, 0)),
                  new, new, new, new,
                  pl.BlockSpec((None, wb, LANES), lambda bi, pt_: (bi, 0, 0)),
                  pl.BlockSpec((None, wb, LANES), lambda bi, pt_: (bi, 0, 0)),
                  pl.BlockSpec(e.shape, lambda bi, pt_: (0, 0)), any_spec, any_spec],
        out_specs=[pl.BlockSpec((None, 8 * nq, LANES), lambda bi, pt_: (bi, 0, 0)),
                   pl.BlockSpec((None, 8 * nq, LANES), lambda bi, pt_: (bi, 0, 0))],
        scratch_shapes=[pltpu.VMEM((2, LANES, ktot), F32), pltpu.VMEM((2, LANES, ktot), F32), pltpu.SemaphoreType.DMA((2, 2))])
    return pl.pallas_call(
        functools.partial(_sselwin_body, layer=layer, n_pages=n_pages, past=past, nq=nq), name="sselwin",
        grid_spec=grid_spec,
        out_shape=[jax.ShapeDtypeStruct((nb, 8 * nq, LANES), F32)] * 2,
        compiler_params=_cparams(("arbitrary",)),
    )(pt, qr, sel, skn, svn, wkn, wvn, wks, wvs, e, cache_k, cache_v)


def _sdiff_body(pt_ref, dq_ref, dkn_ref, dvn_ref, lq1_ref, lk1_ref, lq2_ref, lk2_ref, sg_ref, dk_hbm, dv_hbm,
                o_ref, kbuf, vbuf, sem, *, layer, n_pages, past, nq, lam_init):
    ktot = past + PAGE
    rows = 8 * nq
    slot = _gather_pages(pt_ref, n_pages, [(dk_hbm, kbuf), (dv_hbm, vbuf)], sem, layer)
    pad = jnp.zeros((PAGE - nq, 256), F32)
    kbuf[slot, :, pl.ds(past, PAGE)] = jnp.concatenate([dkn_ref[...], pad], axis=0).T
    vbuf[slot, :, pl.ds(past, PAGE)] = jnp.concatenate([dvn_ref[...], pad], axis=0).T
    chunks = _key_chunks(n_pages + 1)
    q = dq_ref[...]
    zero = jnp.zeros((nq, LANES), BF16)
    blocks = []
    for a in range(8):
        chunk = q[:, LANES * a:LANES * (a + 1)]
        blocks.append(jnp.concatenate([chunk, zero] if a < 4 else [zero, chunk], axis=1))
    qa = jnp.concatenate(blocks, axis=0)
    s = jnp.concatenate([_dot(qa, kbuf[slot, :, pl.ds(c0, cn)].astype(BF16)) for c0, cn in chunks], axis=1)
    kpos = lax.broadcasted_iota(jnp.int32, (rows, ktot), 1)
    qpos = past + (lax.broadcasted_iota(jnp.int32, (rows, ktot), 0) & (nq - 1))
    s = jnp.where(kpos <= qpos, s, NEG)
    p, l = _softmax_rows(s)
    pb = p.astype(BF16)
    o = _dot_nt(pb[:, 0:chunks[0][1]], vbuf[slot, :, pl.ds(0, chunks[0][1])].astype(BF16))
    for c0, cn in chunks[1:]:
        o = o + _dot_nt(pb[:, c0:c0 + cn], vbuf[slot, :, pl.ds(c0, cn)].astype(BF16))
    o = o / l
    lam = (jnp.exp(jnp.sum(lq1_ref[...] * lk1_ref[...], keepdims=True))
           - jnp.exp(jnp.sum(lq2_ref[...] * lk2_ref[...], keepdims=True)) + lam_init)
    lane = lax.broadcasted_iota(jnp.int32, (nq, 256), 1)
    out = jnp.zeros((nq, 256), F32)
    for h in range(DIFF_H):
        d = o[2 * nq * h:2 * nq * h + nq] - lam * o[2 * nq * h + nq:2 * nq * (h + 1)]
        inh = (lane >> 6) == h
        ms = jnp.sum(jnp.where(inh, d * d, 0.0), axis=-1, keepdims=True) * (1.0 / DIFF_DV)
        out = out + jnp.where(inh, d * lax.rsqrt(ms + LN_EPS), 0.0)
    o_ref[...] = out * sg_ref[...] * (1.0 - lam_init)


def _sdiff(pt, dq, dkn, dvn, lams, sg_row, cache_k, cache_v, layer, nb, nq, past, lam_init):
    n_pages = past // PAGE
    ktot = past + PAGE
    any_spec = pl.BlockSpec(memory_space=pl.ANY)
    new = pl.BlockSpec((nq, 256), lambda bi, pt_: (bi, 0))
    small = [pl.BlockSpec(a.shape, lambda bi, pt_: (0, 0)) for a in (*lams, sg_row)]
    grid_spec = pltpu.PrefetchScalarGridSpec(
        num_scalar_prefetch=1, grid=(nb,),
        in_specs=[pl.BlockSpec((nq, 1024), lambda bi, pt_: (bi, 0)), new, new] + small + [any_spec, any_spec],
        out_specs=pl.BlockSpec((nq, 256), lambda bi, pt_: (bi, 0)),
        scratch_shapes=[pltpu.VMEM((2, 256, ktot), F32), pltpu.VMEM((2, 256, ktot), F32), pltpu.SemaphoreType.DMA((2, 2))])
    return pl.pallas_call(
        functools.partial(_sdiff_body, layer=layer, n_pages=n_pages, past=past, nq=nq, lam_init=lam_init), name="sdiff",
        grid_spec=grid_spec,
        out_shape=jax.ShapeDtypeStruct((nb * nq, 256), F32),
        compiler_params=_cparams(("arbitrary",)),
    )(pt, dq, dkn, dvn, *lams, sg_row, cache_k, cache_v)


def _prep_w_in(w):
    pts = np.cumsum([256, 256, 512, 128, 128, 128, 128, 128, 128, 24, 256, 256, 256])[:-1].tolist()
    ca, cg, nq, ck, cv, sk, sv, wk, wv, gt, dq, dk, dv = jnp.split(w, pts, axis=1)
    gtp = jnp.pad(gt, ((0, 0), (0, LANES - gt.shape[1])))
    return jnp.concatenate([ca, cg, nq * (NSA_DH ** -0.5 * LOG2E), ck, cv, sk, sv, wk, wv, gtp, dq, dk, dv], axis=1).astype(BF16)


def _rope_tables(pos):
    out = []
    lane = np.arange(LANES)
    for half in (32, 16):
        inv = ROPE_THETA ** (-jnp.arange(half, dtype=F32) / half)
        ang = pos.astype(F32)[:, None] * inv[None, :]
        idx = lane % half
        sign = jnp.asarray(np.where(lane % (2 * half) < half, -1.0, 1.0), F32)
        out += [jnp.cos(ang)[:, idx], jnp.sin(ang)[:, idx] * sign[None, :]]
    return out


def _prep_cmp(pe, w1, b1, w2):
    def halves(x):
        res = []
        for part in (x[:16 * NSA_DH], x[16 * NSA_DH:]):
            p4 = part.reshape(CMP_STRIDE, 1, NSA_DH, 1, -1)
            eye = jnp.eye(NSA_G, dtype=F32)[None, :, None, :, None]
            res.append((p4 * eye).reshape(CMP_STRIDE * NSA_G * NSA_DH, NSA_G * part.shape[-1]))
        return res
    wa, wb = halves(w1)
    pea = jnp.tile(pe[:16, None, :], (1, NSA_G, 1)).reshape(1, -1)
    peb = jnp.tile(pe[16:, None, :], (1, NSA_G, 1)).reshape(1, -1)
    w2bd = (w2[None, :, None, :] * jnp.eye(NSA_G, dtype=F32)[:, None, :, None]).reshape(NSA_G * CMP_HID, NSA_G * NSA_DH)
    return pea, peb, wa.astype(BF16), wb.astype(BF16), jnp.tile(b1, NSA_G)[None, :], w2bd.astype(BF16)


def _gate_expand():
    e = np.zeros((LANES, 3 * 512), np.float32)
    for h in range(NSA_H):
        for br in range(3):
            e[3 * h + br, 512 * br + 64 * h:512 * br + 64 * (h + 1)] = 1.0
    return jnp.asarray(e, BF16)


def _rows_to_tokens(o, nb, nq):
    o6 = o.reshape(nb, NSA_G, NSA_HPG, nq, NSA_G, NSA_DH)
    pick = jnp.stack([o6[:, g, :, :, g, :] for g in range(NSA_G)], axis=1)
    return pick.transpose(0, 3, 1, 2, 4).reshape(nb * nq, NSA_H * NSA_DH)


def _prompt_layer(x, lw, tabs, nb, t, lam_init, alpha):
    n = nb * t
    (u, qu, qr, gt, dq, ck, cv, skb, wkb, dkb, svb, wvb, dvb,
     ckt, cvt, skt, svt, wkt, wvt, dkt, dvt) = _inproj(x, lw["w_in"], tabs, nb, t, 256, True)
    ext = jnp.pad(u.reshape(nb, t, 256), ((0, 0), (CONV_PAD, 0), (0, 0)))
    yc = _conv(ext, lw["dw_w"], lw["dw_b"], lw["cln_g"], lw["cln_b"], t, 256).reshape(n, 256)
    ocmp, sel = _pcmp(qu, ck, cv, lw["cmp"], nb, t)
    osel = _flash("sel", qr, skb, svb, nb, t, extra=(sel,))
    owin = _win(qr, wkb, wvb, nb, t)
    odiff = _flash("diff", dq, dkb, dvb, nb, t, extra=lw["lams"] + (lw["sg_col"],), lam_init=lam_init)
    x1 = _outproj(x, yc, ocmp, osel, owin, gt, odiff, lw["gate_e"], lw["w_out"], lw["ln1_g"], lw["ln1_b"], alpha, 256)
    x2 = _ffn(x1, lw["w1"], lw["w3"], lw["w2"], lw["ln2_g"], lw["ln2_b"], alpha, 512, 2)
    nk = min(WINDOW, t)
    rows_major = lambda a: a.reshape(nb, a.shape[1] // 64, 64, a.shape[2]).transpose(0, 3, 1, 2)
    news = (rows_major(ckt), rows_major(cvt), rows_major(skt), rows_major(svt), rows_major(dkt), rows_major(dvt),
            rows_major(wkt[:, :, t - nk:]), rows_major(wvt[:, :, t - nk:]),
            u.reshape(nb, t, 256)[:, t - (CONV_W - 1):])
    return x2, news


def _sample_layer(x, lw, tabs, caches, states, pt, layer, nb, nq, past, lam_init, alpha):
    n = nb * nq
    (u, qu, qr, gt, dq, ck, cv, sk, sv, wk, wv, dk, dv) = _inproj(x, lw["w_in"], tabs, nb, nq, n, False)
    c_cmp_k, c_cmp_v, c_sel_k, c_sel_v, c_diff_k, c_diff_v = caches
    st_wk, st_wv, st_conv = states
    ext = jnp.concatenate([jnp.zeros((nb, CONV_PAD - (CONV_W - 1), 256), F32), st_conv, u.reshape(nb, nq, 256)], axis=1)
    yc = _conv(ext, lw["dw_w"], lw["dw_b"], lw["cln_g"], lw["cln_b"], nq, nq).reshape(n, 256)
    ocmp, score = _scmp(pt, qu, c_cmp_k, c_cmp_v, lw["cmp"], layer, nb, nq, past)
    sel = _stopk(score, past, nq)
    wb = st_wk.shape[1]
    osel, owin = _sselwin(pt, qr, sel, sk, sv, wk, wv, st_wk.reshape(nb, wb, LANES), st_wv.reshape(nb, wb, LANES),
                          c_sel_k, c_sel_v, layer, nb, nq, past)
    odiff = _sdiff(pt, dq, dk, dv, lw["lams"], lw["sg_row"], c_diff_k, c_diff_v, layer, nb, nq, past, lam_init)
    x1 = _outproj(x, yc, _rows_to_tokens(ocmp, nb, nq), _rows_to_tokens(osel, nb, nq), _rows_to_tokens(owin, nb, nq),
                  gt, odiff, lw["gate_e"], lw["w_out"], lw["ln1_g"], lw["ln1_b"], alpha, n)
    x2 = _ffn(x1, lw["w1"], lw["w3"], lw["w2"], lw["ln2_g"], lw["ln2_b"], alpha, n, 2)
    new_wk = jnp.concatenate([st_wk, wk.reshape(nb, nq, NSA_G, NSA_DH)], axis=1)[:, -wb:]
    new_wv = jnp.concatenate([st_wv, wv.reshape(nb, nq, NSA_G, NSA_DH)], axis=1)[:, -wb:]
    new_conv = jnp.concatenate([st_conv, u.reshape(nb, nq, 256)], axis=1)[:, -(CONV_W - 1):]
    news = (ck.reshape(nb, nq, NSA_G, NSA_DH), cv.reshape(nb, nq, NSA_G, NSA_DH),
            sk.reshape(nb, nq, NSA_G, NSA_DH), sv.reshape(nb, nq, NSA_G, NSA_DH),
            dk.reshape(nb, nq, DIFF_H, 2 * DIFF_DQK), dv.reshape(nb, nq, DIFF_H, DIFF_DV),
            new_wk, new_wv, new_conv)
    return x2, news


def kernel(x_prompt, x_sample, cache_nsa_cmp_k, cache_nsa_cmp_v, cache_nsa_sel_k, cache_nsa_sel_v, cache_diff_k, cache_diff_v, state_nsa_win_k, state_nsa_win_v, state_conv, page_table, w_in, conv_dw_w, conv_dw_b, conv_ln_g, conv_ln_b, cmp_pe_k, cmp_w1_k, cmp_b1_k, cmp_w2_k, cmp_pe_v, cmp_w1_v, cmp_b1_v, cmp_w2_v, diff_lq1, diff_lk1, diff_lq2, diff_lk2, diff_subln_g, w_out, ln1_g, ln1_b, ln2_g, ln2_b, ffn_w1, ffn_w3, ffn_w2):
    nb, t, d = x_prompt.shape
    sb, nq, _ = x_sample.shape
    depth = w_in.shape[0]
    n_pool = cache_nsa_cmp_k.shape[1]
    past = page_table.shape[1] * PAGE
    alpha = (2 * depth) ** 0.25
    tabs_p = _rope_tables(jnp.arange(t, dtype=jnp.int32))
    tabs_s = _rope_tables(jnp.tile(past + jnp.arange(nq, dtype=jnp.int32), sb))
    as_pages = lambda c: c.transpose(0, 1, 3, 4, 2).reshape(depth, n_pool, c.shape[3] * c.shape[4], PAGE)
    caches = tuple(as_pages(c) for c in (cache_nsa_cmp_k, cache_nsa_cmp_v, cache_nsa_sel_k, cache_nsa_sel_v,
                                         cache_diff_k, cache_diff_v))
    gate_e = _gate_expand()
    xp = x_prompt.reshape(nb * t, d)
    xs = x_sample.reshape(sb * nq, d)
    outs_p, outs_s = [], []
    for l in range(depth):
        ck = _prep_cmp(cmp_pe_k[l], cmp_w1_k[l], cmp_b1_k[l], cmp_w2_k[l])
        cv = _prep_cmp(cmp_pe_v[l], cmp_w1_v[l], cmp_b1_v[l], cmp_w2_v[l])
        names = ("pa", "pb", "wa", "wb", "b1", "w2")
        cmpw = {n_ + "k": a for n_, a in zip(names, ck)}
        cmpw.update({n_ + "v": a for n_, a in zip(names, cv)})
        lw = dict(
            w_in=_prep_w_in(w_in[l]), dw_w=conv_dw_w[l], dw_b=conv_dw_b[l][None], cln_g=conv_ln_g[l][None],
            cln_b=conv_ln_b[l][None], cmp=cmpw,
            lams=(diff_lq1[l][None], diff_lk1[l][None], diff_lq2[l][None], diff_lk2[l][None]),
            sg_col=diff_subln_g[l][:, None], sg_row=jnp.tile(diff_subln_g[l], DIFF_H)[None],
            gate_e=gate_e, w_out=w_out[l].astype(BF16), ln1_g=ln1_g[l][None], ln1_b=ln1_b[l][None],
            ln2_g=ln2_g[l][None], ln2_b=ln2_b[l][None],
            w1=ffn_w1[l].astype(BF16), w3=ffn_w3[l].astype(BF16), w2=ffn_w2[l].astype(BF16))
        lam_init = 0.8 - 0.6 * math.exp(-0.3 * l)
        xp, new_p = _prompt_layer(xp, lw, tabs_p, nb, t, lam_init, alpha)
        xs, new_s = _sample_layer(xs, lw, tabs_s, caches, (state_nsa_win_k[l], state_nsa_win_v[l], state_conv[l]),
                                  page_table, l, sb, nq, past, lam_init, alpha)
        outs_p.append(new_p)
        outs_s.append(new_s)
    stk_p = [jnp.stack([o[i] for o in outs_p]) for i in range(9)]
    stk_s = [jnp.stack([o[i] for o in outs_s]) for i in range(9)]
    return (xp.reshape(nb, t, d), xs.reshape(sb, nq, d), *stk_p, *stk_s)
```

```python
import functools
import math

import jax
import jax.numpy as jnp
import numpy as np
from jax import lax
from jax.experimental import pallas as pl
from jax.experimental.pallas import tpu as pltpu

F32 = jnp.float32
BF16 = jnp.bfloat16

CONV_W = 31
NSA_H = 8
NSA_G = 2
NSA_HPG = NSA_H // NSA_G
NSA_DH = 64
CMP_STRIDE = 16
CMP_LEN = 32
CMP_HID = 128
SEL_BLK = 64
SEL_TOPN = 16
WINDOW = 512
FORCE_SCORE = 1e9
DIFF_H = 4
DIFF_DV = 64
DIFF_DQK = 32
ROPE_THETA = 10000.0
LN_EPS = 1e-5
PAGE = 128

LANES = 128
TQ = 128
TK = 256
LOG2E = 1.4426950408889634
NEG = -1e30
VMEM_LIMIT = 52 * 1024 * 1024

C_CA, C_CG, C_Q, C_KV, C_GT, C_DQ, C_DK, C_DV, C_END = 0, 256, 512, 1024, 1792, 1920, 2176, 2432, 2688


def _cparams(sem):
    return pltpu.CompilerParams(dimension_semantics=sem, vmem_limit_bytes=VMEM_LIMIT)


def _sigmoid(x):
    return 1.0 / (1.0 + jnp.exp(-x))


def _ln_rows(x, g, b):
    mu = jnp.mean(x, axis=-1, keepdims=True)
    xc = x - mu
    var = jnp.mean(xc * xc, axis=-1, keepdims=True)
    return xc * lax.rsqrt(var + LN_EPS) * g + b


def _dot(a, b):
    return jnp.dot(a, b, preferred_element_type=F32)


def _dot_nt(a, b):
    return lax.dot_general(a, b, (((1,), (1,)), ((), ())), preferred_element_type=F32)


def _split3_dot(x, m):
    hi = x.astype(BF16)
    r1 = x - hi.astype(F32)
    mid = r1.astype(BF16)
    lo = (r1 - mid.astype(F32)).astype(BF16)
    return _dot(hi, m) + _dot(mid, m) + _dot(lo, m)


def _rope(x, cos, sin_signed, half):
    lane = lax.broadcasted_iota(jnp.int32, x.shape, 1)
    first = (lane & (2 * half - 1)) < half
    rot = jnp.where(first, pltpu.roll(x, LANES - half, 1), pltpu.roll(x, half, 1))
    return x * cos + rot * sin_signed


def _inproj_body(x_ref, w_ref, c64_ref, s64_ref, c32_ref, s32_ref,
                 u_ref, qu_ref, qr_ref, gt_ref, dq_ref, ck_ref, cv_ref, *refs, dq_scale, transposed):
    xb = x_ref[...].astype(BF16)

    def mm(lo, hi):
        return _dot(xb, w_ref[:, lo:hi])

    c64, s64, c32, s32 = c64_ref[...], s64_ref[...], c32_ref[...], s32_ref[...]
    z = mm(C_CA, C_Q)
    u_ref[...] = z[:, :256] * _sigmoid(z[:, 256:])
    lane = lax.broadcasted_iota(jnp.int32, (x_ref.shape[0], LANES), 1)
    for j in range(4):
        zq = mm(C_Q + LANES * j, C_Q + LANES * (j + 1))
        g = j // 2
        keep = (lane < 64) if g == 0 else (lane >= 64)
        for src, dst_ref in ((zq, qu_ref), (_rope(zq, c64, s64, 32), qr_ref)):
            for e in range(2):
                v = src if e == g else pltpu.roll(src, 64, 1)
                dst_ref[:, LANES * (2 * j + e):LANES * (2 * j + e + 1)] = jnp.where(keep, v, 0.0).astype(BF16)
    z = mm(C_KV, C_GT)
    ck = z[:, 0:128]
    cv = z[:, 128:256]
    ck_ref[...] = ck
    cv_ref[...] = cv
    sk = _rope(z[:, 256:384], c64, s64, 32)
    sv = z[:, 384:512]
    wk = _rope(z[:, 512:640], c64, s64, 32)
    wv = z[:, 640:768]
    gt_ref[...] = _sigmoid(mm(C_GT, C_DQ))
    for c in range(2):
        zq = _rope(mm(C_DQ + LANES * c, C_DQ + LANES * (c + 1)), c32, s32, 16) * dq_scale
        for a in range(4):
            dq_ref[:, LANES * (4 * c + a):LANES * (4 * c + a + 1)] = jnp.where((lane >> 5) == a, zq, 0.0).astype(BF16)
    z = mm(C_DK, C_DV)
    dk = [_rope(z[:, :128], c32, s32, 16), _rope(z[:, 128:], c32, s32, 16)]
    z = mm(C_DV, C_END)
    dv = [z[:, :128], z[:, 128:]]
    if not transposed:
        sk_ref, sv_ref, wk_ref, wv_ref, dk_ref, dv_ref = refs
        sk_ref[...] = sk
        sv_ref[...] = sv
        wk_ref[...] = wk
        wv_ref[...] = wv
        for c in range(2):
            dk_ref[:, LANES * c:LANES * (c + 1)] = dk[c]
            dv_ref[:, LANES * c:LANES * (c + 1)] = dv[c]
        return
    (skb_ref, wkb_ref, dkb_ref, svb_ref, wvb_ref, dvb_ref,
     ckt_ref, cvt_ref, skt_ref, svt_ref, wkt_ref, wvt_ref, dkt_ref, dvt_ref) = refs
    skb_ref[...] = sk.astype(BF16)
    wkb_ref[...] = wk.astype(BF16)
    ckt_ref[...] = ck.T
    cvt_ref[...] = cv.T
    skt_ref[...] = sk.T
    wkt_ref[...] = wk.T
    svt = sv.T
    svt_ref[...] = svt
    svb_ref[...] = svt.astype(BF16)
    wvt = wv.T
    wvt_ref[...] = wvt
    wvb_ref[...] = wvt.astype(BF16)
    for c in range(2):
        rows = slice(LANES * c, LANES * (c + 1))
        dkb_ref[:, rows] = dk[c].astype(BF16)
        dkt_ref[rows, :] = dk[c].T
        dvt = dv[c].T
        dvt_ref[rows, :] = dvt
        dvb_ref[rows, :] = dvt.astype(BF16)


def _inproj(x, w, tabs, nb, t, tm, transposed):
    n, d = x.shape
    nt = t // tm if transposed else 1
    grid = (n // tm,)
    row = lambda c: pl.BlockSpec((tm, c), lambda i: (i, 0))
    tab = pl.BlockSpec((tm, LANES), (lambda i: (i % nt, 0)) if transposed else (lambda i: (i, 0)))
    in_specs = [row(d), pl.BlockSpec((d, C_END), lambda i: (0, 0)), tab, tab, tab, tab]
    shapes = [(256, F32), (1024, BF16), (1024, BF16), (128, F32), (1024, BF16), (128, F32), (128, F32)]
    if transposed:
        shapes += [(128, BF16), (128, BF16), (256, BF16)]
    else:
        shapes += [(128, F32)] * 4 + [(256, F32)] * 2
    out_shape = [jax.ShapeDtypeStruct((n, c), dt) for c, dt in shapes]
    out_specs = [row(c) for c, _ in shapes]
    if transposed:
        for c, dt in [(128, BF16), (128, BF16), (256, BF16)] + [(128, F32)] * 6 + [(256, F32)] * 2:
            out_shape.append(jax.ShapeDtypeStruct((nb, c, t), dt))
            out_specs.append(pl.BlockSpec((None, c, tm), lambda i: (i // nt, 0, i % nt)))
    return pl.pallas_call(
        functools.partial(_inproj_body, dq_scale=DIFF_DQK ** -0.5 * LOG2E, transposed=transposed), name="inproj",
        grid=grid, in_specs=in_specs, out_specs=out_specs, out_shape=out_shape,
        compiler_params=_cparams(("parallel",)),
    )(x, w, *tabs)


CONV_PAD = 32


def _conv_body(ext_ref, w_ref, b_ref, g_ref, beta_ref, y_ref, acc_ref, sh_ref, *, tt, rs):
    t0 = pl.multiple_of(pl.program_id(1) * tt, 8)
    off = CONV_PAD - (CONV_W - 1)
    for c in range(2):
        cs = slice(LANES * c, LANES * (c + 1))
        for r in range(tt // rs):
            win = ext_ref[pl.ds(t0 + rs * r, rs + CONV_PAD), cs]
            acc = jnp.zeros((rs, LANES), F32)
            for r8 in range(8):
                taps = range(r8, CONV_W, 8)
                rows = 8 * (len(taps) - 1) + rs
                sh_ref[0:rows, :] = win[off + r8:off + r8 + rows, :]
                for a, k in enumerate(taps):
                    acc = acc + sh_ref[8 * a:8 * a + rs, :] * w_ref[k:k + 1, cs]
            acc_ref[rs * r:rs * (r + 1), cs] = acc
    y = _ln_rows(acc_ref[...] + b_ref[...], g_ref[...], beta_ref[...])
    y_ref[...] = y * _sigmoid(y)


def _conv(ext, w, b, g, beta, t, tt):
    nb, le, c = ext.shape
    rs = min(tt, 64)
    vec = pl.BlockSpec((1, c), lambda bi, ti: (0, 0))
    return pl.pallas_call(
        functools.partial(_conv_body, tt=tt, rs=rs), name="conv",
        grid=(nb, t // tt),
        in_specs=[pl.BlockSpec((None, le, c), lambda bi, ti: (bi, 0, 0)),
                  pl.BlockSpec((CONV_W, c), lambda bi, ti: (0, 0)), vec, vec, vec],
        out_specs=pl.BlockSpec((None, tt, c), lambda bi, ti: (bi, ti, 0)),
        out_shape=jax.ShapeDtypeStruct((nb, t, c), F32),
        scratch_shapes=[pltpu.VMEM((tt, c), F32), pltpu.VMEM((rs + CONV_PAD, LANES), F32)],
        compiler_params=_cparams(("parallel", "parallel")),
    )(ext, w, b, g, beta)


def _compress(src_ref, n16, pea, peb, wa_ref, wb_ref, b1, w2_ref):
    x = jnp.concatenate([src_ref[pl.ds(p, n16, stride=CMP_STRIDE), :] for p in range(CMP_STRIDE)], axis=1)
    a = _dot((x + pea).astype(BF16), wa_ref[...])
    bm = _dot((x + peb).astype(BF16), wb_ref[...])
    h = a + pltpu.roll(bm, n16 - 1, 0) + b1
    gl = 0.5 * h * (1.0 + jnp.tanh(0.7978845608028654 * (h + 0.044715 * (h * h * h))))
    return _dot(gl.astype(BF16), w2_ref[...])


def _top_rows(score, ridx, k):
    nv = score.shape[0] // 8
    tiles = [score[8 * v:8 * (v + 1), :] for v in range(nv)]
    cnts = [jnp.zeros(t.shape, F32) for t in tiles]
    for jp in range(score.shape[0]):
        row = score[jp:jp + 1, :]
        for v in range(nv):
            if 8 * v > jp:
                beats = row >= tiles[v]
            elif 8 * v + 7 < jp:
                beats = row > tiles[v]
            else:
                beats = (row > tiles[v]) | ((row == tiles[v]) & (ridx[8 * v:8 * (v + 1), :] > jp))
            cnts[v] = cnts[v] + jnp.where(beats, 1.0, 0.0)
    return jnp.concatenate(cnts, axis=0) < k


def _pcmp_body(qu_ref, ck_ref, cv_ref, pak_ref, pbk_ref, wak_ref, wbk_ref, b1k_ref, w2k_ref,
               pav_ref, pbv_ref, wav_ref, wbv_ref, b1v_ref, w2v_ref,
               o_ref, sel_ref, kcc_ref, vcct_ref, imp_ref, *, n16, nsel):
    @pl.when(pl.program_id(1) == 0)
    def _():
        kcc = _compress(ck_ref, n16, pak_ref[...], pbk_ref[...], wak_ref, wbk_ref, b1k_ref[...], w2k_ref)
        kcc_ref[...] = kcc.astype(BF16)
        vcc = _compress(cv_ref, n16, pav_ref[...], pbv_ref[...], wav_ref, wbv_ref, b1v_ref[...], w2v_ref)
        vcct_ref[...] = vcc.T.astype(BF16)

    for u in range(PCMP_TILES):
        _pcmp_tile(pl.program_id(1) * PCMP_TILES + u, qu_ref.at[TQ * u:TQ * (u + 1), :], kcc_ref[...], vcct_ref[...],
                   imp_ref.at[u], o_ref.at[TQ * u:TQ * (u + 1), :], sel_ref.at[u], n16, nsel)


PCMP_TILES = 2


def _pcmp_tile(ti, qu_ref, kcc, vcct, imp_ref, o_ref, sel_ref, n16, nsel):
    q = qu_ref[...]
    nq = 4 * TQ
    cidx = lax.broadcasted_iota(jnp.int32, (n16, nq), 0)
    qpos = ti * TQ + (lax.broadcasted_iota(jnp.int32, (n16, nq), 1) & (TQ - 1))
    vis = (CMP_STRIDE * cidx + CMP_LEN - 1 <= qpos) & (cidx < n16 - 1)
    jidx = lax.broadcasted_iota(jnp.int32, (nsel, TQ), 0)
    qp1 = ti * TQ + lax.broadcasted_iota(jnp.int32, (nsel, TQ), 1)
    cur = qp1 >> 6
    forced = (jidx == 0) | (jidx == cur) | (jidx == cur - 1)
    for g in range(NSA_G):
        qs = jnp.concatenate([q[:, LANES * (4 * g + a):LANES * (4 * g + a + 1)] for a in range(4)], axis=0)
        st = _dot_nt(kcc, qs)
        st = jnp.where(vis, st, NEG)
        m = jnp.max(st, axis=0, keepdims=True)
        p = jnp.exp2(st - m)
        l = jnp.sum(p, axis=0, keepdims=True)
        inv = jnp.where(m > 0.5 * NEG, 1.0 / jnp.maximum(l, 1e-30), 0.0)
        p = p * inv
        ot = _dot(vcct[64 * g:64 * (g + 1), :], p.astype(BF16))
        for a2 in range(2):
            blk = jnp.concatenate([ot[:, TQ * (2 * a2 + e):TQ * (2 * a2 + e + 1)] for e in range(2)], axis=0)
            o_ref[:, 256 * g + LANES * a2:256 * g + LANES * (a2 + 1)] = blk.T
        imp_ref[g, 0:8, :] = jnp.zeros((8, TQ), F32)
        imp_ref[g, 8:8 + n16, :] = p[:, 0:TQ] + p[:, TQ:2 * TQ] + p[:, 2 * TQ:3 * TQ] + p[:, 3 * TQ:4 * TQ]
        r = SEL_BLK // CMP_STRIDE
        isel = imp_ref[g, pl.ds(7, nsel, stride=r), :]
        for e in range(1, r + 1):
            isel = isel + imp_ref[g, pl.ds(7 + e, nsel, stride=r), :]
        score = jnp.where(forced, FORCE_SCORE, isel)
        score = jnp.where(jidx <= cur, score, -jnp.inf)
        sel = _top_rows(score, jidx, min(SEL_TOPN, nsel)) & (jidx <= cur)
        sel_ref[g] = jnp.where(sel, 0.0, NEG)


def _pcmp(qu, ck, cv, cw, nb, t):
    n16 = t // CMP_STRIDE
    nsel = t // SEL_BLK
    nt = t // (TQ * PCMP_TILES)
    const = lambda a: pl.BlockSpec(a.shape, lambda bi, ti: (0,) * a.ndim)
    wlist = [cw[k] for k in ("pak", "pbk", "wak", "wbk", "b1k", "w2k", "pav", "pbv", "wav", "wbv", "b1v", "w2v")]
    return pl.pallas_call(
        functools.partial(_pcmp_body, n16=n16, nsel=nsel), name="pcmp",
        grid=(nb, nt),
        in_specs=[pl.BlockSpec((TQ * PCMP_TILES, 1024), lambda bi, ti: (bi * nt + ti, 0)),
                  pl.BlockSpec((None, t, LANES), lambda bi, ti: (bi, 0, 0)),
                  pl.BlockSpec((None, t, LANES), lambda bi, ti: (bi, 0, 0))] + [const(a) for a in wlist],
        out_specs=[pl.BlockSpec((TQ * PCMP_TILES, 512), lambda bi, ti: (bi * nt + ti, 0)),
                   pl.BlockSpec((None, PCMP_TILES, NSA_G, nsel, TQ), lambda bi, ti: (bi, ti, 0, 0, 0))],
        out_shape=[jax.ShapeDtypeStruct((nb * t, 512), F32),
                   jax.ShapeDtypeStruct((nb, t // TQ, NSA_G, nsel, TQ), F32)],
        scratch_shapes=[pltpu.VMEM((n16, LANES), BF16), pltpu.VMEM((LANES, n16), BF16),
                        pltpu.VMEM((PCMP_TILES, NSA_G, n16 + 8, TQ), F32)],
        compiler_params=_cparams(("parallel", "arbitrary")),
    )(qu, ck.reshape(nb, t, LANES), cv.reshape(nb, t, LANES), *wlist)


def _stack_queries(q, s):
    return jnp.concatenate([q[:, LANES * (4 * s + a):LANES * (4 * s + a + 1)] for a in range(4)], axis=0)


def _flash_out(z, s, o_ref):
    tq = z.shape[1] // 4
    for a2 in range(2):
        blk = jnp.concatenate([z[:, tq * (2 * a2 + e):tq * (2 * a2 + e + 1)] for e in range(2)], axis=0)
        o_ref[:, 256 * s + LANES * a2:256 * s + LANES * (a2 + 1)] = blk.T


def _flash_body(*refs, mode, lam_init):
    if mode == "sel":
        q_ref, k_ref, vt_ref, selb_ref, o_ref, acc_ref, m_ref, l_ref, al_ref, s_ref, p_ref = refs
    else:
        (q_ref, k_ref, vt_ref, lq1_ref, lk1_ref, lq2_ref, lk2_ref, sg_ref, o_ref,
         acc_ref, m_ref, l_ref, al_ref, s_ref, p_ref) = refs
    ti = pl.program_id(1)
    tq = TQ * FLASH_TILES
    nq = 4 * tq
    q = q_ref[...]
    qs = [_stack_queries(q, s) for s in range(2)]
    koffs = [LANES * s if mode == "diff" else 0 for s in range(2)]
    m_ref[...] = jnp.full(m_ref.shape, NEG, F32)
    l_ref[...] = jnp.zeros(l_ref.shape, F32)
    acc_ref[...] = jnp.zeros(acc_ref.shape, F32)
    al_ref[...] = jnp.ones(al_ref.shape, F32)
    p_ref[...] = jnp.zeros(p_ref.shape, BF16)

    def scores(j, slot):
        k0 = pl.multiple_of(j * TK, TK)
        for s in range(2):
            s_ref[slot, s] = _dot_nt(k_ref[pl.ds(k0, TK), koffs[s]:koffs[s] + LANES], qs[s])

    def accumulate(j):
        k0 = pl.multiple_of(j * TK, TK)
        for s in range(2):
            if mode == "sel":
                acc_ref[s] = al_ref[s] * acc_ref[s] + _dot(vt_ref[64 * s:64 * (s + 1), pl.ds(k0, TK)], p_ref[s])
            else:
                for hh in range(2):
                    cols = slice(2 * tq * hh, 2 * tq * (hh + 1))
                    acc_ref[s, :, cols] = (al_ref[s, :, cols] * acc_ref[s, :, cols]
                                           + _dot(vt_ref[koffs[s] + 64 * hh:koffs[s] + 64 * (hh + 1), pl.ds(k0, TK)], p_ref[s, :, cols]))

    def softmax(j, slot, causal):
        for s in range(2):
            st = s_ref[slot, s]
            if mode == "sel":
                rows = [jnp.broadcast_to(jnp.concatenate(
                    [selb_ref[u, s, pl.ds((TK // SEL_BLK) * j + r, 1), :] for u in range(FLASH_TILES)], axis=1), (SEL_BLK, tq))
                    for r in range(TK // SEL_BLK)]
                bias = jnp.concatenate(rows, axis=0)
                st = st + jnp.concatenate([bias, bias, bias, bias], axis=1)
            if causal:
                krow = lax.broadcasted_iota(jnp.int32, (TK, nq), 0)
                qcol = lax.broadcasted_iota(jnp.int32, (TK, nq), 1) & (tq - 1)
                st = jnp.where(ti * tq + qcol >= j * TK + krow, st, NEG)
            m_old = m_ref[s]
            m_new = jnp.maximum(m_old, jnp.max(st, axis=0, keepdims=True))
            alpha = jnp.exp2(m_old - m_new)
            p = jnp.exp2(st - m_new)
            l_ref[s] = alpha * l_ref[s] + jnp.sum(p, axis=0, keepdims=True)
            p_ref[s] = p.astype(BF16)
            al_ref[s] = alpha
            m_ref[s] = m_new

    n_full = (ti * tq) // TK
    scores(0, 0)

    def body(j, c):
        slot = j & 1
        accumulate(jnp.maximum(j - 1, 0))
        softmax(j, slot, False)
        scores(j + 1, 1 - slot)
        return c

    lax.fori_loop(0, n_full, body, 0)
    accumulate(jnp.maximum(n_full - 1, 0))
    softmax(n_full, n_full & 1, True)
    accumulate(n_full)

    if mode == "diff":
        lam = (jnp.exp(jnp.sum(lq1_ref[...] * lk1_ref[...], keepdims=True))
               - jnp.exp(jnp.sum(lq2_ref[...] * lk2_ref[...], keepdims=True)) + lam_init)
    for s in range(2):
        z = acc_ref[s] * (1.0 / l_ref[s])
        if mode == "diff":
            halves = []
            for hh in range(2):
                d = z[:, tq * (2 * hh):tq * (2 * hh + 1)] - lam * z[:, tq * (2 * hh + 1):tq * (2 * hh + 2)]
                ms = jnp.mean(d * d, axis=0, keepdims=True)
                halves.append(d * lax.rsqrt(ms + LN_EPS) * sg_ref[...] * (1.0 - lam_init))
            o_ref[:, LANES * s:LANES * (s + 1)] = jnp.concatenate(halves, axis=0).T
        else:
            _flash_out(z, s, o_ref)


FLASH_TILES = 1


def _flash(mode, q, k, vt, nb, t, extra=(), lam_init=0.0):
    tq = TQ * FLASH_TILES
    assert t % tq == 0 and t % TK == 0 and (tq % TK == 0 or TK % tq == 0)
    nt = t // tq
    kw = k.shape[-1]
    ow = 256 if mode == "diff" else 512
    in_specs = [pl.BlockSpec((tq, 1024), lambda bi, ti: (bi * nt + ti, 0)),
                pl.BlockSpec((None, t, kw), lambda bi, ti: (bi, 0, 0)),
                pl.BlockSpec((None, kw, t), lambda bi, ti: (bi, 0, 0))]
    if mode == "sel":
        nsel = t // SEL_BLK
        in_specs.append(pl.BlockSpec((None, FLASH_TILES, NSA_G, nsel, TQ), lambda bi, ti: (bi, ti, 0, 0, 0)))
    else:
        in_specs += [pl.BlockSpec(a.shape, lambda bi, ti: (0, 0)) for a in extra]
    return pl.pallas_call(
        functools.partial(_flash_body, mode=mode, lam_init=lam_init), name="flash_" + mode,
        grid=(nb, nt), in_specs=in_specs,
        out_specs=pl.BlockSpec((tq, ow), lambda bi, ti: (bi * nt + ti, 0)),
        out_shape=jax.ShapeDtypeStruct((nb * t, ow), F32),
        scratch_shapes=[pltpu.VMEM((2, 64, 4 * tq), F32)] + [pltpu.VMEM((2, 1, 4 * tq), F32)] * 3
        + [pltpu.VMEM((2, 2, TK, 4 * tq), F32), pltpu.VMEM((2, TK, 4 * tq), BF16)],
        compiler_params=_cparams(("parallel", "parallel")),
    )(q, k.reshape(nb, t, kw), vt, *extra)


def _win_body(q_ref, k_ref, vt_ref, o_ref):
    nq = 4 * TQ
    span = WINDOW + TQ

    def tiles(interior):
        if interior:
            krow = lax.broadcasted_iota(jnp.int32, (TQ, nq), 0)
            qcol = lax.broadcasted_iota(jnp.int32, (TQ, nq), 1) & (TQ - 1)
        else:
            krow = lax.broadcasted_iota(jnp.int32, (span, nq), 0)
            qcol = lax.broadcasted_iota(jnp.int32, (span, nq), 1) & (TQ - 1)
        for u in range(WIN_TILES):
            ti = pl.program_id(1) * WIN_TILES + u
            k0 = pl.multiple_of(jnp.maximum(ti * TQ - WINDOW, 0), TQ)
            q = q_ref[TQ * u:TQ * (u + 1), :]
            kt = k_ref[pl.ds(k0, span), :]
            vt = vt_ref[:, pl.ds(k0, span)]
            for s in range(2):
                st = _dot_nt(kt, _stack_queries(q, s))
                if interior:
                    st = jnp.concatenate([jnp.where(krow > qcol, st[:TQ], NEG), st[TQ:WINDOW],
                                          jnp.where(krow <= qcol, st[WINDOW:], NEG)], axis=0)
                else:
                    dlt = ti * TQ + qcol - (k0 + krow)
                    st = jnp.where((dlt >= 0) & (dlt < WINDOW), st, NEG)
                m = jnp.max(st, axis=0, keepdims=True)
                p = jnp.exp2(st - m)
                l = jnp.sum(p, axis=0, keepdims=True)
                _flash_out(_dot(vt[64 * s:64 * (s + 1), :], p.astype(BF16)) * (1.0 / l), s, o_ref.at[TQ * u:TQ * (u + 1), :])

    first_interior = -(-WINDOW // (TQ * WIN_TILES))

    @pl.when(pl.program_id(1) >= first_interior)
    def _():
        tiles(True)

    @pl.when(pl.program_id(1) < first_interior)
    def _():
        tiles(False)


WIN_TILES = 4


def _win(q, k, vt, nb, t):
    nt = t // (TQ * WIN_TILES)
    assert t >= WINDOW + TQ and t % (TQ * WIN_TILES) == 0
    return pl.pallas_call(
        _win_body, name="flash_win", grid=(nb, nt),
        in_specs=[pl.BlockSpec((TQ * WIN_TILES, 1024), lambda bi, ti: (bi * nt + ti, 0)),
                  pl.BlockSpec((None, t, LANES), lambda bi, ti: (bi, 0, 0)),
                  pl.BlockSpec((None, LANES, t), lambda bi, ti: (bi, 0, 0))],
        out_specs=pl.BlockSpec((TQ * WIN_TILES, 512), lambda bi, ti: (bi * nt + ti, 0)),
        out_shape=jax.ShapeDtypeStruct((nb * t, 512), F32),
        compiler_params=_cparams(("parallel", "parallel")),
    )(q, k.reshape(nb, t, LANES), vt)


def _outproj_body(x_ref, yc_ref, oc_ref, os_ref, ow_ref, gt_ref, od_ref, e_ref, w_ref, g_ref, b_ref, o_ref, *, alpha):
    gt = gt_ref[...]
    hi = gt.astype(BF16)
    lo = (gt - hi.astype(F32)).astype(BF16)
    e = e_ref[...]
    gx = _dot(hi, e) + _dot(lo, e)
    onsa = gx[:, 0:512] * oc_ref[...] + gx[:, 512:1024] * os_ref[...] + gx[:, 1024:1536] * ow_ref[...]
    mix = (_dot(yc_ref[...].astype(BF16), w_ref[0:256, :]) + _dot(onsa.astype(BF16), w_ref[256:768, :])
           + _dot(od_ref[...].astype(BF16), w_ref[768:1024, :]))
    o_ref[...] = _ln_rows(alpha * x_ref[...] + mix, g_ref[...], b_ref[...])


def _outproj(x, yc, oc, osel, ow, gt, od, e, w, g, b, alpha, tm):
    n, d = x.shape
    row = lambda c: pl.BlockSpec((tm, c), lambda i: (i, 0))
    const = lambda a: pl.BlockSpec(a.shape, lambda i: (0, 0))
    return pl.pallas_call(
        functools.partial(_outproj_body, alpha=alpha), name="outproj",
        grid=(n // tm,),
        in_specs=[row(d), row(256), row(512), row(512), row(512), row(128), row(256), const(e), const(w), const(g), const(b)],
        out_specs=row(d), out_shape=jax.ShapeDtypeStruct((n, d), F32),
        compiler_params=_cparams(("parallel",)),
    )(x, yc, oc, osel, ow, gt, od, e, w, g, b)


def _ffn_body(x_ref, w1_ref, w3_ref, w2_ref, g_ref, b_ref, o_ref, acc_ref, *, alpha):
    f = pl.program_id(1)
    xb = x_ref[...].astype(BF16)
    h1 = _dot(xb, w1_ref[...])
    h = h1 * _sigmoid(h1) * _dot(xb, w3_ref[...])
    part = _dot(h.astype(BF16), w2_ref[...])

    @pl.when(f == 0)
    def _():
        acc_ref[...] = part

    @pl.when(f != 0)
    def _():
        acc_ref[...] += part

    @pl.when(f == pl.num_programs(1) - 1)
    def _():
        o_ref[...] = _ln_rows(alpha * x_ref[...] + acc_ref[...], g_ref[...], b_ref[...])


def _ffn(x, w1, w3, w2, g, b, alpha, tm, nf):
    n, d = x.shape
    dff = w1.shape[1]
    tf = dff // nf
    return pl.pallas_call(
        functools.partial(_ffn_body, alpha=alpha), name="ffn",
        grid=(n // tm, nf),
        in_specs=[pl.BlockSpec((tm, d), lambda i, f: (i, 0)),
                  pl.BlockSpec((d, tf), lambda i, f: (0, f)),
                  pl.BlockSpec((d, tf), lambda i, f: (0, f)),
                  pl.BlockSpec((tf, d), lambda i, f: (f, 0)),
                  pl.BlockSpec((1, d), lambda i, f: (0, 0)),
                  pl.BlockSpec((1, d), lambda i, f: (0, 0))],
        out_specs=pl.BlockSpec((tm, d), lambda i, f: (i, 0)),
        out_shape=jax.ShapeDtypeStruct((n, d), F32),
        scratch_shapes=[pltpu.VMEM((tm, d), F32)],
        compiler_params=_cparams(("parallel", "arbitrary")),
    )(x, w1, w3, w2, g, b)


def _gather_pages(pt_ref, n_pages, srcs, sem, layer):
    b = pl.program_id(0)
    slot = b & 1

    def copies(bb, sl):
        return [pltpu.make_async_copy(hbm.at[layer, pt_ref[bb, j]], buf.at[sl, :, pl.ds(PAGE * j, PAGE)], sem.at[a, sl])
                for a, (hbm, buf) in enumerate(srcs) for j in range(n_pages)]

    @pl.when(b == 0)
    def _():
        for c in copies(0, 0):
            c.start()

    @pl.when(b + 1 < pl.num_programs(0))
    def _():
        for c in copies(b + 1, 1 - slot):
            c.start()

    for c in copies(b, slot):
        c.wait()
    return slot


def _key_chunks(n):
    per = max(d for d in range(1, 17) if n % d == 0)
    return [(PAGE * per * c, PAGE * per) for c in range(n // per)]


def _stack_heads(q):
    return jnp.concatenate([q[:, LANES * h:LANES * (h + 1)] for h in range(8)], axis=0)


def _scmp_body(pt_ref, qu_ref, ck_hbm, cv_hbm, pak_ref, pbk_ref, wak_ref, wbk_ref, b1k_ref, w2k_ref,
               pav_ref, pbv_ref, wav_ref, wbv_ref, b1v_ref, w2v_ref, m_ref,
               o_ref, sel_ref, kbt, vbt, kbuf, vbuf, sem, *, layer, n_pages, past, nq):
    slot = _gather_pages(pt_ref, n_pages, [(ck_hbm, kbt), (cv_hbm, vbt)], sem, layer)
    for j in range(n_pages):
        kbuf[PAGE * j:PAGE * (j + 1), :] = kbt[slot, :, PAGE * j:PAGE * (j + 1)].T
        vbuf[PAGE * j:PAGE * (j + 1), :] = vbt[slot, :, PAGE * j:PAGE * (j + 1)].T
    n16 = past // CMP_STRIDE
    kcc = _compress(kbuf, n16, pak_ref[...], pbk_ref[...], wak_ref, wbk_ref, b1k_ref[...], w2k_ref).astype(BF16)
    vcc = _compress(vbuf, n16, pav_ref[...], pbv_ref[...], wav_ref, wbv_ref, b1v_ref[...], w2v_ref).astype(BF16)
    qa = _stack_heads(qu_ref[...])
    rows = 8 * nq
    s = _dot_nt(qa, kcc)
    cidx = lax.broadcasted_iota(jnp.int32, (rows, n16), 1)
    qpos = past + (lax.broadcasted_iota(jnp.int32, (rows, n16), 0) & (nq - 1))
    vis = (CMP_STRIDE * cidx + CMP_LEN - 1 <= qpos) & (cidx < n16 - 1)
    s = jnp.where(vis, s, NEG)
    m = jnp.max(s, axis=-1, keepdims=True)
    p = jnp.where(vis, jnp.exp2(s - m), 0.0)
    l = jnp.sum(p, axis=-1, keepdims=True)
    p = p / jnp.maximum(l, 1e-30)
    o_ref[...] = _dot(p.astype(BF16), vcc)
    imp = jnp.concatenate(
        [p[nq * 4 * g:nq * (4 * g + 1)] + p[nq * (4 * g + 1):nq * (4 * g + 2)]
         + p[nq * (4 * g + 2):nq * (4 * g + 3)] + p[nq * (4 * g + 3):nq * (4 * g + 4)] for g in range(NSA_G)], axis=0)
    isel = _split3_dot(imp, m_ref[...])
    nselp = isel.shape[1]
    jidx = lax.broadcasted_iota(jnp.int32, (NSA_G * nq, nselp), 1)
    qp = past + (lax.broadcasted_iota(jnp.int32, (NSA_G * nq, nselp), 0) & (nq - 1))
    cur = qp >> 6
    forced = (jidx == 0) | (jidx == cur) | (jidx == cur - 1)
    score = jnp.where(forced, FORCE_SCORE, isel)
    sel_ref[...] = jnp.where(jidx <= cur, score, -jnp.inf)


def _stopk_body(score_ref, sel_ref, *, k, past, nq):
    score = score_ref[...]
    nselp = score.shape[1]
    jidx = lax.broadcasted_iota(jnp.int32, score.shape, 1)
    cur = (past + (lax.broadcasted_iota(jnp.int32, score.shape, 0) & (nq - 1))) >> 6
    sel = jnp.zeros(score.shape, F32)
    for _ in range(k):
        mx = jnp.max(score, axis=-1, keepdims=True)
        first = jnp.min(jnp.where(score == mx, jidx, nselp), axis=-1, keepdims=True)
        hit = jidx == first
        sel = jnp.where(hit, 1.0, sel)
        score = jnp.where(hit, -jnp.inf, score)
    sel_ref[...] = jnp.where(jidx <= cur, sel, 0.0)


def _stopk(score, past, nq):
    nb, r, nselp = score.shape
    n_sel = -(-(past + nq) // SEL_BLK)
    full = pl.BlockSpec((nb * r, nselp), lambda i: (0, 0))
    sel = pl.pallas_call(
        functools.partial(_stopk_body, k=min(SEL_TOPN, n_sel), past=past, nq=nq), name="stopk",
        grid=(1,), in_specs=[full], out_specs=full,
        out_shape=jax.ShapeDtypeStruct((nb * r, nselp), F32),
        compiler_params=_cparams(("arbitrary",)),
    )(score.reshape(nb * r, nselp))
    return sel.reshape(nb, r, nselp)


def _scmp(pt, qu, cache_k, cache_v, cw, layer, nb, nq, past):
    n_pages = past // PAGE
    n16 = past // CMP_STRIDE
    n_sel = -(-(past + nq) // SEL_BLK)
    nselp = -(-n_sel // LANES) * LANES
    m = np.zeros((n16, nselp), np.float32)
    for j in range(n_sel):
        for c in range(4 * j - 1, 4 * j + 4):
            if 0 <= c < n16 - 1:
                m[c, j] = 1.0
    m = jnp.asarray(m, BF16)
    wlist = [cw[k] for k in ("pak", "pbk", "wak", "wbk", "b1k", "w2k", "pav", "pbv", "wav", "wbv", "b1v", "w2v")] + [m]
    const = lambda a: pl.BlockSpec(a.shape, lambda bi, pt_: (0,) * a.ndim)
    any_spec = pl.BlockSpec(memory_space=pl.ANY)
    grid_spec = pltpu.PrefetchScalarGridSpec(
        num_scalar_prefetch=1, grid=(nb,),
        in_specs=[pl.BlockSpec((nq, 1024), lambda bi, pt_: (bi, 0)), any_spec, any_spec] + [const(a) for a in wlist],
        out_specs=[pl.BlockSpec((None, 8 * nq, LANES), lambda bi, pt_: (bi, 0, 0)),
                   pl.BlockSpec((None, NSA_G * nq, nselp), lambda bi, pt_: (bi, 0, 0))],
        scratch_shapes=[pltpu.VMEM((2, LANES, past), F32), pltpu.VMEM((2, LANES, past), F32),
                        pltpu.VMEM((past, LANES), F32), pltpu.VMEM((past, LANES), F32), pltpu.SemaphoreType.DMA((2, 2))])
    return pl.pallas_call(
        functools.partial(_scmp_body, layer=layer, n_pages=n_pages, past=past, nq=nq), name="scmp",
        grid_spec=grid_spec,
        out_shape=[jax.ShapeDtypeStruct((nb, 8 * nq, LANES), F32), jax.ShapeDtypeStruct((nb, NSA_G * nq, nselp), F32)],
        compiler_params=_cparams(("arbitrary",)),
    )(pt, qu, cache_k, cache_v, *wlist)


def _softmax_rows(s):
    m = jnp.max(s, axis=-1, keepdims=True)
    p = jnp.exp2(s - m)
    return p, jnp.sum(p, axis=-1, keepdims=True)


def _sselwin_body(pt_ref, qr_ref, sel_ref, skn_ref, svn_ref, wkn_ref, wvn_ref, wks_ref, wvs_ref, e_ref, sk_hbm, sv_hbm,
                  osel_ref, owin_ref, kbuf, vbuf, sem, *, layer, n_pages, past, nq):
    ktot = past + PAGE
    rows = 8 * nq
    slot = _gather_pages(pt_ref, n_pages, [(sk_hbm, kbuf), (sv_hbm, vbuf)], sem, layer)
    pad = jnp.zeros((PAGE - nq, LANES), F32)
    kbuf[slot, :, pl.ds(past, PAGE)] = jnp.concatenate([skn_ref[...], pad], axis=0).T
    vbuf[slot, :, pl.ds(past, PAGE)] = jnp.concatenate([svn_ref[...], pad], axis=0).T
    chunks = _key_chunks(n_pages + 1)
    qa = _stack_heads(qr_ref[...])
    s = jnp.concatenate([_dot(qa, kbuf[slot, :, pl.ds(c0, cn)].astype(BF16)) for c0, cn in chunks], axis=1)
    se = _dot(sel_ref[...].astype(BF16), e_ref[...])
    se = jnp.concatenate([se[nq * g:nq * (g + 1)] for g in range(NSA_G) for _ in range(NSA_HPG)], axis=0)
    kpos = lax.broadcasted_iota(jnp.int32, (rows, ktot), 1)
    qpos = past + (lax.broadcasted_iota(jnp.int32, (rows, ktot), 0) & (nq - 1))
    s = jnp.where((se > 0.5) & (kpos <= qpos), s, NEG)
    p, l = _softmax_rows(s)
    pb = p.astype(BF16)
    o = _dot_nt(pb[:, 0:chunks[0][1]], vbuf[slot, :, pl.ds(0, chunks[0][1])].astype(BF16))
    for c0, cn in chunks[1:]:
        o = o + _dot_nt(pb[:, c0:c0 + cn], vbuf[slot, :, pl.ds(c0, cn)].astype(BF16))
    osel_ref[...] = o / l
    wb = wks_ref.shape[0]
    kw = jnp.concatenate([wks_ref[...], wkn_ref[...], pad], axis=0).astype(BF16)
    vw = jnp.concatenate([wvs_ref[...], wvn_ref[...], pad], axis=0).astype(BF16)
    s = _dot_nt(qa, kw)
    i = lax.broadcasted_iota(jnp.int32, (rows, wb + PAGE), 1)
    kp = jnp.where(i < wb, past - wb + i, past + i - wb)
    qp = past + (lax.broadcasted_iota(jnp.int32, (rows, wb + PAGE), 0) & (nq - 1))
    dlt = qp - kp
    s = jnp.where((dlt >= 0) & (dlt < WINDOW) & (i < wb + nq), s, NEG)
    p, l = _softmax_rows(s)
    owin_ref[...] = _dot(p.astype(BF16), vw) / l


def _sselwin(pt, qr, sel, skn, svn, wkn, wvn, wks, wvs, cache_k, cache_v, layer, nb, nq, past):
    n_pages = past // PAGE
    ktot = past + PAGE
    nselp = sel.shape[-1]
    e = (np.arange(ktot)[None, :] // SEL_BLK == np.arange(nselp)[:, None]).astype(np.float32)
    e = jnp.asarray(e, BF16)
    wb = wks.shape[1]
    any_spec = pl.BlockSpec(memory_space=pl.ANY)
    new = pl.BlockSpec((nq, LANES), lambda bi, pt_: (bi, 0))
    grid_spec = pltpu.PrefetchScalarGridSpec(
        num_scalar_prefetch=1, grid=(nb,),
        in_specs=[pl.BlockSpec((nq, 1024), lambda bi, pt_: (bi, 0)),
                  pl.BlockSpec((None, NSA_G * nq, nselp), lambda bi, pt_: (bi, 0, 0)),
                  new, new, new, new,
                  pl.BlockSpec((None, wb, LANES), lambda bi, pt_: (bi, 0, 0)),
                  pl.BlockSpec((None, wb, LANES), lambda bi, pt_: (bi, 0, 0)),
                  pl.BlockSpec(e.shape, lambda bi, pt_: (0, 0)), any_spec, any_spec],
        out_specs=[pl.BlockSpec((None, 8 * nq, LANES), lambda bi, pt_: (bi, 0, 0)),
                   pl.BlockSpec((None, 8 * nq, LANES), lambda bi, pt_: (bi, 0, 0))],
        scratch_shapes=[pltpu.VMEM((2, LANES, ktot), F32), pltpu.VMEM((2, LANES, ktot), F32), pltpu.SemaphoreType.DMA((2, 2))])
    return pl.pallas_call(
        functools.partial(_sselwin_body, layer=layer, n_pages=n_pages, past=past, nq=nq), name="sselwin",
        grid_spec=grid_spec,
        out_shape=[jax.ShapeDtypeStruct((nb, 8 * nq, LANES), F32)] * 2,
        compiler_params=_cparams(("arbitrary",)),
    )(pt, qr, sel, skn, svn, wkn, wvn, wks, wvs, e, cache_k, cache_v)


def _sdiff_body(pt_ref, dq_ref, dkn_ref, dvn_ref, lq1_ref, lk1_ref, lq2_ref, lk2_ref, sg_ref, dk_hbm, dv_hbm,
                o_ref, kbuf, vbuf, sem, *, layer, n_pages, past, nq, lam_init):
    ktot = past + PAGE
    rows = 8 * nq
    slot = _gather_pages(pt_ref, n_pages, [(dk_hbm, kbuf), (dv_hbm, vbuf)], sem, layer)
    pad = jnp.zeros((PAGE - nq, 256), F32)
    kbuf[slot, :, pl.ds(past, PAGE)] = jnp.concatenate([dkn_ref[...], pad], axis=0).T
    vbuf[slot, :, pl.ds(past, PAGE)] = jnp.concatenate([dvn_ref[...], pad], axis=0).T
    chunks = _key_chunks(n_pages + 1)
    q = dq_ref[...]
    zero = jnp.zeros((nq, LANES), BF16)
    blocks = []
    for a in range(8):
        chunk = q[:, LANES * a:LANES * (a + 1)]
        blocks.append(jnp.concatenate([chunk, zero] if a < 4 else [zero, chunk], axis=1))
    qa = jnp.concatenate(blocks, axis=0)
    s = jnp.concatenate([_dot(qa, kbuf[slot, :, pl.ds(c0, cn)].astype(BF16)) for c0, cn in chunks], axis=1)
    kpos = lax.broadcasted_iota(jnp.int32, (rows, ktot), 1)
    qpos = past + (lax.broadcasted_iota(jnp.int32, (rows, ktot), 0) & (nq - 1))
    s = jnp.where(kpos <= qpos, s, NEG)
    p, l = _softmax_rows(s)
    pb = p.astype(BF16)
    o = _dot_nt(pb[:, 0:chunks[0][1]], vbuf[slot, :, pl.ds(0, chunks[0][1])].astype(BF16))
    for c0, cn in chunks[1:]:
        o = o + _dot_nt(pb[:, c0:c0 + cn], vbuf[slot, :, pl.ds(c0, cn)].astype(BF16))
    o = o / l
    lam = (jnp.exp(jnp.sum(lq1_ref[...] * lk1_ref[...], keepdims=True))
           - jnp.exp(jnp.sum(lq2_ref[...] * lk2_ref[...], keepdims=True)) + lam_init)
    lane = lax.broadcasted_iota(jnp.int32, (nq, 256), 1)
    out = jnp.zeros((nq, 256), F32)
    for h in range(DIFF_H):
        d = o[2 * nq * h:2 * nq * h + nq] - lam * o[2 * nq * h + nq:2 * nq * (h + 1)]
        inh = (lane >> 6) == h
        ms = jnp.sum(jnp.where(inh, d * d, 0.0), axis=-1, keepdims=True) * (1.0 / DIFF_DV)
        out = out + jnp.where(inh, d * lax.rsqrt(ms + LN_EPS), 0.0)
    o_ref[...] = out * sg_ref[...] * (1.0 - lam_init)


def _sdiff(pt, dq, dkn, dvn, lams, sg_row, cache_k, cache_v, layer, nb, nq, past, lam_init):
    n_pages = past // PAGE
    ktot = past + PAGE
    any_spec = pl.BlockSpec(memory_space=pl.ANY)
    new = pl.BlockSpec((nq, 256), lambda bi, pt_: (bi, 0))
    small = [pl.BlockSpec(a.shape, lambda bi, pt_: (0, 0)) for a in (*lams, sg_row)]
    grid_spec = pltpu.PrefetchScalarGridSpec(
        num_scalar_prefetch=1, grid=(nb,),
        in_specs=[pl.BlockSpec((nq, 1024), lambda bi, pt_: (bi, 0)), new, new] + small + [any_spec, any_spec],
        out_specs=pl.BlockSpec((nq, 256), lambda bi, pt_: (bi, 0)),
        scratch_shapes=[pltpu.VMEM((2, 256, ktot), F32), pltpu.VMEM((2, 256, ktot), F32), pltpu.SemaphoreType.DMA((2, 2))])
    return pl.pallas_call(
        functools.partial(_sdiff_body, layer=layer, n_pages=n_pages, past=past, nq=nq, lam_init=lam_init), name="sdiff",
        grid_spec=grid_spec,
        out_shape=jax.ShapeDtypeStruct((nb * nq, 256), F32),
        compiler_params=_cparams(("arbitrary",)),
    )(pt, dq, dkn, dvn, *lams, sg_row, cache_k, cache_v)


def _prep_w_in(w):
    pts = np.cumsum([256, 256, 512, 128, 128, 128, 128, 128, 128, 24, 256, 256, 256])[:-1].tolist()
    ca, cg, nq, ck, cv, sk, sv, wk, wv, gt, dq, dk, dv = jnp.split(w, pts, axis=1)
    gtp = jnp.pad(gt, ((0, 0), (0, LANES - gt.shape[1])))
    return jnp.concatenate([ca, cg, nq * (NSA_DH ** -0.5 * LOG2E), ck, cv, sk, sv, wk, wv, gtp, dq, dk, dv], axis=1).astype(BF16)


def _rope_tables(pos):
    out = []
    lane = np.arange(LANES)
    for half in (32, 16):
        inv = ROPE_THETA ** (-jnp.arange(half, dtype=F32) / half)
        ang = pos.astype(F32)[:, None] * inv[None, :]
        idx = lane % half
        sign = jnp.asarray(np.where(lane % (2 * half) < half, -1.0, 1.0), F32)
        out += [jnp.cos(ang)[:, idx], jnp.sin(ang)[:, idx] * sign[None, :]]
    return out


def _prep_cmp(pe, w1, b1, w2):
    def halves(x):
        res = []
        for part in (x[:16 * NSA_DH], x[16 * NSA_DH:]):
            p4 = part.reshape(CMP_STRIDE, 1, NSA_DH, 1, -1)
            eye = jnp.eye(NSA_G, dtype=F32)[None, :, None, :, None]
            res.append((p4 * eye).reshape(CMP_STRIDE * NSA_G * NSA_DH, NSA_G * part.shape[-1]))
        return res
    wa, wb = halves(w1)
    pea = jnp.tile(pe[:16, None, :], (1, NSA_G, 1)).reshape(1, -1)
    peb = jnp.tile(pe[16:, None, :], (1, NSA_G, 1)).reshape(1, -1)
    w2bd = (w2[None, :, None, :] * jnp.eye(NSA_G, dtype=F32)[:, None, :, None]).reshape(NSA_G * CMP_HID, NSA_G * NSA_DH)
    return pea, peb, wa.astype(BF16), wb.astype(BF16), jnp.tile(b1, NSA_G)[None, :], w2bd.astype(BF16)


def _gate_expand():
    e = np.zeros((LANES, 3 * 512), np.float32)
    for h in range(NSA_H):
        for br in range(3):
            e[3 * h + br, 512 * br + 64 * h:512 * br + 64 * (h + 1)] = 1.0
    return jnp.asarray(e, BF16)


def _rows_to_tokens(o, nb, nq):
    o6 = o.reshape(nb, NSA_G, NSA_HPG, nq, NSA_G, NSA_DH)
    pick = jnp.stack([o6[:, g, :, :, g, :] for g in range(NSA_G)], axis=1)
    return pick.transpose(0, 3, 1, 2, 4).reshape(nb * nq, NSA_H * NSA_DH)


def _prompt_layer(x, lw, tabs, nb, t, lam_init, alpha):
    n = nb * t
    (u, qu, qr, gt, dq, ck, cv, skb, wkb, dkb, svb, wvb, dvb,
     ckt, cvt, skt, svt, wkt, wvt, dkt, dvt) = _inproj(x, lw["w_in"], tabs, nb, t, 256, True)
    ext = jnp.pad(u.reshape(nb, t, 256), ((0, 0), (CONV_PAD, 0), (0, 0)))
    yc = _conv(ext, lw["dw_w"], lw["dw_b"], lw["cln_g"], lw["cln_b"], t, 256).reshape(n, 256)
    ocmp, sel = _pcmp(qu, ck, cv, lw["cmp"], nb, t)
    osel = _flash("sel", qr, skb, svb, nb, t, extra=(sel,))
    owin = _win(qr, wkb, wvb, nb, t)
    odiff = _flash("diff", dq, dkb, dvb, nb, t, extra=lw["lams"] + (lw["sg_col"],), lam_init=lam_init)
    x1 = _outproj(x, yc, ocmp, osel, owin, gt, odiff, lw["gate_e"], lw["w_out"], lw["ln1_g"], lw["ln1_b"], alpha, 256)
    x2 = _ffn(x1, lw["w1"], lw["w3"], lw["w2"], lw["ln2_g"], lw["ln2_b"], alpha, 512, 2)
    nk = min(WINDOW, t)
    rows_major = lambda a: a.reshape(nb, a.shape[1] // 64, 64, a.shape[2]).transpose(0, 3, 1, 2)
    news = (rows_major(ckt), rows_major(cvt), rows_major(skt), rows_major(svt), rows_major(dkt), rows_major(dvt),
            rows_major(wkt[:, :, t - nk:]), rows_major(wvt[:, :, t - nk:]),
            u.reshape(nb, t, 256)[:, t - (CONV_W - 1):])
    return x2, news


def _sample_layer(x, lw, tabs, caches, states, pt, layer, nb, nq, past, lam_init, alpha):
    n = nb * nq
    (u, qu, qr, gt, dq, ck, cv, sk, sv, wk, wv, dk, dv) = _inproj(x, lw["w_in"], tabs, nb, nq, n, False)
    c_cmp_k, c_cmp_v, c_sel_k, c_sel_v, c_diff_k, c_diff_v = caches
    st_wk, st_wv, st_conv = states
    ext = jnp.concatenate([jnp.zeros((nb, CONV_PAD - (CONV_W - 1), 256), F32), st_conv, u.reshape(nb, nq, 256)], axis=1)
    yc = _conv(ext, lw["dw_w"], lw["dw_b"], lw["cln_g"], lw["cln_b"], nq, nq).reshape(n, 256)
    ocmp, score = _scmp(pt, qu, c_cmp_k, c_cmp_v, lw["cmp"], layer, nb, nq, past)
    sel = _stopk(score, past, nq)
    wb = st_wk.shape[1]
    osel, owin = _sselwin(pt, qr, sel, sk, sv, wk, wv, st_wk.reshape(nb, wb, LANES), st_wv.reshape(nb, wb, LANES),
                          c_sel_k, c_sel_v, layer, nb, nq, past)
    odiff = _sdiff(pt, dq, dk, dv, lw["lams"], lw["sg_row"], c_diff_k, c_diff_v, layer, nb, nq, past, lam_init)
    x1 = _outproj(x, yc, _rows_to_tokens(ocmp, nb, nq), _rows_to_tokens(osel, nb, nq), _rows_to_tokens(owin, nb, nq),
                  gt, odiff, lw["gate_e"], lw["w_out"], lw["ln1_g"], lw["ln1_b"], alpha, n)
    x2 = _ffn(x1, lw["w1"], lw["w3"], lw["w2"], lw["ln2_g"], lw["ln2_b"], alpha, n, 2)
    new_wk = jnp.concatenate([st_wk, wk.reshape(nb, nq, NSA_G, NSA_DH)], axis=1)[:, -wb:]
    new_wv = jnp.concatenate([st_wv, wv.reshape(nb, nq, NSA_G, NSA_DH)], axis=1)[:, -wb:]
    new_conv = jnp.concatenate([st_conv, u.reshape(nb, nq, 256)], axis=1)[:, -(CONV_W - 1):]
    news = (ck.reshape(nb, nq, NSA_G, NSA_DH), cv.reshape(nb, nq, NSA_G, NSA_DH),
            sk.reshape(nb, nq, NSA_G, NSA_DH), sv.reshape(nb, nq, NSA_G, NSA_DH),
            dk.reshape(nb, nq, DIFF_H, 2 * DIFF_DQK), dv.reshape(nb, nq, DIFF_H, DIFF_DV),
            new_wk, new_wv, new_conv)
    return x2, news


def kernel(x_prompt, x_sample, cache_nsa_cmp_k, cache_nsa_cmp_v, cache_nsa_sel_k, cache_nsa_sel_v, cache_diff_k, cache_diff_v, state_nsa_win_k, state_nsa_win_v, state_conv, page_table, w_in, conv_dw_w, conv_dw_b, conv_ln_g, conv_ln_b, cmp_pe_k, cmp_w1_k, cmp_b1_k, cmp_w2_k, cmp_pe_v, cmp_w1_v, cmp_b1_v, cmp_w2_v, diff_lq1, diff_lk1, diff_lq2, diff_lk2, diff_subln_g, w_out, ln1_g, ln1_b, ln2_g, ln2_b, ffn_w1, ffn_w3, ffn_w2):
    nb, t, d = x_prompt.shape
    sb, nq, _ = x_sample.shape
    depth = w_in.shape[0]
    n_pool = cache_nsa_cmp_k.shape[1]
    past = page_table.shape[1] * PAGE
    alpha = (2 * depth) ** 0.25
    tabs_p = _rope_tables(jnp.arange(t, dtype=jnp.int32))
    tabs_s = _rope_tables(jnp.tile(past + jnp.arange(nq, dtype=jnp.int32), sb))
    as_pages = lambda c: c.transpose(0, 1, 3, 4, 2).reshape(depth, n_pool, c.shape[3] * c.shape[4], PAGE)
    caches = tuple(as_pages(c) for c in (cache_nsa_cmp_k, cache_nsa_cmp_v, cache_nsa_sel_k, cache_nsa_sel_v,
                                         cache_diff_k, cache_diff_v))
    gate_e = _gate_expand()
    xp = x_prompt.reshape(nb * t, d)
    xs = x_sample.reshape(sb * nq, d)
    outs_p, outs_s = [], []
    for l in range(depth):
        ck = _prep_cmp(cmp_pe_k[l], cmp_w1_k[l], cmp_b1_k[l], cmp_w2_k[l])
        cv = _prep_cmp(cmp_pe_v[l], cmp_w1_v[l], cmp_b1_v[l], cmp_w2_v[l])
        names = ("pa", "pb", "wa", "wb", "b1", "w2")
        cmpw = {n_ + "k": a for n_, a in zip(names, ck)}
        cmpw.update({n_ + "v": a for n_, a in zip(names, cv)})
        lw = dict(
            w_in=_prep_w_in(w_in[l]), dw_w=conv_dw_w[l], dw_b=conv_dw_b[l][None], cln_g=conv_ln_g[l][None],
            cln_b=conv_ln_b[l][None], cmp=cmpw,
            lams=(diff_lq1[l][None], diff_lk1[l][None], diff_lq2[l][None], diff_lk2[l][None]),
            sg_col=diff_subln_g[l][:, None], sg_row=jnp.tile(diff_subln_g[l], DIFF_H)[None],
            gate_e=gate_e, w_out=w_out[l].astype(BF16), ln1_g=ln1_g[l][None], ln1_b=ln1_b[l][None],
            ln2_g=ln2_g[l][None], ln2_b=ln2_b[l][None],
            w1=ffn_w1[l].astype(BF16), w3=ffn_w3[l].astype(BF16), w2=ffn_w2[l].astype(BF16))
        lam_init = 0.8 - 0.6 * math.exp(-0.3 * l)
        xp, new_p = _prompt_layer(xp, lw, tabs_p, nb, t, lam_init, alpha)
        xs, new_s = _sample_layer(xs, lw, tabs_s, caches, (state_nsa_win_k[l], state_nsa_win_v[l], state_conv[l]),
                                  page_table, l, sb, nq, past, lam_init, alpha)
        outs_p.append(new_p)
        outs_s.append(new_s)
    stk_p = [jnp.stack([o[i] for o in outs_p]) for i in range(9)]
    stk_s = [jnp.stack([o[i] for o in outs_s]) for i in range(9)]
    return (xp.reshape(nb, t, d), xs.reshape(sb, nq, d), *stk_p, *stk_s)
```

```python
import functools
import math

import jax
import jax.numpy as jnp
import numpy as np
from jax import lax
from jax.experimental import pallas as pl
from jax.experimental.pallas import tpu as pltpu

F32 = jnp.float32
BF16 = jnp.bfloat16

CONV_W = 31
NSA_H = 8
NSA_G = 2
NSA_HPG = NSA_H // NSA_G
NSA_DH = 64
CMP_STRIDE = 16
CMP_LEN = 32
CMP_HID = 128
SEL_BLK = 64
SEL_TOPN = 16
WINDOW = 512
FORCE_SCORE = 1e9
DIFF_H = 4
DIFF_DV = 64
DIFF_DQK = 32
ROPE_THETA = 10000.0
LN_EPS = 1e-5
PAGE = 128

LANES = 128
TQ = 128
TK = 256
LOG2E = 1.4426950408889634
NEG = -1e30
VMEM_LIMIT = 52 * 1024 * 1024

C_CA, C_CG, C_Q, C_KV, C_GT, C_DQ, C_DK, C_DV, C_END = 0, 256, 512, 1024, 1792, 1920, 2176, 2432, 2688


def _cparams(sem):
    return pltpu.CompilerParams(dimension_semantics=sem, vmem_limit_bytes=VMEM_LIMIT)


def _sigmoid(x):
    return 1.0 / (1.0 + jnp.exp(-x))


def _ln_rows(x, g, b):
    mu = jnp.mean(x, axis=-1, keepdims=True)
    xc = x - mu
    var = jnp.mean(xc * xc, axis=-1, keepdims=True)
    return xc * lax.rsqrt(var + LN_EPS) * g + b


def _dot(a, b):
    return jnp.dot(a, b, preferred_element_type=F32)


def _dot_nt(a, b):
    return lax.dot_general(a, b, (((1,), (1,)), ((), ())), preferred_element_type=F32)


def _split3_dot(x, m):
    hi = x.astype(BF16)
    r1 = x - hi.astype(F32)
    mid = r1.astype(BF16)
    lo = (r1 - mid.astype(F32)).astype(BF16)
    return _dot(hi, m) + _dot(mid, m) + _dot(lo, m)


def _rope(x, cos, sin_signed, half):
    lane = lax.broadcasted_iota(jnp.int32, x.shape, 1)
    first = (lane & (2 * half - 1)) < half
    rot = jnp.where(first, pltpu.roll(x, LANES - half, 1), pltpu.roll(x, half, 1))
    return x * cos + rot * sin_signed


def _inproj_body(x_ref, w_ref, c64_ref, s64_ref, c32_ref, s32_ref,
                 u_ref, qu_ref, qr_ref, gt_ref, dq_ref, ck_ref, cv_ref, *refs, dq_scale, transposed):
    xb = x_ref[...].astype(BF16)

    def mm(lo, hi):
        return _dot(xb, w_ref[:, lo:hi])

    c64, s64, c32, s32 = c64_ref[...], s64_ref[...], c32_ref[...], s32_ref[...]
    z = mm(C_CA, C_Q)
    u_ref[...] = z[:, :256] * _sigmoid(z[:, 256:])
    lane = lax.broadcasted_iota(jnp.int32, (x_ref.shape[0], LANES), 1)
    for j in range(4):
        zq = mm(C_Q + LANES * j, C_Q + LANES * (j + 1))
        g = j // 2
        keep = (lane < 64) if g == 0 else (lane >= 64)
        for src, dst_ref in ((zq, qu_ref), (_rope(zq, c64, s64, 32), qr_ref)):
            for e in range(2):
                v = src if e == g else pltpu.roll(src, 64, 1)
                dst_ref[:, LANES * (2 * j + e):LANES * (2 * j + e + 1)] = jnp.where(keep, v, 0.0).astype(BF16)
    z = mm(C_KV, C_GT)
    ck = z[:, 0:128]
    cv = z[:, 128:256]
    ck_ref[...] = ck
    cv_ref[...] = cv
    sk = _rope(z[:, 256:384], c64, s64, 32)
    sv = z[:, 384:512]
    wk = _rope(z[:, 512:640], c64, s64, 32)
    wv = z[:, 640:768]
    gt_ref[...] = _sigmoid(mm(C_GT, C_DQ))
    for c in range(2):
        zq = _rope(mm(C_DQ + LANES * c, C_DQ + LANES * (c + 1)), c32, s32, 16) * dq_scale
        for a in range(4):
            dq_ref[:, LANES * (4 * c + a):LANES * (4 * c + a + 1)] = jnp.where((lane >> 5) == a, zq, 0.0).astype(BF16)
    z = mm(C_DK, C_DV)
    dk = [_rope(z[:, :128], c32, s32, 16), _rope(z[:, 128:], c32, s32, 16)]
    z = mm(C_DV, C_END)
    dv = [z[:, :128], z[:, 128:]]
    if not transposed:
        sk_ref, sv_ref, wk_ref, wv_ref, dk_ref, dv_ref = refs
        sk_ref[...] = sk
        sv_ref[...] = sv
        wk_ref[...] = wk
        wv_ref[...] = wv
        for c in range(2):
            dk_ref[:, LANES * c:LANES * (c + 1)] = dk[c]
            dv_ref[:, LANES * c:LANES * (c + 1)] = dv[c]
        return
    (skb_ref, wkb_ref, dkb_ref, svb_ref, wvb_ref, dvb_ref,
     ckt_ref, cvt_ref, skt_ref, svt_ref, wkt_ref, wvt_ref, dkt_ref, dvt_ref) = refs
    skb_ref[...] = sk.astype(BF16)
    wkb_ref[...] = wk.astype(BF16)
    ckt_ref[...] = ck.T
    cvt_ref[...] = cv.T
    skt_ref[...] = sk.T
    wkt_ref[...] = wk.T
    svt = sv.T
    svt_ref[...] = svt
    svb_ref[...] = svt.astype(BF16)
    wvt = wv.T
    wvt_ref[...] = wvt
    wvb_ref[...] = wvt.astype(BF16)
    for c in range(2):
        rows = slice(LANES * c, LANES * (c + 1))
        dkb_ref[:, rows] = dk[c].astype(BF16)
        dkt_ref[rows, :] = dk[c].T
        dvt = dv[c].T
        dvt_ref[rows, :] = dvt
        dvb_ref[rows, :] = dvt.astype(BF16)


def _inproj(x, w, tabs, nb, t, tm, transposed):
    n, d = x.shape
    nt = t // tm if transposed else 1
    grid = (n // tm,)
    row = lambda c: pl.BlockSpec((tm, c), lambda i: (i, 0))
    tab = pl.BlockSpec((tm, LANES), (lambda i: (i % nt, 0)) if transposed else (lambda i: (i, 0)))
    in_specs = [row(d), pl.BlockSpec((d, C_END), lambda i: (0, 0)), tab, tab, tab, tab]
    shapes = [(256, F32), (1024, BF16), (1024, BF16), (128, F32), (1024, BF16), (128, F32), (128, F32)]
    if transposed:
        shapes += [(128, BF16), (128, BF16), (256, BF16)]
    else:
        shapes += [(128, F32)] * 4 + [(256, F32)] * 2
    out_shape = [jax.ShapeDtypeStruct((n, c), dt) for c, dt in shapes]
    out_specs = [row(c) for c, _ in shapes]
    if transposed:
        for c, dt in [(128, BF16), (128, BF16), (256, BF16)] + [(128, F32)] * 6 + [(256, F32)] * 2:
            out_shape.append(jax.ShapeDtypeStruct((nb, c, t), dt))
            out_specs.append(pl.BlockSpec((None, c, tm), lambda i: (i // nt, 0, i % nt)))
    return pl.pallas_call(
        functools.partial(_inproj_body, dq_scale=DIFF_DQK ** -0.5 * LOG2E, transposed=transposed), name="inproj",
        grid=grid, in_specs=in_specs, out_specs=out_specs, out_shape=out_shape,
        compiler_params=_cparams(("parallel",)),
    )(x, w, *tabs)


CONV_PAD = 32


def _conv_body(ext_ref, w_ref, b_ref, g_ref, beta_ref, y_ref, acc_ref, sh_ref, *, tt, rs):
    t0 = pl.multiple_of(pl.program_id(1) * tt, 8)
    off = CONV_PAD - (CONV_W - 1)
    for c in range(2):
        cs = slice(LANES * c, LANES * (c + 1))
        for r in range(tt // rs):
            win = ext_ref[pl.ds(t0 + rs * r, rs + CONV_PAD), cs]
            acc = jnp.zeros((rs, LANES), F32)
            for r8 in range(8):
                taps = range(r8, CONV_W, 8)
                rows = 8 * (len(taps) - 1) + rs
                sh_ref[0:rows, :] = win[off + r8:off + r8 + rows, :]
                for a, k in enumerate(taps):
                    acc = acc + sh_ref[8 * a:8 * a + rs, :] * w_ref[k:k + 1, cs]
            acc_ref[rs * r:rs * (r + 1), cs] = acc
    y = _ln_rows(acc_ref[...] + b_ref[...], g_ref[...], beta_ref[...])
    y_ref[...] = y * _sigmoid(y)


def _conv(ext, w, b, g, beta, t, tt):
    nb, le, c = ext.shape
    rs = min(tt, 64)
    vec = pl.BlockSpec((1, c), lambda bi, ti: (0, 0))
    return pl.pallas_call(
        functools.partial(_conv_body, tt=tt, rs=rs), name="conv",
        grid=(nb, t // tt),
        in_specs=[pl.BlockSpec((None, le, c), lambda bi, ti: (bi, 0, 0)),
                  pl.BlockSpec((CONV_W, c), lambda bi, ti: (0, 0)), vec, vec, vec],
        out_specs=pl.BlockSpec((None, tt, c), lambda bi, ti: (bi, ti, 0)),
        out_shape=jax.ShapeDtypeStruct((nb, t, c), F32),
        scratch_shapes=[pltpu.VMEM((tt, c), F32), pltpu.VMEM((rs + CONV_PAD, LANES), F32)],
        compiler_params=_cparams(("parallel", "parallel")),
    )(ext, w, b, g, beta)


def _compress(src_ref, n16, pea, peb, wa_ref, wb_ref, b1, w2_ref):
    x = jnp.concatenate([src_ref[pl.ds(p, n16, stride=CMP_STRIDE), :] for p in range(CMP_STRIDE)], axis=1)
    a = _dot((x + pea).astype(BF16), wa_ref[...])
    bm = _dot((x + peb).astype(BF16), wb_ref[...])
    h = a + pltpu.roll(bm, n16 - 1, 0) + b1
    gl = 0.5 * h * (1.0 + jnp.tanh(0.7978845608028654 * (h + 0.044715 * (h * h * h))))
    return _dot(gl.astype(BF16), w2_ref[...])


def _top_rows(score, ridx, k):
    nv = score.shape[0] // 8
    tiles = [score[8 * v:8 * (v + 1), :] for v in range(nv)]
    cnts = [jnp.zeros(t.shape, F32) for t in tiles]
    for jp in range(score.shape[0]):
        row = score[jp:jp + 1, :]
        for v in range(nv):
            if 8 * v > jp:
                beats = row >= tiles[v]
            elif 8 * v + 7 < jp:
                beats = row > tiles[v]
            else:
                beats = (row > tiles[v]) | ((row == tiles[v]) & (ridx[8 * v:8 * (v + 1), :] > jp))
            cnts[v] = cnts[v] + jnp.where(beats, 1.0, 0.0)
    return jnp.concatenate(cnts, axis=0) < k


def _pcmp_body(qu_ref, ck_ref, cv_ref, pak_ref, pbk_ref, wak_ref, wbk_ref, b1k_ref, w2k_ref,
               pav_ref, pbv_ref, wav_ref, wbv_ref, b1v_ref, w2v_ref,
               o_ref, sel_ref, kcc_ref, vcct_ref, imp_ref, *, n16, nsel):
    @pl.when(pl.program_id(1) == 0)
    def _():
        kcc = _compress(ck_ref, n16, pak_ref[...], pbk_ref[...], wak_ref, wbk_ref, b1k_ref[...], w2k_ref)
        kcc_ref[...] = kcc.astype(BF16)
        vcc = _compress(cv_ref, n16, pav_ref[...], pbv_ref[...], wav_ref, wbv_ref, b1v_ref[...], w2v_ref)
        vcct_ref[...] = vcc.T.astype(BF16)

    for u in range(PCMP_TILES):
        _pcmp_tile(pl.program_id(1) * PCMP_TILES + u, qu_ref.at[TQ * u:TQ * (u + 1), :], kcc_ref[...], vcct_ref[...],
                   imp_ref.at[u], o_ref.at[TQ * u:TQ * (u + 1), :], sel_ref.at[u], n16, nsel)


PCMP_TILES = 8


def _pcmp_tile(ti, qu_ref, kcc, vcct, imp_ref, o_ref, sel_ref, n16, nsel):
    q = qu_ref[...]
    nq = 4 * TQ
    cidx = lax.broadcasted_iota(jnp.int32, (n16, nq), 0)
    qpos = ti * TQ + (lax.broadcasted_iota(jnp.int32, (n16, nq), 1) & (TQ - 1))
    vis = (CMP_STRIDE * cidx + CMP_LEN - 1 <= qpos) & (cidx < n16 - 1)
    jidx = lax.broadcasted_iota(jnp.int32, (nsel, TQ), 0)
    qp1 = ti * TQ + lax.broadcasted_iota(jnp.int32, (nsel, TQ), 1)
    cur = qp1 >> 6
    forced = (jidx == 0) | (jidx == cur) | (jidx == cur - 1)
    for g in range(NSA_G):
        qs = jnp.concatenate([q[:, LANES * (4 * g + a):LANES * (4 * g + a + 1)] for a in range(4)], axis=0)
        st = _dot_nt(kcc, qs)
        st = jnp.where(vis, st, NEG)
        m = jnp.max(st, axis=0, keepdims=True)
        p = jnp.exp2(st - m)
        l = jnp.sum(p, axis=0, keepdims=True)
        inv = jnp.where(m > 0.5 * NEG, 1.0 / jnp.maximum(l, 1e-30), 0.0)
        p = p * inv
        ot = _dot(vcct[64 * g:64 * (g + 1), :], p.astype(BF16))
        for a2 in range(2):
            blk = jnp.concatenate([ot[:, TQ * (2 * a2 + e):TQ * (2 * a2 + e + 1)] for e in range(2)], axis=0)
            o_ref[:, 256 * g + LANES * a2:256 * g + LANES * (a2 + 1)] = blk.T
        imp_ref[g, 0:8, :] = jnp.zeros((8, TQ), F32)
        imp_ref[g, 8:8 + n16, :] = p[:, 0:TQ] + p[:, TQ:2 * TQ] + p[:, 2 * TQ:3 * TQ] + p[:, 3 * TQ:4 * TQ]
        r = SEL_BLK // CMP_STRIDE
        isel = imp_ref[g, pl.ds(7, nsel, stride=r), :]
        for e in range(1, r + 1):
            isel = isel + imp_ref[g, pl.ds(7 + e, nsel, stride=r), :]
        score = jnp.where(forced, FORCE_SCORE, isel)
        score = jnp.where(jidx <= cur, score, -jnp.inf)
        sel = _top_rows(score, jidx, min(SEL_TOPN, nsel)) & (jidx <= cur)
        sel_ref[g] = jnp.where(sel, 0.0, NEG)


def _pcmp(qu, ck, cv, cw, nb, t):
    n16 = t // CMP_STRIDE
    nsel = t // SEL_BLK
    nt = t // (TQ * PCMP_TILES)
    const = lambda a: pl.BlockSpec(a.shape, lambda bi, ti: (0,) * a.ndim)
    wlist = [cw[k] for k in ("pak", "pbk", "wak", "wbk", "b1k", "w2k", "pav", "pbv", "wav", "wbv", "b1v", "w2v")]
    return pl.pallas_call(
        functools.partial(_pcmp_body, n16=n16, nsel=nsel), name="pcmp",
        grid=(nb, nt),
        in_specs=[pl.BlockSpec((TQ * PCMP_TILES, 1024), lambda bi, ti: (bi * nt + ti, 0)),
                  pl.BlockSpec((None, t, LANES), lambda bi, ti: (bi, 0, 0)),
                  pl.BlockSpec((None, t, LANES), lambda bi, ti: (bi, 0, 0))] + [const(a) for a in wlist],
        out_specs=[pl.BlockSpec((TQ * PCMP_TILES, 512), lambda bi, ti: (bi * nt + ti, 0)),
                   pl.BlockSpec((None, PCMP_TILES, NSA_G, nsel, TQ), lambda bi, ti: (bi, ti, 0, 0, 0))],
        out_shape=[jax.ShapeDtypeStruct((nb * t, 512), F32),
                   jax.ShapeDtypeStruct((nb, t // TQ, NSA_G, nsel, TQ), F32)],
        scratch_shapes=[pltpu.VMEM((n16, LANES), BF16), pltpu.VMEM((LANES, n16), BF16),
                        pltpu.VMEM((PCMP_TILES, NSA_G, n16 + 8, TQ), F32)],
        compiler_params=_cparams(("parallel", "arbitrary")),
    )(qu, ck.reshape(nb, t, LANES), cv.reshape(nb, t, LANES), *wlist)


def _stack_queries(q, s):
    return jnp.concatenate([q[:, LANES * (4 * s + a):LANES * (4 * s + a + 1)] for a in range(4)], axis=0)


def _flash_out(z, s, o_ref):
    tq = z.shape[1] // 4
    for a2 in range(2):
        blk = jnp.concatenate([z[:, tq * (2 * a2 + e):tq * (2 * a2 + e + 1)] for e in range(2)], axis=0)
        o_ref[:, 256 * s + LANES * a2:256 * s + LANES * (a2 + 1)] = blk.T


def _flash_body(*refs, mode, lam_init):
    if mode == "sel":
        q_ref, k_ref, vt_ref, selb_ref, o_ref, acc_ref, m_ref, l_ref, al_ref, s_ref, p_ref = refs
    else:
        (q_ref, k_ref, vt_ref, lq1_ref, lk1_ref, lq2_ref, lk2_ref, sg_ref, o_ref,
         acc_ref, m_ref, l_ref, al_ref, s_ref, p_ref) = refs
    ti = pl.program_id(1)
    tq = TQ * FLASH_TILES
    nq = 4 * tq
    q = q_ref[...]
    qs = [_stack_queries(q, s) for s in range(2)]
    koffs = [LANES * s if mode == "diff" else 0 for s in range(2)]
    m_ref[...] = jnp.full(m_ref.shape, NEG, F32)
    l_ref[...] = jnp.zeros(l_ref.shape, F32)
    acc_ref[...] = jnp.zeros(acc_ref.shape, F32)
    al_ref[...] = jnp.ones(al_ref.shape, F32)
    p_ref[...] = jnp.zeros(p_ref.shape, BF16)

    def scores(j, slot):
        k0 = pl.multiple_of(j * TK, TK)
        for s in range(2):
            s_ref[slot, s] = _dot_nt(k_ref[pl.ds(k0, TK), koffs[s]:koffs[s] + LANES], qs[s])

    def accumulate(j):
        k0 = pl.multiple_of(j * TK, TK)
        for s in range(2):
            if mode == "sel":
                acc_ref[s] = al_ref[s] * acc_ref[s] + _dot(vt_ref[64 * s:64 * (s + 1), pl.ds(k0, TK)], p_ref[s])
            else:
                for hh in range(2):
                    cols = slice(2 * tq * hh, 2 * tq * (hh + 1))
                    acc_ref[s, :, cols] = (al_ref[s, :, cols] * acc_ref[s, :, cols]
                                           + _dot(vt_ref[koffs[s] + 64 * hh:koffs[s] + 64 * (hh + 1), pl.ds(k0, TK)], p_ref[s, :, cols]))

    def softmax(j, slot, causal):
        for s in range(2):
            st = s_ref[slot, s]
            if mode == "sel":
                rows = [jnp.broadcast_to(jnp.concatenate(
                    [selb_ref[u, s, pl.ds((TK // SEL_BLK) * j + r, 1), :] for u in range(FLASH_TILES)], axis=1), (SEL_BLK, tq))
                    for r in range(TK // SEL_BLK)]
                bias = jnp.concatenate(rows, axis=0)
                st = st + jnp.concatenate([bias, bias, bias, bias], axis=1)
            if causal:
                krow = lax.broadcasted_iota(jnp.int32, (TK, nq), 0)
                qcol = lax.broadcasted_iota(jnp.int32, (TK, nq), 1) & (tq - 1)
                st = jnp.where(ti * tq + qcol >= j * TK + krow, st, NEG)
            m_old = m_ref[s]
            m_new = jnp.maximum(m_old, jnp.max(st, axis=0, keepdims=True))
            alpha = jnp.exp2(m_old - m_new)
            p = jnp.exp2(st - m_new)
            l_ref[s] = alpha * l_ref[s] + jnp.sum(p, axis=0, keepdims=True)
            p_ref[s] = p.astype(BF16)
            al_ref[s] = alpha
            m_ref[s] = m_new

    n_full = (ti * tq) // TK
    scores(0, 0)

    def body(j, c):
        slot = j & 1
        accumulate(jnp.maximum(j - 1, 0))
        softmax(j, slot, False)
        scores(j + 1, 1 - slot)
        return c

    lax.fori_loop(0, n_full, body, 0)
    accumulate(jnp.maximum(n_full - 1, 0))
    softmax(n_full, n_full & 1, True)
    accumulate(n_full)

    if mode == "diff":
        lam = (jnp.exp(jnp.sum(lq1_ref[...] * lk1_ref[...], keepdims=True))
               - jnp.exp(jnp.sum(lq2_ref[...] * lk2_ref[...], keepdims=True)) + lam_init)
    for s in range(2):
        z = acc_ref[s] * (1.0 / l_ref[s])
        if mode == "diff":
            halves = []
            for hh in range(2):
                d = z[:, tq * (2 * hh):tq * (2 * hh + 1)] - lam * z[:, tq * (2 * hh + 1):tq * (2 * hh + 2)]
                ms = jnp.mean(d * d, axis=0, keepdims=True)
                halves.append(d * lax.rsqrt(ms + LN_EPS) * sg_ref[...] * (1.0 - lam_init))
            o_ref[:, LANES * s:LANES * (s + 1)] = jnp.concatenate(halves, axis=0).T
        else:
            _flash_out(z, s, o_ref)


FLASH_TILES = 1


def _flash(mode, q, k, vt, nb, t, extra=(), lam_init=0.0):
    tq = TQ * FLASH_TILES
    assert t % tq == 0 and t % TK == 0 and (tq % TK == 0 or TK % tq == 0)
    nt = t // tq
    kw = k.shape[-1]
    ow = 256 if mode == "diff" else 512
    in_specs = [pl.BlockSpec((tq, 1024), lambda bi, ti: (bi * nt + ti, 0)),
                pl.BlockSpec((None, t, kw), lambda bi, ti: (bi, 0, 0)),
                pl.BlockSpec((None, kw, t), lambda bi, ti: (bi, 0, 0))]
    if mode == "sel":
        nsel = t // SEL_BLK
        in_specs.append(pl.BlockSpec((None, FLASH_TILES, NSA_G, nsel, TQ), lambda bi, ti: (bi, ti, 0, 0, 0)))
    else:
        in_specs += [pl.BlockSpec(a.shape, lambda bi, ti: (0, 0)) for a in extra]
    return pl.pallas_call(
        functools.partial(_flash_body, mode=mode, lam_init=lam_init), name="flash_" + mode,
        grid=(nb, nt), in_specs=in_specs,
        out_specs=pl.BlockSpec((tq, ow), lambda bi, ti: (bi * nt + ti, 0)),
        out_shape=jax.ShapeDtypeStruct((nb * t, ow), F32),
        scratch_shapes=[pltpu.VMEM((2, 64, 4 * tq), F32)] + [pltpu.VMEM((2, 1, 4 * tq), F32)] * 3
        + [pltpu.VMEM((2, 2, TK, 4 * tq), F32), pltpu.VMEM((2, TK, 4 * tq), BF16)],
        compiler_params=_cparams(("parallel", "parallel")),
    )(q, k.reshape(nb, t, kw), vt, *extra)


def _win_body(q_ref, k_ref, vt_ref, o_ref):
    nq = 4 * TQ
    span = WINDOW + TQ

    def tiles(interior):
        if interior:
            krow = lax.broadcasted_iota(jnp.int32, (TQ, nq), 0)
            qcol = lax.broadcasted_iota(jnp.int32, (TQ, nq), 1) & (TQ - 1)
        else:
            krow = lax.broadcasted_iota(jnp.int32, (span, nq), 0)
            qcol = lax.broadcasted_iota(jnp.int32, (span, nq), 1) & (TQ - 1)
        for u in range(WIN_TILES):
            ti = pl.program_id(1) * WIN_TILES + u
            k0 = pl.multiple_of(jnp.maximum(ti * TQ - WINDOW, 0), TQ)
            q = q_ref[TQ * u:TQ * (u + 1), :]
            kt = k_ref[pl.ds(k0, span), :]
            vt = vt_ref[:, pl.ds(k0, span)]
            for s in range(2):
                st = _dot_nt(kt, _stack_queries(q, s))
                if interior:
                    st = jnp.concatenate([jnp.where(krow > qcol, st[:TQ], NEG), st[TQ:WINDOW],
                                          jnp.where(krow <= qcol, st[WINDOW:], NEG)], axis=0)
                else:
                    dlt = ti * TQ + qcol - (k0 + krow)
                    st = jnp.where((dlt >= 0) & (dlt < WINDOW), st, NEG)
                m = jnp.max(st, axis=0, keepdims=True)
                p = jnp.exp2(st - m)
                l = jnp.sum(p, axis=0, keepdims=True)
                _flash_out(_dot(vt[64 * s:64 * (s + 1), :], p.astype(BF16)) * (1.0 / l), s, o_ref.at[TQ * u:TQ * (u + 1), :])

    first_interior = -(-WINDOW // (TQ * WIN_TILES))

    @pl.when(pl.program_id(1) >= first_interior)
    def _():
        tiles(True)

    @pl.when(pl.program_id(1) < first_interior)
    def _():
        tiles(False)


WIN_TILES = 8


def _win(q, k, vt, nb, t):
    nt = t // (TQ * WIN_TILES)
    assert t >= WINDOW + TQ and t % (TQ * WIN_TILES) == 0
    return pl.pallas_call(
        _win_body, name="flash_win", grid=(nb, nt),
        in_specs=[pl.BlockSpec((TQ * WIN_TILES, 1024), lambda bi, ti: (bi * nt + ti, 0)),
                  pl.BlockSpec((None, t, LANES), lambda bi, ti: (bi, 0, 0)),
                  pl.BlockSpec((None, LANES, t), lambda bi, ti: (bi, 0, 0))],
        out_specs=pl.BlockSpec((TQ * WIN_TILES, 512), lambda bi, ti: (bi * nt + ti, 0)),
        out_shape=jax.ShapeDtypeStruct((nb * t, 512), F32),
        compiler_params=_cparams(("parallel", "parallel")),
    )(q, k.reshape(nb, t, LANES), vt)


def _outproj_body(x_ref, yc_ref, oc_ref, os_ref, ow_ref, gt_ref, od_ref, e_ref, w_ref, g_ref, b_ref, o_ref, *, alpha):
    gt = gt_ref[...]
    hi = gt.astype(BF16)
    lo = (gt - hi.astype(F32)).astype(BF16)
    e = e_ref[...]
    gx = _dot(hi, e) + _dot(lo, e)
    onsa = gx[:, 0:512] * oc_ref[...] + gx[:, 512:1024] * os_ref[...] + gx[:, 1024:1536] * ow_ref[...]
    mix = (_dot(yc_ref[...].astype(BF16), w_ref[0:256, :]) + _dot(onsa.astype(BF16), w_ref[256:768, :])
           + _dot(od_ref[...].astype(BF16), w_ref[768:1024, :]))
    o_ref[...] = _ln_rows(alpha * x_ref[...] + mix, g_ref[...], b_ref[...])


def _outproj(x, yc, oc, osel, ow, gt, od, e, w, g, b, alpha, tm):
    n, d = x.shape
    row = lambda c: pl.BlockSpec((tm, c), lambda i: (i, 0))
    const = lambda a: pl.BlockSpec(a.shape, lambda i: (0, 0))
    return pl.pallas_call(
        functools.partial(_outproj_body, alpha=alpha), name="outproj",
        grid=(n // tm,),
        in_specs=[row(d), row(256), row(512), row(512), row(512), row(128), row(256), const(e), const(w), const(g), const(b)],
        out_specs=row(d), out_shape=jax.ShapeDtypeStruct((n, d), F32),
        compiler_params=_cparams(("parallel",)),
    )(x, yc, oc, osel, ow, gt, od, e, w, g, b)


def _ffn_body(x_ref, w1_ref, w3_ref, w2_ref, g_ref, b_ref, o_ref, acc_ref, *, alpha):
    f = pl.program_id(1)
    xb = x_ref[...].astype(BF16)
    h1 = _dot(xb, w1_ref[...])
    h = h1 * _sigmoid(h1) * _dot(xb, w3_ref[...])
    part = _dot(h.astype(BF16), w2_ref[...])

    @pl.when(f == 0)
    def _():
        acc_ref[...] = part

    @pl.when(f != 0)
    def _():
        acc_ref[...] += part

    @pl.when(f == pl.num_programs(1) - 1)
    def _():
        o_ref[...] = _ln_rows(alpha * x_ref[...] + acc_ref[...], g_ref[...], b_ref[...])


def _ffn(x, w1, w3, w2, g, b, alpha, tm, nf):
    n, d = x.shape
    dff = w1.shape[1]
    tf = dff // nf
    return pl.pallas_call(
        functools.partial(_ffn_body, alpha=alpha), name="ffn",
        grid=(n // tm, nf),
        in_specs=[pl.BlockSpec((tm, d), lambda i, f: (i, 0)),
                  pl.BlockSpec((d, tf), lambda i, f: (0, f)),
                  pl.BlockSpec((d, tf), lambda i, f: (0, f)),
                  pl.BlockSpec((tf, d), lambda i, f: (f, 0)),
                  pl.BlockSpec((1, d), lambda i, f: (0, 0)),
                  pl.BlockSpec((1, d), lambda i, f: (0, 0))],
        out_specs=pl.BlockSpec((tm, d), lambda i, f: (i, 0)),
        out_shape=jax.ShapeDtypeStruct((n, d), F32),
        scratch_shapes=[pltpu.VMEM((tm, d), F32)],
        compiler_params=_cparams(("parallel", "arbitrary")),
    )(x, w1, w3, w2, g, b)


def _gather_pages(pt_ref, n_pages, srcs, sem, layer):
    b = pl.program_id(0)
    slot = b & 1

    def copies(bb, sl):
        return [pltpu.make_async_copy(hbm.at[layer, pt_ref[bb, j]], buf.at[sl, :, pl.ds(PAGE * j, PAGE)], sem.at[a, sl])
                for a, (hbm, buf) in enumerate(srcs) for j in range(n_pages)]

    @pl.when(b == 0)
    def _():
        for c in copies(0, 0):
            c.start()

    @pl.when(b + 1 < pl.num_programs(0))
    def _():
        for c in copies(b + 1, 1 - slot):
            c.start()

    for c in copies(b, slot):
        c.wait()
    return slot


def _key_chunks(n):
    per = max(d for d in range(1, 17) if n % d == 0)
    return [(PAGE * per * c, PAGE * per) for c in range(n // per)]


def _stack_heads(q):
    return jnp.concatenate([q[:, LANES * h:LANES * (h + 1)] for h in range(8)], axis=0)


def _scmp_body(pt_ref, qu_ref, ck_hbm, cv_hbm, pak_ref, pbk_ref, wak_ref, wbk_ref, b1k_ref, w2k_ref,
               pav_ref, pbv_ref, wav_ref, wbv_ref, b1v_ref, w2v_ref, m_ref,
               o_ref, sel_ref, kbt, vbt, kbuf, vbuf, sem, *, layer, n_pages, past, nq):
    slot = _gather_pages(pt_ref, n_pages, [(ck_hbm, kbt), (cv_hbm, vbt)], sem, layer)
    for j in range(n_pages):
        kbuf[PAGE * j:PAGE * (j + 1), :] = kbt[slot, :, PAGE * j:PAGE * (j + 1)].T
        vbuf[PAGE * j:PAGE * (j + 1), :] = vbt[slot, :, PAGE * j:PAGE * (j + 1)].T
    n16 = past // CMP_STRIDE
    kcc = _compress(kbuf, n16, pak_ref[...], pbk_ref[...], wak_ref, wbk_ref, b1k_ref[...], w2k_ref).astype(BF16)
    vcc = _compress(vbuf, n16, pav_ref[...], pbv_ref[...], wav_ref, wbv_ref, b1v_ref[...], w2v_ref).astype(BF16)
    qa = _stack_heads(qu_ref[...])
    rows = 8 * nq
    s = _dot_nt(qa, kcc)
    cidx = lax.broadcasted_iota(jnp.int32, (rows, n16), 1)
    qpos = past + (lax.broadcasted_iota(jnp.int32, (rows, n16), 0) & (nq - 1))
    vis = (CMP_STRIDE * cidx + CMP_LEN - 1 <= qpos) & (cidx < n16 - 1)
    s = jnp.where(vis, s, NEG)
    m = jnp.max(s, axis=-1, keepdims=True)
    p = jnp.where(vis, jnp.exp2(s - m), 0.0)
    l = jnp.sum(p, axis=-1, keepdims=True)
    p = p / jnp.maximum(l, 1e-30)
    o_ref[...] = _dot(p.astype(BF16), vcc)
    imp = jnp.concatenate(
        [p[nq * 4 * g:nq * (4 * g + 1)] + p[nq * (4 * g + 1):nq * (4 * g + 2)]
         + p[nq * (4 * g + 2):nq * (4 * g + 3)] + p[nq * (4 * g + 3):nq * (4 * g + 4)] for g in range(NSA_G)], axis=0)
    isel = _split3_dot(imp, m_ref[...])
    nselp = isel.shape[1]
    jidx = lax.broadcasted_iota(jnp.int32, (NSA_G * nq, nselp), 1)
    qp = past + (lax.broadcasted_iota(jnp.int32, (NSA_G * nq, nselp), 0) & (nq - 1))
    cur = qp >> 6
    forced = (jidx == 0) | (jidx == cur) | (jidx == cur - 1)
    score = jnp.where(forced, FORCE_SCORE, isel)
    sel_ref[...] = jnp.where(jidx <= cur, score, -jnp.inf)


def _stopk_body(score_ref, sel_ref, *, k, past, nq):
    score = score_ref[...]
    nselp = score.shape[1]
    jidx = lax.broadcasted_iota(jnp.int32, score.shape, 1)
    cur = (past + (lax.broadcasted_iota(jnp.int32, score.shape, 0) & (nq - 1))) >> 6
    sel = jnp.zeros(score.shape, F32)
    for _ in range(k):
        mx = jnp.max(score, axis=-1, keepdims=True)
        first = jnp.min(jnp.where(score == mx, jidx, nselp), axis=-1, keepdims=True)
        hit = jidx == first
        sel = jnp.where(hit, 1.0, sel)
        score = jnp.where(hit, -jnp.inf, score)
    sel_ref[...] = jnp.where(jidx <= cur, sel, 0.0)


def _stopk(score, past, nq):
    nb, r, nselp = score.shape
    n_sel = -(-(past + nq) // SEL_BLK)
    full = pl.BlockSpec((nb * r, nselp), lambda i: (0, 0))
    sel = pl.pallas_call(
        functools.partial(_stopk_body, k=min(SEL_TOPN, n_sel), past=past, nq=nq), name="stopk",
        grid=(1,), in_specs=[full], out_specs=full,
        out_shape=jax.ShapeDtypeStruct((nb * r, nselp), F32),
        compiler_params=_cparams(("arbitrary",)),
    )(score.reshape(nb * r, nselp))
    return sel.reshape(nb, r, nselp)


def _scmp(pt, qu, cache_k, cache_v, cw, layer, nb, nq, past):
    n_pages = past // PAGE
    n16 = past // CMP_STRIDE
    n_sel = -(-(past + nq) // SEL_BLK)
    nselp = -(-n_sel // LANES) * LANES
    m = np.zeros((n16, nselp), np.float32)
    for j in range(n_sel):
        for c in range(4 * j - 1, 4 * j + 4):
            if 0 <= c < n16 - 1:
                m[c, j] = 1.0
    m = jnp.asarray(m, BF16)
    wlist = [cw[k] for k in ("pak", "pbk", "wak", "wbk", "b1k", "w2k", "pav", "pbv", "wav", "wbv", "b1v", "w2v")] + [m]
    const = lambda a: pl.BlockSpec(a.shape, lambda bi, pt_: (0,) * a.ndim)
    any_spec = pl.BlockSpec(memory_space=pl.ANY)
    grid_spec = pltpu.PrefetchScalarGridSpec(
        num_scalar_prefetch=1, grid=(nb,),
        in_specs=[pl.BlockSpec((nq, 1024), lambda bi, pt_: (bi, 0)), any_spec, any_spec] + [const(a) for a in wlist],
        out_specs=[pl.BlockSpec((None, 8 * nq, LANES), lambda bi, pt_: (bi, 0, 0)),
                   pl.BlockSpec((None, NSA_G * nq, nselp), lambda bi, pt_: (bi, 0, 0))],
        scratch_shapes=[pltpu.VMEM((2, LANES, past), F32), pltpu.VMEM((2, LANES, past), F32),
                        pltpu.VMEM((past, LANES), F32), pltpu.VMEM((past, LANES), F32), pltpu.SemaphoreType.DMA((2, 2))])
    return pl.pallas_call(
        functools.partial(_scmp_body, layer=layer, n_pages=n_pages, past=past, nq=nq), name="scmp",
        grid_spec=grid_spec,
        out_shape=[jax.ShapeDtypeStruct((nb, 8 * nq, LANES), F32), jax.ShapeDtypeStruct((nb, NSA_G * nq, nselp), F32)],
        compiler_params=_cparams(("arbitrary",)),
    )(pt, qu, cache_k, cache_v, *wlist)


def _softmax_rows(s):
    m = jnp.max(s, axis=-1, keepdims=True)
    p = jnp.exp2(s - m)
    return p, jnp.sum(p, axis=-1, keepdims=True)


def _sselwin_body(pt_ref, qr_ref, sel_ref, skn_ref, svn_ref, wkn_ref, wvn_ref, wks_ref, wvs_ref, e_ref, sk_hbm, sv_hbm,
                  osel_ref, owin_ref, kbuf, vbuf, sem, *, layer, n_pages, past, nq):
    ktot = past + PAGE
    rows = 8 * nq
    slot = _gather_pages(pt_ref, n_pages, [(sk_hbm, kbuf), (sv_hbm, vbuf)], sem, layer)
    pad = jnp.zeros((PAGE - nq, LANES), F32)
    kbuf[slot, :, pl.ds(past, PAGE)] = jnp.concatenate([skn_ref[...], pad], axis=0).T
    vbuf[slot, :, pl.ds(past, PAGE)] = jnp.concatenate([svn_ref[...], pad], axis=0).T
    chunks = _key_chunks(n_pages + 1)
    qa = _stack_heads(qr_ref[...])
    s = jnp.concatenate([_dot(qa, kbuf[slot, :, pl.ds(c0, cn)].astype(BF16)) for c0, cn in chunks], axis=1)
    se = _dot(sel_ref[...].astype(BF16), e_ref[...])
    se = jnp.concatenate([se[nq * g:nq * (g + 1)] for g in range(NSA_G) for _ in range(NSA_HPG)], axis=0)
    kpos = lax.broadcasted_iota(jnp.int32, (rows, ktot), 1)
    qpos = past + (lax.broadcasted_iota(jnp.int32, (rows, ktot), 0) & (nq - 1))
    s = jnp.where((se > 0.5) & (kpos <= qpos), s, NEG)
    p, l = _softmax_rows(s)
    pb = p.astype(BF16)
    o = _dot_nt(pb[:, 0:chunks[0][1]], vbuf[slot, :, pl.ds(0, chunks[0][1])].astype(BF16))
    for c0, cn in chunks[1:]:
        o = o + _dot_nt(pb[:, c0:c0 + cn], vbuf[slot, :, pl.ds(c0, cn)].astype(BF16))
    osel_ref[...] = o / l
    wb = wks_ref.shape[0]
    kw = jnp.concatenate([wks_ref[...], wkn_ref[...], pad], axis=0).astype(BF16)
    vw = jnp.concatenate([wvs_ref[...], wvn_ref[...], pad], axis=0).astype(BF16)
    s = _dot_nt(qa, kw)
    i = lax.broadcasted_iota(jnp.int32, (rows, wb + PAGE), 1)
    kp = jnp.where(i < wb, past - wb + i, past + i - wb)
    qp = past + (lax.broadcasted_iota(jnp.int32, (rows, wb + PAGE), 0) & (nq - 1))
    dlt = qp - kp
    s = jnp.where((dlt >= 0) & (dlt < WINDOW) & (i < wb + nq), s, NEG)
    p, l = _softmax_rows(s)
    owin_ref[...] = _dot(p.astype(BF16), vw) / l


def _sselwin(pt, qr, sel, skn, svn, wkn, wvn, wks, wvs, cache_k, cache_v, layer, nb, nq, past):
    n_pages = past // PAGE
    ktot = past + PAGE
    nselp = sel.shape[-1]
    e = (np.arange(ktot)[None, :] // SEL_BLK == np.arange(nselp)[:, None]).astype(np.float32)
    e = jnp.asarray(e, BF16)
    wb = wks.shape[1]
    any_spec = pl.BlockSpec(memory_space=pl.ANY)
    new = pl.BlockSpec((nq, LANES), lambda bi, pt_: (bi, 0))
    grid_spec = pltpu.PrefetchScalarGridSpec(
        num_scalar_prefetch=1, grid=(nb,),
        in_specs=[pl.BlockSpec((nq, 1024), lambda bi, pt_: (bi, 0)),
                  pl.BlockSpec((None, NSA_G * nq, nselp), lambda bi, pt_: (bi, 0, 0)),
                  new, new, new, new,
                  pl.BlockSpec((None, wb, LANES), lambda bi, pt_: (bi, 0, 0)),
                  pl.BlockSpec((None, wb, LANES), lambda bi, pt_: (bi, 0, 0)),
                  pl.BlockSpec(e.shape, lambda bi, pt_: (0, 0)), any_spec, any_spec],
        out_specs=[pl.BlockSpec((None, 8 * nq, LANES), lambda bi, pt_: (bi, 0, 0)),
                   pl.BlockSpec((None, 8 * nq, LANES), lambda bi, pt_: (bi, 0, 0))],
        scratch_shapes=[pltpu.VMEM((2, LANES, ktot), F32), pltpu.VMEM((2, LANES, ktot), F32), pltpu.SemaphoreType.DMA((2, 2))])
    return pl.pallas_call(
        functools.partial(_sselwin_body, layer=layer, n_pages=n_pages, past=past, nq=nq), name="sselwin",
        grid_spec=grid_spec,
        out_shape=[jax.ShapeDtypeStruct((nb, 8 * nq, LANES), F32)] * 2,
        compiler_params=_cparams(("arbitrary",)),
    )(pt, qr, sel, skn, svn, wkn, wvn, wks, wvs, e, cache_k, cache_v)


def _sdiff_body(pt_ref, dq_ref, dkn_ref, dvn_ref, lq1_ref, lk1_ref, lq2_ref, lk2_ref, sg_ref, dk_hbm, dv_hbm,
                o_ref, kbuf, vbuf, sem, *, layer, n_pages, past, nq, lam_init):
    ktot = past + PAGE
    rows = 8 * nq
    slot = _gather_pages(pt_ref, n_pages, [(dk_hbm, kbuf), (dv_hbm, vbuf)], sem, layer)
    pad = jnp.zeros((PAGE - nq, 256), F32)
    kbuf[slot, :, pl.ds(past, PAGE)] = jnp.concatenate([dkn_ref[...], pad], axis=0).T
    vbuf[slot, :, pl.ds(past, PAGE)] = jnp.concatenate([dvn_ref[...], pad], axis=0).T
    chunks = _key_chunks(n_pages + 1)
    q = dq_ref[...]
    zero = jnp.zeros((nq, LANES), BF16)
    blocks = []
    for a in range(8):
        chunk = q[:, LANES * a:LANES * (a + 1)]
        blocks.append(jnp.concatenate([chunk, zero] if a < 4 else [zero, chunk], axis=1))
    qa = jnp.concatenate(blocks, axis=0)
    s = jnp.concatenate([_dot(qa, kbuf[slot, :, pl.ds(c0, cn)].astype(BF16)) for c0, cn in chunks], axis=1)
    kpos = lax.broadcasted_iota(jnp.int32, (rows, ktot), 1)
    qpos = past + (lax.broadcasted_iota(jnp.int32, (rows, ktot), 0) & (nq - 1))
    s = jnp.where(kpos <= qpos, s, NEG)
    p, l = _softmax_rows(s)
    pb = p.astype(BF16)
    o = _dot_nt(pb[:, 0:chunks[0][1]], vbuf[slot, :, pl.ds(0, chunks[0][1])].astype(BF16))
    for c0, cn in chunks[1:]:
        o = o + _dot_nt(pb[:, c0:c0 + cn], vbuf[slot, :, pl.ds(c0, cn)].astype(BF16))
    o = o / l
    lam = (jnp.exp(jnp.sum(lq1_ref[...] * lk1_ref[...], keepdims=True))
           - jnp.exp(jnp.sum(lq2_ref[...] * lk2_ref[...], keepdims=True)) + lam_init)
    lane = lax.broadcasted_iota(jnp.int32, (nq, 256), 1)
    out = jnp.zeros((nq, 256), F32)
    for h in range(DIFF_H):
        d = o[2 * nq * h:2 * nq * h + nq] - lam * o[2 * nq * h + nq:2 * nq * (h + 1)]
        inh = (lane >> 6) == h
        ms = jnp.sum(jnp.where(inh, d * d, 0.0), axis=-1, keepdims=True) * (1.0 / DIFF_DV)
        out = out + jnp.where(inh, d * lax.rsqrt(ms + LN_EPS), 0.0)
    o_ref[...] = out * sg_ref[...] * (1.0 - lam_init)


def _sdiff(pt, dq, dkn, dvn, lams, sg_row, cache_k, cache_v, layer, nb, nq, past, lam_init):
    n_pages = past // PAGE
    ktot = past + PAGE
    any_spec = pl.BlockSpec(memory_space=pl.ANY)
    new = pl.BlockSpec((nq, 256), lambda bi, pt_: (bi, 0))
    small = [pl.BlockSpec(a.shape, lambda bi, pt_: (0, 0)) for a in (*lams, sg_row)]
    grid_spec = pltpu.PrefetchScalarGridSpec(
        num_scalar_prefetch=1, grid=(nb,),
        in_specs=[pl.BlockSpec((nq, 1024), lambda bi, pt_: (bi, 0)), new, new] + small + [any_spec, any_spec],
        out_specs=pl.BlockSpec((nq, 256), lambda bi, pt_: (bi, 0)),
        scratch_shapes=[pltpu.VMEM((2, 256, ktot), F32), pltpu.VMEM((2, 256, ktot), F32), pltpu.SemaphoreType.DMA((2, 2))])
    return pl.pallas_call(
        functools.partial(_sdiff_body, layer=layer, n_pages=n_pages, past=past, nq=nq, lam_init=lam_init), name="sdiff",
        grid_spec=grid_spec,
        out_shape=jax.ShapeDtypeStruct((nb * nq, 256), F32),
        compiler_params=_cparams(("arbitrary",)),
    )(pt, dq, dkn, dvn, *lams, sg_row, cache_k, cache_v)


def _prep_w_in(w):
    pts = np.cumsum([256, 256, 512, 128, 128, 128, 128, 128, 128, 24, 256, 256, 256])[:-1].tolist()
    ca, cg, nq, ck, cv, sk, sv, wk, wv, gt, dq, dk, dv = jnp.split(w, pts, axis=1)
    gtp = jnp.pad(gt, ((0, 0), (0, LANES - gt.shape[1])))
    return jnp.concatenate([ca, cg, nq * (NSA_DH ** -0.5 * LOG2E), ck, cv, sk, sv, wk, wv, gtp, dq, dk, dv], axis=1).astype(BF16)


def _rope_tables(pos):
    out = []
    lane = np.arange(LANES)
    for half in (32, 16):
        inv = ROPE_THETA ** (-jnp.arange(half, dtype=F32) / half)
        ang = pos.astype(F32)[:, None] * inv[None, :]
        idx = lane % half
        sign = jnp.asarray(np.where(lane % (2 * half) < half, -1.0, 1.0), F32)
        out += [jnp.cos(ang)[:, idx], jnp.sin(ang)[:, idx] * sign[None, :]]
    return out


def _prep_cmp(pe, w1, b1, w2):
    def halves(x):
        res = []
        for part in (x[:16 * NSA_DH], x[16 * NSA_DH:]):
            p4 = part.reshape(CMP_STRIDE, 1, NSA_DH, 1, -1)
            eye = jnp.eye(NSA_G, dtype=F32)[None, :, None, :, None]
            res.append((p4 * eye).reshape(CMP_STRIDE * NSA_G * NSA_DH, NSA_G * part.shape[-1]))
        return res
    wa, wb = halves(w1)
    pea = jnp.tile(pe[:16, None, :], (1, NSA_G, 1)).reshape(1, -1)
    peb = jnp.tile(pe[16:, None, :], (1, NSA_G, 1)).reshape(1, -1)
    w2bd = (w2[None, :, None, :] * jnp.eye(NSA_G, dtype=F32)[:, None, :, None]).reshape(NSA_G * CMP_HID, NSA_G * NSA_DH)
    return pea, peb, wa.astype(BF16), wb.astype(BF16), jnp.tile(b1, NSA_G)[None, :], w2bd.astype(BF16)


def _gate_expand():
    e = np.zeros((LANES, 3 * 512), np.float32)
    for h in range(NSA_H):
        for br in range(3):
            e[3 * h + br, 512 * br + 64 * h:512 * br + 64 * (h + 1)] = 1.0
    return jnp.asarray(e, BF16)


def _rows_to_tokens(o, nb, nq):
    o6 = o.reshape(nb, NSA_G, NSA_HPG, nq, NSA_G, NSA_DH)
    pick = jnp.stack([o6[:, g, :, :, g, :] for g in range(NSA_G)], axis=1)
    return pick.transpose(0, 3, 1, 2, 4).reshape(nb * nq, NSA_H * NSA_DH)


def _prompt_layer(x, lw, tabs, nb, t, lam_init, alpha):
    n = nb * t
    (u, qu, qr, gt, dq, ck, cv, skb, wkb, dkb, svb, wvb, dvb,
     ckt, cvt, skt, svt, wkt, wvt, dkt, dvt) = _inproj(x, lw["w_in"], tabs, nb, t, 256, True)
    ext = jnp.pad(u.reshape(nb, t, 256), ((0, 0), (CONV_PAD, 0), (0, 0)))
    yc = _conv(ext, lw["dw_w"], lw["dw_b"], lw["cln_g"], lw["cln_b"], t, 256).reshape(n, 256)
    ocmp, sel = _pcmp(qu, ck, cv, lw["cmp"], nb, t)
    osel = _flash("sel", qr, skb, svb, nb, t, extra=(sel,))
    owin = _win(qr, wkb, wvb, nb, t)
    odiff = _flash("diff", dq, dkb, dvb, nb, t, extra=lw["lams"] + (lw["sg_col"],), lam_init=lam_init)
    x1 = _outproj(x, yc, ocmp, osel, owin, gt, odiff, lw["gate_e"], lw["w_out"], lw["ln1_g"], lw["ln1_b"], alpha, 256)
    x2 = _ffn(x1, lw["w1"], lw["w3"], lw["w2"], lw["ln2_g"], lw["ln2_b"], alpha, 512, 2)
    nk = min(WINDOW, t)
    rows_major = lambda a: a.reshape(nb, a.shape[1] // 64, 64, a.shape[2]).transpose(0, 3, 1, 2)
    news = (rows_major(ckt), rows_major(cvt), rows_major(skt), rows_major(svt), rows_major(dkt), rows_major(dvt),
            rows_major(wkt[:, :, t - nk:]), rows_major(wvt[:, :, t - nk:]),
            u.reshape(nb, t, 256)[:, t - (CONV_W - 1):])
    return x2, news


def _sample_layer(x, lw, tabs, caches, states, pt, layer, nb, nq, past, lam_init, alpha):
    n = nb * nq
    (u, qu, qr, gt, dq, ck, cv, sk, sv, wk, wv, dk, dv) = _inproj(x, lw["w_in"], tabs, nb, nq, n, False)
    c_cmp_k, c_cmp_v, c_sel_k, c_sel_v, c_diff_k, c_diff_v = caches
    st_wk, st_wv, st_conv = states
    ext = jnp.concatenate([jnp.zeros((nb, CONV_PAD - (CONV_W - 1), 256), F32), st_conv, u.reshape(nb, nq, 256)], axis=1)
    yc = _conv(ext, lw["dw_w"], lw["dw_b"], lw["cln_g"], lw["cln_b"], nq, nq).reshape(n, 256)
    ocmp, score = _scmp(pt, qu, c_cmp_k, c_cmp_v, lw["cmp"], layer, nb, nq, past)
    sel = _stopk(score, past, nq)
    wb = st_wk.shape[1]
    osel, owin = _sselwin(pt, qr, sel, sk, sv, wk, wv, st_wk.reshape(nb, wb, LANES), st_wv.reshape(nb, wb, LANES),
                          c_sel_k, c_sel_v, layer, nb, nq, past)
    odiff = _sdiff(pt, dq, dk, dv, lw["lams"], lw["sg_row"], c_diff_k, c_diff_v, layer, nb, nq, past, lam_init)
    x1 = _outproj(x, yc, _rows_to_tokens(ocmp, nb, nq), _rows_to_tokens(osel, nb, nq), _rows_to_tokens(owin, nb, nq),
                  gt, odiff, lw["gate_e"], lw["w_out"], lw["ln1_g"], lw["ln1_b"], alpha, n)
    x2 = _ffn(x1, lw["w1"], lw["w3"], lw["w2"], lw["ln2_g"], lw["ln2_b"], alpha, n, 2)
    new_wk = jnp.concatenate([st_wk, wk.reshape(nb, nq, NSA_G, NSA_DH)], axis=1)[:, -wb:]
    new_wv = jnp.concatenate([st_wv, wv.reshape(nb, nq, NSA_G, NSA_DH)], axis=1)[:, -wb:]
    new_conv = jnp.concatenate([st_conv, u.reshape(nb, nq, 256)], axis=1)[:, -(CONV_W - 1):]
    news = (ck.reshape(nb, nq, NSA_G, NSA_DH), cv.reshape(nb, nq, NSA_G, NSA_DH),
            sk.reshape(nb, nq, NSA_G, NSA_DH), sv.reshape(nb, nq, NSA_G, NSA_DH),
            dk.reshape(nb, nq, DIFF_H, 2 * DIFF_DQK), dv.reshape(nb, nq, DIFF_H, DIFF_DV),
            new_wk, new_wv, new_conv)
    return x2, news


def kernel(x_prompt, x_sample, cache_nsa_cmp_k, cache_nsa_cmp_v, cache_nsa_sel_k, cache_nsa_sel_v, cache_diff_k, cache_diff_v, state_nsa_win_k, state_nsa_win_v, state_conv, page_table, w_in, conv_dw_w, conv_dw_b, conv_ln_g, conv_ln_b, cmp_pe_k, cmp_w1_k, cmp_b1_k, cmp_w2_k, cmp_pe_v, cmp_w1_v, cmp_b1_v, cmp_w2_v, diff_lq1, diff_lk1, diff_lq2, diff_lk2, diff_subln_g, w_out, ln1_g, ln1_b, ln2_g, ln2_b, ffn_w1, ffn_w3, ffn_w2):
    nb, t, d = x_prompt.shape
    sb, nq, _ = x_sample.shape
    depth = w_in.shape[0]
    n_pool = cache_nsa_cmp_k.shape[1]
    past = page_table.shape[1] * PAGE
    alpha = (2 * depth) ** 0.25
    tabs_p = _rope_tables(jnp.arange(t, dtype=jnp.int32))
    tabs_s = _rope_tables(jnp.tile(past + jnp.arange(nq, dtype=jnp.int32), sb))
    as_pages = lambda c: c.transpose(0, 1, 3, 4, 2).reshape(depth, n_pool, c.shape[3] * c.shape[4], PAGE)
    caches = tuple(as_pages(c) for c in (cache_nsa_cmp_k, cache_nsa_cmp_v, cache_nsa_sel_k, cache_nsa_sel_v,
                                         cache_diff_k, cache_diff_v))
    gate_e = _gate_expand()
    xp = x_prompt.reshape(nb * t, d)
    xs = x_sample.reshape(sb * nq, d)
    outs_p, outs_s = [], []
    for l in range(depth):
        ck = _prep_cmp(cmp_pe_k[l], cmp_w1_k[l], cmp_b1_k[l], cmp_w2_k[l])
        cv = _prep_cmp(cmp_pe_v[l], cmp_w1_v[l], cmp_b1_v[l], cmp_w2_v[l])
        names = ("pa", "pb", "wa", "wb", "b1", "w2")
        cmpw = {n_ + "k": a for n_, a in zip(names, ck)}
        cmpw.update({n_ + "v": a for n_, a in zip(names, cv)})
        lw = dict(
            w_in=_prep_w_in(w_in[l]), dw_w=conv_dw_w[l], dw_b=conv_dw_b[l][None], cln_g=conv_ln_g[l][None],
            cln_b=conv_ln_b[l][None], cmp=cmpw,
            lams=(diff_lq1[l][None], diff_lk1[l][None], diff_lq2[l][None], diff_lk2[l][None]),
            sg_col=diff_subln_g[l][:, None], sg_row=jnp.tile(diff_subln_g[l], DIFF_H)[None],
            gate_e=gate_e, w_out=w_out[l].astype(BF16), ln1_g=ln1_g[l][None], ln1_b=ln1_b[l][None],
            ln2_g=ln2_g[l][None], ln2_b=ln2_b[l][None],
            w1=ffn_w1[l].astype(BF16), w3=ffn_w3[l].astype(BF16), w2=ffn_w2[l].astype(BF16))
        lam_init = 0.8 - 0.6 * math.exp(-0.3 * l)
        xp, new_p = _prompt_layer(xp, lw, tabs_p, nb, t, lam_init, alpha)
        xs, new_s = _sample_layer(xs, lw, tabs_s, caches, (state_nsa_win_k[l], state_nsa_win_v[l], state_conv[l]),
                                  page_table, l, sb, nq, past, lam_init, alpha)
        outs_p.append(new_p)
        outs_s.append(new_s)
    stk_p = [jnp.stack([o[i] for o in outs_p]) for i in range(9)]
    stk_s = [jnp.stack([o[i] for o in outs_s]) for i in range(9)]
    return (xp.reshape(nb, t, d), xs.reshape(sb, nq, d), *stk_p, *stk_s)
```
